```python
import jax
import jax.numpy as jnp
from jax import lax
import numpy as np

D_MODEL = 4096
BATCH = 8
SEQ = 2048
DEPTH = 2

GRID_W = 64
CTX_LEN = 256
HEAD_DIM = 128
ATTN_WIDTH = D_MODEL // 2
N_Q_HEADS = ATTN_WIDTH // HEAD_DIM
N_KV_HEADS = N_Q_HEADS // 4
KV_GROUP = N_Q_HEADS // N_KV_HEADS
KV_WIDTH = N_KV_HEADS * HEAD_DIM
Q_BLOCK = 128
ROPE_THETA = 10000.0
ROPE_AXIS_DIM = HEAD_DIM // 2
HGRN_WIDTH = D_MODEL - ATTN_WIDTH
HGRN_HEADS = HGRN_WIDTH // HEAD_DIM
HGRN_EXPAND = HEAD_DIM
HGRN_KEY_WIDTH = HGRN_HEADS * HGRN_EXPAND
HGRN_CHUNK = 32
AB_SPLIT_SIZES = (ATTN_WIDTH, KV_WIDTH, KV_WIDTH, HGRN_KEY_WIDTH, HGRN_KEY_WIDTH, HGRN_KEY_WIDTH, HGRN_WIDTH, HGRN_WIDTH)
AB_IN = ATTN_WIDTH + 2 * KV_WIDTH + 3 * HGRN_KEY_WIDTH + 2 * HGRN_WIDTH
AB_OUT = ATTN_WIDTH + HGRN_WIDTH
CONV_WIDTH = 3
D_FF = 2 * D_MODEL
N_MOD = 9
N_EVEN = (DEPTH + 1) // 2
N_ODD = DEPTH // 2
EPS = 1e-6

kernel_name = 'hybrid_diffusion_gqa_hgrn2_shortconv_macaron'


def rms_norm(x, gain):
    xf = x.astype(jnp.float32)
    y = xf * lax.rsqrt(jnp.mean(xf * xf, axis=-1, keepdims=True) + EPS)
    return (y * gain.astype(jnp.float32)).astype(x.dtype)


def adaln_in(u, gain, shift, scale):
    return rms_norm(u, gain) * (1.0 + scale) + shift


def swiglu(h, w1, w2):
    gate, up = jnp.split(h @ w1, 2, axis=-1)
    return (jax.nn.silu(gate) * up) @ w2


def axial_rope_tables(n_tokens):
    rows = n_tokens // GRID_W
    row_ids = jnp.repeat(jnp.arange(rows, dtype=jnp.float32), GRID_W)
    col_ids = jnp.tile(jnp.arange(GRID_W, dtype=jnp.float32), rows)
    inv_freq = ROPE_THETA ** (-jnp.arange(0, ROPE_AXIS_DIM, 2, dtype=jnp.float32) / ROPE_AXIS_DIM)
    ang = jnp.stack([row_ids[:, None] * inv_freq, col_ids[:, None] * inv_freq], axis=1)
    return jnp.cos(ang), jnp.sin(ang)


def apply_axial_rope(x, cos, sin):
    b, t, hh, d = x.shape
    xr = x.reshape(b, t, hh, 2, 2, ROPE_AXIS_DIM // 2)
    x1 = xr[..., 0, :]
    x2 = xr[..., 1, :]
    cs = cos[None, :, None].astype(x.dtype)
    sn = sin[None, :, None].astype(x.dtype)
    out = jnp.stack([x1 * cs - x2 * sn, x2 * cs + x1 * sn], axis=-2)
    return out.reshape(b, t, hh, d)


def attend(q, k, v):
    b, t = q.shape[:2]
    n_blk = t // Q_BLOCK
    qb = q.reshape(b, n_blk, Q_BLOCK, N_KV_HEADS, KV_GROUP, HEAD_DIM).transpose(1, 0, 2, 3, 4, 5)
    scale = HEAD_DIM ** -0.5

    def one_block(q_blk):
        s = jnp.einsum('bqkgd,bskd->bkgqs', q_blk, k).astype(jnp.float32) * scale
        p = jax.nn.softmax(s, axis=-1).astype(v.dtype)
        return jnp.einsum('bkgqs,bskd->bqkgd', p, v)

    o = lax.map(one_block, qb)
    return o.transpose(1, 0, 2, 3, 4, 5).reshape(b, t, N_Q_HEADS * HEAD_DIM)


def hgrn_heads(a):
    return a.astype(jnp.float32).reshape(a.shape[0], a.shape[1], HGRN_HEADS, -1)


def hgrn_key_decay(z, lb):
    zf = z.astype(jnp.float32)
    log_f = jnp.log(lb + (1.0 - lb) * jax.nn.sigmoid(zf))
    k = (1.0 - lb) * jax.nn.sigmoid(-zf)
    return hgrn_heads(k), hgrn_heads(log_f)


def gla_chunk_scan(q, k, v, log_f, s0):
    b, t, hh, dk = q.shape
    dv = v.shape[-1]
    L = HGRN_CHUNK
    n_c = t // L

    def to_chunks(a):
        return a.reshape(b, n_c, L, hh, a.shape[-1]).transpose(1, 0, 3, 2, 4)

    mask = jnp.tril(jnp.ones((L, L), dtype=bool))

    def step(state, inp):
        qc, kc, vc, gc = inp
        cum = jnp.cumsum(gc, axis=-2)
        ref = cum[..., L // 2:L // 2 + 1, :]
        end = cum[..., L - 1:, :]
        attn = jnp.einsum('bhld,bhmd->bhlm', qc * jnp.exp(cum - ref), kc * jnp.exp(ref - cum))
        attn = jnp.where(mask, attn, 0.0)
        o = jnp.einsum('bhlm,bhme->bhle', attn, vc) + jnp.einsum('bhld,bhde->bhle', qc * jnp.exp(cum), state)
        new_state = jnp.exp(end[..., 0, :])[..., None] * state + jnp.einsum('bhld,bhle->bhde', kc * jnp.exp(end - cum), vc)
        return new_state, o

    s_fin, o = lax.scan(step, s0, (to_chunks(q), to_chunks(k), to_chunks(v), to_chunks(log_f)))
    return o.transpose(1, 0, 3, 2, 4).reshape(b, t, hh, dv), s_fin


def gla_final_state(k, v, log_f):
    cum = jnp.cumsum(log_f, axis=1)
    return jnp.einsum('bthd,bthe->bhde', k * jnp.exp(cum[:, -1:] - cum), v)


def flip_seq(a, direction):
    return jnp.flip(a, axis=1) if direction == 1 else a


def hgrn_bidirectional(q, v, z_dirs, qc, vc, zc_dirs, lb, need_ctx_out):
    b = q.shape[0]
    s_zero = jnp.zeros((b, HGRN_HEADS, HGRN_EXPAND, v.shape[-1]), jnp.float32)
    o_lat = []
    o_ctx = []
    for d in range(2):
        k, lf = hgrn_key_decay(z_dirs[d], lb[d])
        kc, lfc = hgrn_key_decay(zc_dirs[d], lb[d])
        if need_ctx_out:
            oc, s_c = gla_chunk_scan(flip_seq(qc, d), flip_seq(kc, d), flip_seq(vc, d), flip_seq(lfc, d), s_zero)
            o_ctx.append(flip_seq(oc, d))
        else:
            s_c = gla_final_state(flip_seq(kc, d), flip_seq(vc, d), flip_seq(lfc, d))
        ol, _ = gla_chunk_scan(flip_seq(q, d), flip_seq(k, d), flip_seq(v, d), flip_seq(lf, d), s_c)
        o_lat.append(flip_seq(ol, d))
    o_ctx_sum = (o_ctx[0] + o_ctx[1]) if need_ctx_out else None
    return o_lat[0] + o_lat[1], o_ctx_sum


def hgrn_gate_out(o, g, gain):
    b, t = o.shape[:2]
    return rms_norm(o, gain).reshape(b, t, -1).astype(g.dtype) * jax.nn.silu(g)


def mixer_ab(h, hc, w_in, w_out, q_gain, k_gain, lb, out_gain, cos, sin, need_ctx_out):
    split_idx = np.cumsum(AB_SPLIT_SIZES)[:-1].tolist()
    qa, ka, va, qb, zf, zb, ib, gb = jnp.split(h @ w_in, split_idx, axis=-1)
    qa_c, ka_c, va_c, qb_c, zf_c, zb_c, ib_c, gb_c = jnp.split(hc @ w_in, split_idx, axis=-1)
    b, t = h.shape[:2]
    tc = hc.shape[1]
    q = apply_axial_rope(rms_norm(qa.reshape(b, t, N_Q_HEADS, HEAD_DIM), q_gain), cos, sin)
    k = apply_axial_rope(rms_norm(ka.reshape(b, t, N_KV_HEADS, HEAD_DIM), k_gain), cos, sin)
    v = va.reshape(b, t, N_KV_HEADS, HEAD_DIM)
    k_c = rms_norm(ka_c.reshape(b, tc, N_KV_HEADS, HEAD_DIM), k_gain)
    v_c = va_c.reshape(b, tc, N_KV_HEADS, HEAD_DIM)
    o_attn = attend(q, jnp.concatenate([k, k_c], axis=1), jnp.concatenate([v, v_c], axis=1))
    q_h = hgrn_heads(jax.nn.silu(qb))
    v_h = hgrn_heads(ib)
    v_hc = hgrn_heads(ib_c)
    q_hc = hgrn_heads(jax.nn.silu(qb_c)) if need_ctx_out else None
    o_rec, o_rec_c = hgrn_bidirectional(q_h, v_h, (zf, zb), q_hc, v_hc, (zf_c, zb_c), lb, need_ctx_out)
    y = jnp.concatenate([o_attn, hgrn_gate_out(o_rec, gb, out_gain)], axis=-1) @ w_out
    if not need_ctx_out:
        return y, None
    q_c = rms_norm(qa_c.reshape(b, tc, N_Q_HEADS, HEAD_DIM), q_gain)
    o_attn_c = attend(q_c, k_c, v_c)
    y_c = jnp.concatenate([o_attn_c, hgrn_gate_out(o_rec_c, gb_c, out_gain)], axis=-1) @ w_out
    return y, y_c


def mixer_conv(h, w_in, conv_w, w_out):
    b_gate, c_gate, u = jnp.split(h @ w_in, 3, axis=-1)
    u = c_gate * u
    t = u.shape[1]
    pad = CONV_WIDTH // 2
    up = jnp.pad(u, ((0, 0), (pad, pad), (0, 0)))
    y = up[:, 0:t] * conv_w[0]
    for j in range(1, CONV_WIDTH):
        y = y + up[:, j:j + t] * conv_w[j]
    return (b_gate * y) @ w_out


def setup_inputs(seed: int = 0) -> dict:
    key = jax.random.key(seed)
    ks = jax.random.split(key, 18)

    def nrm(k, shape, scale):
        return jax.random.normal(k, shape, jnp.float32) * scale

    return {
        'x': nrm(ks[0], (BATCH, SEQ, D_MODEL), 1.0),
        'c': nrm(ks[1], (BATCH, D_MODEL), 1.0),
        'ctx': nrm(ks[2], (BATCH, CTX_LEN, D_MODEL), 1.0),
        'c_ctx': nrm(ks[3], (D_MODEL,), 1.0),
        'mod_w': nrm(ks[4], (DEPTH, D_MODEL, N_MOD * D_MODEL), 0.5 * D_MODEL ** -0.5),
        'mod_b': nrm(ks[5], (DEPTH, N_MOD * D_MODEL), 0.02),
        'norm_g': 1.0 + nrm(ks[6], (DEPTH, 3, D_MODEL), 0.05),
        'ffn_w1': nrm(ks[7], (DEPTH, 2, D_MODEL, 2 * D_FF), D_MODEL ** -0.5),
        'ffn_w2': nrm(ks[8], (DEPTH, 2, D_FF, D_MODEL), D_FF ** -0.5),
        'ab_w_in': nrm(ks[9], (N_EVEN, D_MODEL, AB_IN), D_MODEL ** -0.5),
        'ab_w_out': nrm(ks[10], (N_EVEN, AB_OUT, D_MODEL), AB_OUT ** -0.5),
        'attn_q_gain': 1.0 + nrm(ks[11], (N_EVEN, HEAD_DIM), 0.05),
        'attn_k_gain': 1.0 + nrm(ks[12], (N_EVEN, HEAD_DIM), 0.05),
        'hgrn_lb_logits': nrm(ks[13], (2, DEPTH + 1, HGRN_KEY_WIDTH), 0.1),
        'hgrn_out_gain': 1.0 + nrm(ks[14], (N_EVEN, HGRN_WIDTH // HGRN_HEADS), 0.05),
        'conv_w_in': nrm(ks[15], (N_ODD, D_MODEL, 3 * D_MODEL), D_MODEL ** -0.5),
        'conv_w': nrm(ks[16], (N_ODD, CONV_WIDTH, D_MODEL), CONV_WIDTH ** -0.5),
        'conv_w_out': nrm(ks[17], (N_ODD, D_MODEL, D_MODEL), D_MODEL ** -0.5),
    }


def reference(x, c, ctx, c_ctx, mod_w, mod_b, norm_g, ffn_w1, ffn_w2, ab_w_in, ab_w_out, attn_q_gain, attn_k_gain, hgrn_lb_logits, hgrn_out_gain, conv_w_in, conv_w, conv_w_out):
    cos, sin = axial_rope_tables(x.shape[1])
    lb_table = jnp.cumsum(jax.nn.softmax(hgrn_lb_logits.astype(jnp.float32), axis=1), axis=1)
    silu_c = jax.nn.silu(c)
    silu_cc = jax.nn.silu(c_ctx)[None]
    xc = ctx
    for layer in range(DEPTH):
        last = layer == DEPTH - 1
        even = layer % 2 == 0
        ctx_needed = even or not last
        g = norm_g[layer]
        mod = (silu_c @ mod_w[layer] + mod_b[layer]).reshape(-1, 1, N_MOD, D_MODEL)
        mod_c = (silu_cc @ mod_w[layer] + mod_b[layer]).reshape(1, 1, N_MOD, D_MODEL)
        x = x + 0.5 * mod[:, :, 2] * swiglu(adaln_in(x, g[0], mod[:, :, 0], mod[:, :, 1]), ffn_w1[layer, 0], ffn_w2[layer, 0])
        h = adaln_in(x, g[1], mod[:, :, 3], mod[:, :, 4])
        hc = None
        if ctx_needed:
            xc = xc + 0.5 * mod_c[:, :, 2] * swiglu(adaln_in(xc, g[0], mod_c[:, :, 0], mod_c[:, :, 1]), ffn_w1[layer, 0], ffn_w2[layer, 0])
            hc = adaln_in(xc, g[1], mod_c[:, :, 3], mod_c[:, :, 4])
        if even:
            e = layer // 2
            y, y_c = mixer_ab(h, hc, ab_w_in[e], ab_w_out[e], attn_q_gain[e], attn_k_gain[e], lb_table[:, layer], hgrn_out_gain[e], cos, sin, not last)
        else:
            o = layer // 2
            y = mixer_conv(h, conv_w_in[o], conv_w[o], conv_w_out[o])
            y_c = None if last else mixer_conv(hc, conv_w_in[o], conv_w[o], conv_w_out[o])
        x = x + mod[:, :, 5] * y
        x = x + 0.5 * mod[:, :, 8] * swiglu(adaln_in(x, g[2], mod[:, :, 6], mod[:, :, 7]), ffn_w1[layer, 1], ffn_w2[layer, 1])
        if not last:
            xc = xc + mod_c[:, :, 5] * y_c
            xc = xc + 0.5 * mod_c[:, :, 8] * swiglu(adaln_in(xc, g[2], mod_c[:, :, 6], mod_c[:, :, 7]), ffn_w1[layer, 1], ffn_w2[layer, 1])
    return x
```

```python
import functools

import numpy as np
import jax
import jax.numpy as jnp
from jax import lax
from jax.experimental import pallas as pl
from jax.experimental.pallas import tpu as pltpu

GRID_W = 64
HEAD_DIM = 128
KV_GROUP = 4
ROPE_THETA = 10000.0
ROPE_AXIS_DIM = HEAD_DIM // 2
HGRN_CHUNK = 32
N_MOD = 9
EPS = 1e-6

VMEM_LIMIT_BYTES = 56 * 1024 * 1024
HGRN_BLOCK = 256
NORM_ROWS = 64

BF16 = jnp.bfloat16
F32 = jnp.float32


def _params(*sem):
    return pltpu.CompilerParams(dimension_semantics=sem, vmem_limit_bytes=VMEM_LIMIT_BYTES)


def _pick(n, candidates):
    for c in candidates:
        if n % c == 0:
            return c
    return n


def _mod_kernel(c_ref, w_ref, b_ref, o_ref):
    a = jax.nn.silu(c_ref[...]).astype(BF16)
    o_ref[0] = jnp.dot(a, w_ref[0].astype(BF16), preferred_element_type=F32) + b_ref[0]


def _mod_all(cc, mod_w, mod_b):
    depth, d, n = mod_w.shape
    rows = cc.shape[0]
    tn = _pick(n, (512, 256, 128))
    return pl.pallas_call(
        _mod_kernel,
        grid=(depth, n // tn),
        in_specs=[
            pl.BlockSpec((rows, d), lambda l, j: (0, 0)),
            pl.BlockSpec((1, d, tn), lambda l, j: (l, 0, j)),
            pl.BlockSpec((1, 1, tn), lambda l, j: (l, 0, j)),
        ],
        out_specs=pl.BlockSpec((1, rows, tn), lambda l, j: (l, 0, j)),
        out_shape=jax.ShapeDtypeStruct((depth, rows, n), F32),
        compiler_params=_params("parallel", "parallel"),
        name="mod_proj",
    )(cc, mod_w, mod_b.reshape(depth, 1, n))


def _adaln_rows(x_ref, mod_ref, gain_ref, h_scr, gain_row, shift_row, scale_row):
    gain = gain_ref[gain_row:gain_row + 1, :]
    shift = mod_ref[0, shift_row:shift_row + 1, :]
    scale1 = 1.0 + mod_ref[0, scale_row:scale_row + 1, :]
    tm = x_ref.shape[0]
    rows = min(NORM_ROWS, tm)

    def body(r, carry):
        sl = pl.ds(pl.multiple_of(r * rows, rows), rows)
        x = x_ref[sl, :]
        y = x * lax.rsqrt(jnp.mean(x * x, axis=-1, keepdims=True) + EPS)
        h_scr[sl, :] = ((y * gain) * scale1 + shift).astype(BF16)
        return carry

    lax.fori_loop(0, tm // rows, body, 0)


def _up_kernel(x_ref, mod_ref, gain_ref, wg_ref, wu_ref, o_ref, h_scr, *, rows3):
    @pl.when(pl.program_id(1) == 0)
    def _():
        _adaln_rows(x_ref, mod_ref, gain_ref, h_scr, *rows3)

    h = h_scr[...]
    g = jnp.dot(h, wg_ref[...], preferred_element_type=F32)
    u = jnp.dot(h, wu_ref[...], preferred_element_type=F32)
    o_ref[...] = (jax.nn.silu(g) * u).astype(o_ref.dtype)


def _proj_kernel(x_ref, mod_ref, gain_ref, w_ref, o_ref, h_scr, *, rows3):
    @pl.when(pl.program_id(1) == 0)
    def _():
        _adaln_rows(x_ref, mod_ref, gain_ref, h_scr, *rows3)

    o_ref[...] = jnp.dot(h_scr[...], w_ref[...], preferred_element_type=F32).astype(o_ref.dtype)


def _row_tile(m, rows_per_mod):
    return _pick(rows_per_mod, (512, 256, 128, 64, 32, 16, 8))


def _ffn_up(x2d, mod, gain, w1, rows3, rows_per_mod):
    m, d = x2d.shape
    f = w1.shape[1] // 2
    tm = _row_tile(m, rows_per_mod)
    tn = _pick(f, (256, 128))
    bpm = rows_per_mod // tm
    nj = f // tn
    return pl.pallas_call(
        functools.partial(_up_kernel, rows3=rows3),
        grid=(m // tm, nj),
        in_specs=[
            pl.BlockSpec((tm, d), lambda i, j: (i, 0)),
            pl.BlockSpec((1, N_MOD, d), lambda i, j: (i // bpm, 0, 0)),
            pl.BlockSpec(gain.shape, lambda i, j: (0, 0)),
            pl.BlockSpec((d, tn), lambda i, j: (0, j)),
            pl.BlockSpec((d, tn), lambda i, j: (0, j + nj)),
        ],
        out_specs=pl.BlockSpec((tm, tn), lambda i, j: (i, j)),
        out_shape=jax.ShapeDtypeStruct((m, f), BF16),
        scratch_shapes=[pltpu.VMEM((tm, d), BF16)],
        compiler_params=_params("parallel", "arbitrary"),
        name="ffn_up",
    )(x2d, mod, gain, w1, w1)


def _proj(x2d, mod, gain, w, rows3, rows_per_mod, name):
    m, d = x2d.shape
    n = w.shape[1]
    tm = _row_tile(m, rows_per_mod)
    tn = _pick(n, (512, 256, 128))
    bpm = rows_per_mod // tm
    return pl.pallas_call(
        functools.partial(_proj_kernel, rows3=rows3),
        grid=(m // tm, n // tn),
        in_specs=[
            pl.BlockSpec((tm, d), lambda i, j: (i, 0)),
            pl.BlockSpec((1, N_MOD, d), lambda i, j: (i // bpm, 0, 0)),
            pl.BlockSpec(gain.shape, lambda i, j: (0, 0)),
            pl.BlockSpec((d, tn), lambda i, j: (0, j)),
        ],
        out_specs=pl.BlockSpec((tm, tn), lambda i, j: (i, j)),
        out_shape=jax.ShapeDtypeStruct((m, n), F32),
        scratch_shapes=[pltpu.VMEM((tm, d), BF16)],
        compiler_params=_params("parallel", "arbitrary"),
        name=name,
    )(x2d, mod, gain, w)


def _down_kernel(*refs, n_a, gate_row, coef):
    a_refs = refs[:n_a]
    w_refs = refs[n_a:2 * n_a]
    x_ref, mod_ref, o_ref = refs[2 * n_a:]
    acc = jnp.dot(a_refs[0][...], w_refs[0][...], preferred_element_type=F32)
    for a_ref, w_ref in zip(a_refs[1:], w_refs[1:]):
        acc = acc + jnp.dot(a_ref[...], w_ref[...], preferred_element_type=F32)
    gate = mod_ref[0, gate_row:gate_row + 1, :]
    if coef != 1.0:
        gate = coef * gate
    o_ref[...] = x_ref[...] + gate * acc


def _down(a_list, w, x2d, mod, gate_row, coef, rows_per_mod, name):
    m, d = x2d.shape
    n_a = len(a_list)
    kk = a_list[0].shape[1]
    tm = _row_tile(m, rows_per_mod)
    tn = _pick(d, (256, 128))
    bpm = rows_per_mod // tm
    in_specs = [pl.BlockSpec((tm, kk), lambda i, j: (i, 0)) for _ in a_list]
    in_specs += [pl.BlockSpec((kk, tn), functools.partial(lambda i, j, r: (r, j), r=r)) for r in range(n_a)]
    in_specs += [
        pl.BlockSpec((tm, tn), lambda i, j: (i, j)),
        pl.BlockSpec((1, N_MOD, tn), lambda i, j: (i // bpm, 0, j)),
    ]
    return pl.pallas_call(
        functools.partial(_down_kernel, n_a=n_a, gate_row=gate_row, coef=coef),
        grid=(m // tm, d // tn),
        in_specs=in_specs,
        out_specs=pl.BlockSpec((tm, tn), lambda i, j: (i, j)),
        out_shape=jax.ShapeDtypeStruct((m, d), F32),
        compiler_params=_params("parallel", "arbitrary"),
        name=name,
    )(*a_list, *([w] * n_a), x2d, mod)


def _head_norm(x, gain):
    return x * lax.rsqrt(jnp.mean(x * x, axis=-1, keepdims=True) + EPS) * gain


def _rope(x, c, s_hi, s_lo):
    return x * c + pltpu.roll(x, HEAD_DIM - ROPE_AXIS_DIM // 2, 1) * s_hi + pltpu.roll(x, ROPE_AXIS_DIM // 2, 1) * s_lo


def _attn_kernel(*refs, kv_rows, rope):
    n_kv = len(kv_rows)
    q_ref = refs[0]
    kv_refs = refs[1:1 + 2 * n_kv]
    qg_ref, kg_ref = refs[1 + 2 * n_kv:3 + 2 * n_kv]
    pos = 3 + 2 * n_kv
    if rope:
        cq_ref, hq_ref, lq_ref, ck_ref, hk_ref, lk_ref = refs[pos:pos + 6]
        pos += 6
    o_ref, k_scr, v_scr = refs[pos:pos + 3]

    @pl.when(pl.program_id(2) == 0)
    def _():
        off = 0
        for part, rows in enumerate(kv_rows):
            kn = _head_norm(kv_refs[2 * part][...], kg_ref[...])
            if rope and part == 0:
                kn = _rope(kn, ck_ref[...], hk_ref[...], lk_ref[...])
            k_scr[off:off + rows, :] = kn.astype(BF16)
            v_scr[off:off + rows, :] = kv_refs[2 * part + 1][...].astype(BF16)
            off += rows

    scale = HEAD_DIM ** -0.5
    for g in range(KV_GROUP):
        cols = slice(g * HEAD_DIM, (g + 1) * HEAD_DIM)
        q = _head_norm(q_ref[:, cols], qg_ref[...])
        if rope:
            q = _rope(q, cq_ref[...], hq_ref[...], lq_ref[...])
        q = (q * scale).astype(BF16)
        s = lax.dot_general(q, k_scr[...], (((1,), (1,)), ((), ())), preferred_element_type=F32)
        p = jnp.exp(s - jnp.max(s, axis=-1, keepdims=True))
        denom = jnp.sum(p, axis=-1, keepdims=True)
        o = jnp.dot(p.astype(BF16), v_scr[...], preferred_element_type=F32)
        o_ref[:, cols] = (o / denom).astype(o_ref.dtype)


def _attention(pq, kv_sources, q_gain, k_gain, tables, batch, q_col0, k_col0, v_col0):
    t = pq.shape[0] // batch
    n_kv_heads = (k_col0 - q_col0) // (KV_GROUP * HEAD_DIM)
    tq = _pick(t, (256, 128, 64, 32, 16, 8))
    nq = t // tq
    gw = KV_GROUP * HEAD_DIM
    kv_rows = tuple(src.shape[0] // batch for src in kv_sources)
    rope = tables is not None
    in_specs = [pl.BlockSpec((tq, gw), lambda b, h, i: (b * nq + i, q_col0 // gw + h))]
    args = [pq]
    for src, rows in zip(kv_sources, kv_rows):
        in_specs.append(pl.BlockSpec((rows, HEAD_DIM), lambda b, h, i: (b, k_col0 // HEAD_DIM + h)))
        in_specs.append(pl.BlockSpec((rows, HEAD_DIM), lambda b, h, i: (b, v_col0 // HEAD_DIM + h)))
        args += [src, src]
    in_specs += [pl.BlockSpec((1, HEAD_DIM), lambda b, h, i: (0, 0))] * 2
    args += [q_gain.reshape(1, HEAD_DIM), k_gain.reshape(1, HEAD_DIM)]
    if rope:
        in_specs += [pl.BlockSpec((tq, HEAD_DIM), lambda b, h, i: (i, 0))] * 3
        in_specs += [pl.BlockSpec((t, HEAD_DIM), lambda b, h, i: (0, 0))] * 3
        args += list(tables) * 2
    return pl.pallas_call(
        functools.partial(_attn_kernel, kv_rows=kv_rows, rope=rope),
        grid=(batch, n_kv_heads, nq),
        in_specs=in_specs,
        out_specs=pl.BlockSpec((tq, gw), lambda b, h, i: (b * nq + i, h)),
        out_shape=jax.ShapeDtypeStruct((batch * t, n_kv_heads * gw), BF16),
        scratch_shapes=[pltpu.VMEM((sum(kv_rows), HEAD_DIM), BF16), pltpu.VMEM((sum(kv_rows), HEAD_DIM), BF16)],
        compiler_params=_params("parallel", "parallel", "arbitrary"),
        name="gqa_rope" if rope else "gqa_ctx",
    )(*args)


def _rope_tables(t):
    rows = t // GRID_W
    row_ids = np.repeat(np.arange(rows, dtype=np.float32), GRID_W)
    col_ids = np.tile(np.arange(GRID_W, dtype=np.float32), rows)
    inv_freq = jnp.asarray(ROPE_THETA, F32) ** (-jnp.arange(0, ROPE_AXIS_DIM, 2, dtype=F32) / ROPE_AXIS_DIM)
    ang_r = jnp.asarray(row_ids)[:, None] * inv_freq
    ang_c = jnp.asarray(col_ids)[:, None] * inv_freq
    zero = jnp.zeros_like(ang_r)
    cos = jnp.concatenate([jnp.cos(ang_r), jnp.cos(ang_r), jnp.cos(ang_c), jnp.cos(ang_c)], axis=1)
    s_hi = jnp.concatenate([-jnp.sin(ang_r), zero, -jnp.sin(ang_c), zero], axis=1)
    s_lo = jnp.concatenate([zero, jnp.sin(ang_r), zero, jnp.sin(ang_c)], axis=1)
    return cos, s_hi, s_lo


def _chunk_tri(rev):
    idx = np.arange(HGRN_BLOCK)
    same = (idx[:, None] // HGRN_CHUNK) == (idx[None, :] // HGRN_CHUNK)
    tri = (idx[None, :] >= idx[:, None]) if rev else (idx[None, :] <= idx[:, None])
    return jnp.asarray((same & tri).astype(np.float32), BF16)


def _hgrn_kernel(*refs, rev, final):
    q_ref, z_ref, v_ref, lb_ref, tri_ref, s0_ref = refs[:6]
    pos = 6
    if final:
        oprev_ref, g_ref, gain_ref = refs[6:9]
        pos = 9
    o_ref, sfin_ref, st_scr = refs[pos:pos + 3]
    n_chunks = HGRN_BLOCK // HGRN_CHUNK
    ref_row = HGRN_CHUNK - 1 - HGRN_CHUNK // 2 if rev else HGRN_CHUNK // 2
    end_row = 0 if rev else HGRN_CHUNK - 1

    @pl.when(pl.program_id(2) == 0)
    def _():
        st_scr[...] = s0_ref[0, 0]

    z = z_ref[...]
    lb = lb_ref[...]
    log_f = jnp.log(lb + (1.0 - lb) * jax.nn.sigmoid(z))
    k = (1.0 - lb) * jax.nn.sigmoid(-z)
    q = jax.nn.silu(q_ref[...])
    v = v_ref[...]
    tri = tri_ref[...]

    g1 = log_f.astype(BF16)
    r1 = log_f - g1.astype(F32)
    g2 = r1.astype(BF16)
    g3 = (r1 - g2.astype(F32)).astype(BF16)
    cum = (jnp.dot(tri, g1, preferred_element_type=F32) + jnp.dot(tri, g2, preferred_element_type=F32)
           + jnp.dot(tri, g3, preferred_element_type=F32))
    cum3 = cum.reshape(n_chunks, HGRN_CHUNK, HEAD_DIM)
    ref3 = cum3[:, ref_row:ref_row + 1, :]
    end3 = cum3[:, end_row:end_row + 1, :]
    refb = jnp.broadcast_to(ref3, cum3.shape).reshape(cum.shape)
    endb = jnp.broadcast_to(end3, cum3.shape).reshape(cum.shape)

    qa = (q * jnp.exp(cum - refb)).astype(BF16)
    ka = (k * jnp.exp(refb - cum)).astype(BF16)
    attn = lax.dot_general(qa, ka, (((1,), (1,)), ((), ())), preferred_element_type=F32)
    attn = jnp.where(tri > 0, attn, 0.0)
    o_intra = jnp.dot(attn.astype(BF16), v.astype(BF16), preferred_element_type=F32)
    qe = (q * jnp.exp(cum)).astype(BF16)
    kd = (k * jnp.exp(endb - cum)).astype(BF16)
    v_t = v.T.astype(BF16)
    lane_chunk = lax.broadcasted_iota(jnp.int32, v_t.shape, 1) // HGRN_CHUNK
    dec = jnp.exp(end3)

    state = st_scr[...]
    order = range(n_chunks - 1, -1, -1) if rev else range(n_chunks)
    outs = [None] * n_chunks
    for c in order:
        rows = slice(c * HGRN_CHUNK, (c + 1) * HGRN_CHUNK)
        upd = jnp.dot(jnp.where(lane_chunk == c, v_t, jnp.zeros_like(v_t)), kd, preferred_element_type=F32)
        inter = lax.dot_general(qe[rows], state.astype(BF16), (((1,), (1,)), ((), ())), preferred_element_type=F32)
        outs[c] = o_intra[rows] + inter
        state = state * dec[c] + upd
    st_scr[...] = state
    sfin_ref[0, 0] = state
    o = jnp.concatenate(outs, axis=0)
    if final:
        o = o + oprev_ref[...]
        y = o * lax.rsqrt(jnp.mean(o * o, axis=-1, keepdims=True) + EPS) * gain_ref[...]
        o = y * jax.nn.silu(g_ref[...])
    o_ref[...] = o.astype(o_ref.dtype)


def _hgrn_scan(p, lb_row, s0, batch, cols, rev, o_prev=None, out_gain=None):
    t = p.shape[0] // batch
    nsb = t // HGRN_BLOCK
    n_heads = s0.shape[1]
    final = o_prev is not None
    q0, z0, v0, g0 = (c // HEAD_DIM for c in cols)

    def row_blk(b, s):
        return b * nsb + (nsb - 1 - s if rev else s)

    blk = (HGRN_BLOCK, HEAD_DIM)
    in_specs = [
        pl.BlockSpec(blk, lambda b, h, s: (row_blk(b, s), q0 + h)),
        pl.BlockSpec(blk, lambda b, h, s: (row_blk(b, s), z0 + h)),
        pl.BlockSpec(blk, lambda b, h, s: (row_blk(b, s), v0 + h)),
        pl.BlockSpec((1, HEAD_DIM), lambda b, h, s: (0, h)),
        pl.BlockSpec((HGRN_BLOCK, HGRN_BLOCK), lambda b, h, s: (0, 0)),
        pl.BlockSpec((1, 1, HEAD_DIM, HEAD_DIM), lambda b, h, s: (b, h, 0, 0)),
    ]
    args = [p, p, p, lb_row, _chunk_tri(rev), s0]
    if final:
        in_specs += [
            pl.BlockSpec(blk, lambda b, h, s: (row_blk(b, s), h)),
            pl.BlockSpec(blk, lambda b, h, s: (row_blk(b, s), g0 + h)),
            pl.BlockSpec((1, HEAD_DIM), lambda b, h, s: (0, 0)),
        ]
        args += [o_prev, p, out_gain.reshape(1, HEAD_DIM)]
    return pl.pallas_call(
        functools.partial(_hgrn_kernel, rev=rev, final=final),
        grid=(batch, n_heads, nsb),
        in_specs=in_specs,
        out_specs=[
            pl.BlockSpec(blk, lambda b, h, s: (row_blk(b, s), h)),
            pl.BlockSpec((1, 1, HEAD_DIM, HEAD_DIM), lambda b, h, s: (b, h, 0, 0)),
        ],
        out_shape=[
            jax.ShapeDtypeStruct((batch * t, n_heads * HEAD_DIM), BF16 if final else F32),
            jax.ShapeDtypeStruct(s0.shape, F32),
        ],
        scratch_shapes=[pltpu.VMEM((HEAD_DIM, HEAD_DIM), F32)],
        compiler_params=_params("parallel", "parallel", "arbitrary"),
        name="hgrn_bwd" if rev else "hgrn_fwd",
    )(*args)


def _conv_kernel(b_ref, c_ref, u_ref, w_ref, o_ref):
    u = c_ref[...] * u_ref[...]
    t = u.shape[0]
    row = lax.broadcasted_iota(jnp.int32, u.shape, 0)
    prev = jnp.where(row == 0, 0.0, pltpu.roll(u, 1, 0))
    nxt = jnp.where(row == t - 1, 0.0, pltpu.roll(u, t - 1, 0))
    y = prev * w_ref[0:1, :] + u * w_ref[1:2, :] + nxt * w_ref[2:3, :]
    o_ref[...] = (b_ref[...] * y).astype(o_ref.dtype)


def _gated_conv(p, conv_w, batch):
    d = conv_w.shape[1]
    t = p.shape[0] // batch
    td = _pick(d, (256, 128))
    nd = d // td
    return pl.pallas_call(
        _conv_kernel,
        grid=(batch, nd),
        in_specs=[
            pl.BlockSpec((t, td), lambda b, j: (b, j)),
            pl.BlockSpec((t, td), lambda b, j: (b, nd + j)),
            pl.BlockSpec((t, td), lambda b, j: (b, 2 * nd + j)),
            pl.BlockSpec((conv_w.shape[0], td), lambda b, j: (0, j)),
        ],
        out_specs=pl.BlockSpec((t, td), lambda b, j: (b, j)),
        out_shape=jax.ShapeDtypeStruct((batch * t, d), BF16),
        compiler_params=_params("parallel", "parallel"),
        name="gated_conv3",
    )(p, p, p, conv_w)


def kernel(x, c, ctx, c_ctx, mod_w, mod_b, norm_g, ffn_w1, ffn_w2, ab_w_in, ab_w_out, attn_q_gain, attn_k_gain, hgrn_lb_logits, hgrn_out_gain, conv_w_in, conv_w, conv_w_out):
    batch, seq, d = x.shape
    ctx_len = ctx.shape[1]
    depth = mod_w.shape[0]
    attn_w = d // 2
    kv_w = attn_w // KV_GROUP
    hg_w = d - attn_w
    n_hg_heads = hg_w // HEAD_DIM
    col_q, col_k, col_v = 0, attn_w, attn_w + kv_w
    col_qb = attn_w + 2 * kv_w
    col_zf, col_zb, col_ib, col_gb = (col_qb + hg_w * i for i in range(1, 5))

    lb_table = jnp.cumsum(jax.nn.softmax(hgrn_lb_logits.astype(F32), axis=1), axis=1)
    tables = _rope_tables(seq)

    pad = (-(batch + 1)) % 8
    cc = jnp.concatenate([c, c_ctx[None], jnp.zeros((pad, d), F32)], axis=0)
    mod_all = _mod_all(cc, mod_w, mod_b).reshape(depth, batch + 1 + pad, N_MOD, d)

    xl = x.reshape(batch * seq, d)
    xc = ctx.reshape(batch * ctx_len, d)
    n_ctx = batch * ctx_len

    def ffn(xs, mod, gain, w1, w2, rows3, gate_row, rpm):
        a = _ffn_up(xs, mod, gain, w1, rows3, rpm)
        return _down([a], w2, xs, mod, gate_row, 0.5, rpm, "ffn_down")

    for layer in range(depth):
        last = layer == depth - 1
        even = layer % 2 == 0
        ctx_needed = even or not last
        gain = norm_g[layer]
        mod_l = mod_all[layer, :batch]
        mod_c = mod_all[layer, batch:batch + 1]
        w1a, w1b = ffn_w1[layer, 0].astype(BF16), ffn_w1[layer, 1].astype(BF16)
        w2a, w2b = ffn_w2[layer, 0].astype(BF16), ffn_w2[layer, 1].astype(BF16)

        xl = ffn(xl, mod_l, gain, w1a, w2a, (0, 0, 1), 2, seq)
        if ctx_needed:
            xc = ffn(xc, mod_c, gain, w1a, w2a, (0, 0, 1), 2, n_ctx)

        if even:
            e = layer // 2
            w_in = ab_w_in[e].astype(BF16)
            w_out = ab_w_out[e].astype(BF16)
            pl_ = _proj(xl, mod_l, gain, w_in, (1, 3, 4), seq, "ab_proj")
            pc_ = _proj(xc, mod_c, gain, w_in, (1, 3, 4), n_ctx, "ab_proj")
            o_attn = _attention(pl_, [pl_, pc_], attn_q_gain[e], attn_k_gain[e], tables, batch, col_q, col_k, col_v)
            s_zero = jnp.zeros((batch, n_hg_heads, HEAD_DIM, HEAD_DIM), F32)
            lbs = [lb_table[dd, layer].reshape(1, hg_w) for dd in range(2)]
            zcols = (col_zf, col_zb)
            oc_dir, sc_dir = [], []
            o_prev = None
            for dd in range(2):
                o_prev, s_c = _hgrn_scan(pc_, lbs[dd], s_zero, batch, (col_qb, zcols[dd], col_ib, col_gb), dd == 1,
                                         o_prev, hgrn_out_gain[e] if dd == 1 else None)
                sc_dir.append(s_c)
            o_rec_c = o_prev
            o_prev = None
            for dd in range(2):
                o_prev, _ = _hgrn_scan(pl_, lbs[dd], sc_dir[dd], batch, (col_qb, zcols[dd], col_ib, col_gb), dd == 1,
                                       o_prev, hgrn_out_gain[e] if dd == 1 else None)
            o_rec = o_prev
            xl = _down([o_attn, o_rec], w_out, xl, mod_l, 5, 1.0, seq, "mixer_out")
            if not last:
                o_attn_c = _attention(pc_, [pc_], attn_q_gain[e], attn_k_gain[e], None, batch, col_q, col_k, col_v)
                xc = _down([o_attn_c, o_rec_c], w_out, xc, mod_c, 5, 1.0, n_ctx, "mixer_out")
        else:
            o = layer // 2
            w_in = conv_w_in[o].astype(BF16)
            w_out = conv_w_out[o].astype(BF16)
            yl = _gated_conv(_proj(xl, mod_l, gain, w_in, (1, 3, 4), seq, "conv_proj"), conv_w[o], batch)
            xl = _down([yl], w_out, xl, mod_l, 5, 1.0, seq, "mixer_out")
            if not last:
                yc = _gated_conv(_proj(xc, mod_c, gain, w_in, (1, 3, 4), n_ctx, "conv_proj"), conv_w[o], batch)
                xc = _down([yc], w_out, xc, mod_c, 5, 1.0, n_ctx, "mixer_out")

        xl = ffn(xl, mod_l, gain, w1b, w2b, (2, 6, 7), 8, seq)
        if not last:
            xc = ffn(xc, mod_c, gain, w1b, w2b, (2, 6, 7), 8, n_ctx)

    return xl.reshape(batch, seq, d)
```

```python
import functools

import numpy as np
import jax
import jax.numpy as jnp
from jax import lax
from jax.experimental import pallas as pl
from jax.experimental.pallas import tpu as pltpu

GRID_W = 64
HEAD_DIM = 128
KV_GROUP = 4
ROPE_THETA = 10000.0
ROPE_AXIS_DIM = HEAD_DIM // 2
HGRN_CHUNK = 32
N_MOD = 9
EPS = 1e-6

VMEM_LIMIT_BYTES = 56 * 1024 * 1024
MATMUL_VMEM_BUDGET = 46 * 1024 * 1024
MATMUL_ROWS = 1024
HGRN_BLOCK = 256
HGRN_HEADS_PER_STEP = 4

BF16 = jnp.bfloat16
F32 = jnp.float32


def _params(*sem):
    return pltpu.CompilerParams(dimension_semantics=sem, vmem_limit_bytes=VMEM_LIMIT_BYTES)


def _pick(n, candidates):
    for c in candidates:
        if n % c == 0:
            return c
    return n


def _mod_kernel(c_ref, w_ref, b_ref, o_ref):
    a = jax.nn.silu(c_ref[...]).astype(BF16)
    o_ref[0] = jnp.dot(a, w_ref[0].astype(BF16), preferred_element_type=F32) + b_ref[0]


def _mod_all(cc, mod_w, mod_b):
    depth, d, n = mod_w.shape
    rows = cc.shape[0]
    tn = _pick(n, (512, 256, 128))
    return pl.pallas_call(
        _mod_kernel,
        grid=(depth, n // tn),
        in_specs=[
            pl.BlockSpec((rows, d), lambda l, j: (0, 0)),
            pl.BlockSpec((1, d, tn), lambda l, j: (l, 0, j)),
            pl.BlockSpec((1, 1, tn), lambda l, j: (l, 0, j)),
        ],
        out_specs=pl.BlockSpec((1, rows, tn), lambda l, j: (l, 0, j)),
        out_shape=jax.ShapeDtypeStruct((depth, rows, n), F32),
        compiler_params=_params("parallel", "parallel"),
        name="mod_proj",
    )(cc, mod_w, mod_b.reshape(depth, 1, n))


def _adaln_kernel(x_ref, mod_ref, gain_ref, o_ref, *, rows3):
    gain_row, shift_row, scale_row = rows3
    x = x_ref[...]
    y = x * lax.rsqrt(jnp.mean(x * x, axis=-1, keepdims=True) + EPS)
    y = y * gain_ref[gain_row:gain_row + 1, :]
    h = y * (1.0 + mod_ref[0, scale_row:scale_row + 1, :]) + mod_ref[0, shift_row:shift_row + 1, :]
    o_ref[...] = h.astype(o_ref.dtype)


def _adaln(x2d, mod, gain, rows3, rows_per_mod):
    m, d = x2d.shape
    tr = _pick(rows_per_mod, (256, 128, 64, 32, 16, 8))
    bpm = rows_per_mod // tr
    return pl.pallas_call(
        functools.partial(_adaln_kernel, rows3=rows3),
        grid=(m // tr,),
        in_specs=[
            pl.BlockSpec((tr, d), lambda i: (i, 0)),
            pl.BlockSpec((1, N_MOD, d), lambda i: (i // bpm, 0, 0)),
            pl.BlockSpec(gain.shape, lambda i: (0, 0)),
        ],
        out_specs=pl.BlockSpec((tr, d), lambda i: (i, 0)),
        out_shape=jax.ShapeDtypeStruct((m, d), BF16),
        compiler_params=_params("parallel"),
        name="adaln",
    )(x2d, mod, gain)


def _tiles(rows_per_mod, n, a_bytes_per_row, w_bytes_per_col, io_bytes_per_elem):
    tm = _pick(rows_per_mod, (MATMUL_ROWS, 512, 256, 128, 64, 32, 16, 8))
    for tn in (512, 256, 128):
        vmem = 2 * (tm * a_bytes_per_row + tn * w_bytes_per_col + tm * tn * io_bytes_per_elem)
        if n % tn == 0 and vmem <= MATMUL_VMEM_BUDGET:
            return tm, tn
    return tm, 128


def _up_kernel(h_ref, wg_ref, wu_ref, o_ref):
    h = h_ref[...]
    g = jnp.dot(h, wg_ref[...], preferred_element_type=F32)
    u = jnp.dot(h, wu_ref[...], preferred_element_type=F32)
    o_ref[...] = (jax.nn.silu(g) * u).astype(o_ref.dtype)


def _proj_kernel(h_ref, w_ref, o_ref):
    o_ref[...] = jnp.dot(h_ref[...], w_ref[...], preferred_element_type=F32).astype(o_ref.dtype)


def _ffn_up(h, w1):
    m, d = h.shape
    f = w1.shape[1] // 2
    tm, tn = _tiles(m, f, 2 * d, 2 * 2 * d, 2)
    nj = f // tn
    return pl.pallas_call(
        _up_kernel,
        grid=(m // tm, nj),
        in_specs=[
            pl.BlockSpec((tm, d), lambda i, j: (i, 0)),
            pl.BlockSpec((d, tn), lambda i, j: (0, j)),
            pl.BlockSpec((d, tn), lambda i, j: (0, j + nj)),
        ],
        out_specs=pl.BlockSpec((tm, tn), lambda i, j: (i, j)),
        out_shape=jax.ShapeDtypeStruct((m, f), BF16),
        compiler_params=_params("parallel", "arbitrary"),
        name="ffn_up",
    )(h, w1, w1)


def _proj(h, w, name):
    m, d = h.shape
    n = w.shape[1]
    tm, tn = _tiles(m, n, 2 * d, 2 * d, 4)
    return pl.pallas_call(
        _proj_kernel,
        grid=(m // tm, n // tn),
        in_specs=[
            pl.BlockSpec((tm, d), lambda i, j: (i, 0)),
            pl.BlockSpec((d, tn), lambda i, j: (0, j)),
        ],
        out_specs=pl.BlockSpec((tm, tn), lambda i, j: (i, j)),
        out_shape=jax.ShapeDtypeStruct((m, n), F32),
        compiler_params=_params("parallel", "arbitrary"),
        name=name,
    )(h, w)


def _down_kernel(*refs, n_a, gate_row, coef):
    a_refs = refs[:n_a]
    w_refs = refs[n_a:2 * n_a]
    x_ref, mod_ref, o_ref = refs[2 * n_a:]
    acc = jnp.dot(a_refs[0][...], w_refs[0][...], preferred_element_type=F32)
    for a_ref, w_ref in zip(a_refs[1:], w_refs[1:]):
        acc = acc + jnp.dot(a_ref[...], w_ref[...], preferred_element_type=F32)
    gate = mod_ref[0, gate_row:gate_row + 1, :]
    if coef != 1.0:
        gate = coef * gate
    o_ref[...] = x_ref[...] + gate * acc


def _down(a_list, w, x2d, mod, gate_row, coef, rows_per_mod, name):
    m, d = x2d.shape
    n_a = len(a_list)
    kk = a_list[0].shape[1]
    tm, tn = _tiles(rows_per_mod, d, 2 * kk * n_a, 2 * kk * n_a, 2 * 4)
    bpm = rows_per_mod // tm
    in_specs = [pl.BlockSpec((tm, kk), lambda i, j: (i, 0)) for _ in a_list]
    in_specs += [pl.BlockSpec((kk, tn), functools.partial(lambda i, j, r: (r, j), r=r)) for r in range(n_a)]
    in_specs += [
        pl.BlockSpec((tm, tn), lambda i, j: (i, j)),
        pl.BlockSpec((1, N_MOD, tn), lambda i, j: (i // bpm, 0, j)),
    ]
    return pl.pallas_call(
        functools.partial(_down_kernel, n_a=n_a, gate_row=gate_row, coef=coef),
        grid=(m // tm, d // tn),
        in_specs=in_specs,
        out_specs=pl.BlockSpec((tm, tn), lambda i, j: (i, j)),
        out_shape=jax.ShapeDtypeStruct((m, d), F32),
        compiler_params=_params("parallel", "arbitrary"),
        name=name,
    )(*a_list, *([w] * n_a), x2d, mod)


def _head_norm(x, gain):
    return x * lax.rsqrt(jnp.mean(x * x, axis=-1, keepdims=True) + EPS) * gain


def _rope(x, c, s_hi, s_lo):
    return x * c + pltpu.roll(x, HEAD_DIM - ROPE_AXIS_DIM // 2, 1) * s_hi + pltpu.roll(x, ROPE_AXIS_DIM // 2, 1) * s_lo


def _attn_kernel(*refs, kv_rows, rope):
    n_kv = len(kv_rows)
    q_ref = refs[0]
    kv_refs = refs[1:1 + 2 * n_kv]
    qg_ref, kg_ref = refs[1 + 2 * n_kv:3 + 2 * n_kv]
    pos = 3 + 2 * n_kv
    if rope:
        cq_ref, hq_ref, lq_ref, ck_ref, hk_ref, lk_ref = refs[pos:pos + 6]
        pos += 6
    o_ref, k_scr, v_scr = refs[pos:pos + 3]

    @pl.when(pl.program_id(2) == 0)
    def _():
        off = 0
        for part, rows in enumerate(kv_rows):
            kn = _head_norm(kv_refs[2 * part][...], kg_ref[...])
            if rope and part == 0:
                kn = _rope(kn, ck_ref[...], hk_ref[...], lk_ref[...])
            k_scr[off:off + rows, :] = kn.astype(BF16)
            v_scr[off:off + rows, :] = kv_refs[2 * part + 1][...].astype(BF16)
            off += rows

    scale = HEAD_DIM ** -0.5
    for g in range(KV_GROUP):
        cols = slice(g * HEAD_DIM, (g + 1) * HEAD_DIM)
        q = _head_norm(q_ref[:, cols], qg_ref[...])
        if rope:
            q = _rope(q, cq_ref[...], hq_ref[...], lq_ref[...])
        q = (q * scale).astype(BF16)
        s = lax.dot_general(q, k_scr[...], (((1,), (1,)), ((), ())), preferred_element_type=F32)
        p = jnp.exp(s - jnp.max(s, axis=-1, keepdims=True))
        denom = jnp.sum(p, axis=-1, keepdims=True)
        o = jnp.dot(p.astype(BF16), v_scr[...], preferred_element_type=F32)
        o_ref[:, cols] = (o / denom).astype(o_ref.dtype)


def _attention(pq, kv_sources, q_gain, k_gain, tables, batch, q_col0, k_col0, v_col0):
    t = pq.shape[0] // batch
    n_kv_heads = (k_col0 - q_col0) // (KV_GROUP * HEAD_DIM)
    tq = _pick(t, (256, 128, 64, 32, 16, 8))
    nq = t // tq
    gw = KV_GROUP * HEAD_DIM
    kv_rows = tuple(src.shape[0] // batch for src in kv_sources)
    rope = tables is not None
    in_specs = [pl.BlockSpec((tq, gw), lambda b, h, i: (b * nq + i, q_col0 // gw + h))]
    args = [pq]
    for src, rows in zip(kv_sources, kv_rows):
        in_specs.append(pl.BlockSpec((rows, HEAD_DIM), lambda b, h, i: (b, k_col0 // HEAD_DIM + h)))
        in_specs.append(pl.BlockSpec((rows, HEAD_DIM), lambda b, h, i: (b, v_col0 // HEAD_DIM + h)))
        args += [src, src]
    in_specs += [pl.BlockSpec((1, HEAD_DIM), lambda b, h, i: (0, 0))] * 2
    args += [q_gain.reshape(1, HEAD_DIM), k_gain.reshape(1, HEAD_DIM)]
    if rope:
        in_specs += [pl.BlockSpec((tq, HEAD_DIM), lambda b, h, i: (i, 0))] * 3
        in_specs += [pl.BlockSpec((t, HEAD_DIM), lambda b, h, i: (0, 0))] * 3
        args += list(tables) * 2
    return pl.pallas_call(
        functools.partial(_attn_kernel, kv_rows=kv_rows, rope=rope),
        grid=(batch, n_kv_heads, nq),
        in_specs=in_specs,
        out_specs=pl.BlockSpec((tq, gw), lambda b, h, i: (b * nq + i, h)),
        out_shape=jax.ShapeDtypeStruct((batch * t, n_kv_heads * gw), BF16),
        scratch_shapes=[pltpu.VMEM((sum(kv_rows), HEAD_DIM), BF16), pltpu.VMEM((sum(kv_rows), HEAD_DIM), BF16)],
        compiler_params=_params("parallel", "parallel", "arbitrary"),
        name="gqa_rope" if rope else "gqa_ctx",
    )(*args)


def _rope_tables(t):
    rows = t // GRID_W
    row_ids = np.repeat(np.arange(rows, dtype=np.float32), GRID_W)
    col_ids = np.tile(np.arange(GRID_W, dtype=np.float32), rows)
    inv_freq = jnp.asarray(ROPE_THETA, F32) ** (-jnp.arange(0, ROPE_AXIS_DIM, 2, dtype=F32) / ROPE_AXIS_DIM)
    ang_r = jnp.asarray(row_ids)[:, None] * inv_freq
    ang_c = jnp.asarray(col_ids)[:, None] * inv_freq
    zero = jnp.zeros_like(ang_r)
    cos = jnp.concatenate([jnp.cos(ang_r), jnp.cos(ang_r), jnp.cos(ang_c), jnp.cos(ang_c)], axis=1)
    s_hi = jnp.concatenate([-jnp.sin(ang_r), zero, -jnp.sin(ang_c), zero], axis=1)
    s_lo = jnp.concatenate([zero, jnp.sin(ang_r), zero, jnp.sin(ang_c)], axis=1)
    return cos, s_hi, s_lo


def _chunk_tri(rev):
    idx = np.arange(HGRN_BLOCK)
    same = (idx[:, None] // HGRN_CHUNK) == (idx[None, :] // HGRN_CHUNK)
    tri = (idx[None, :] >= idx[:, None]) if rev else (idx[None, :] <= idx[:, None])
    return jnp.asarray((same & tri).astype(np.float32), BF16)


def _hgrn_kernel(*refs, rev, final, n_heads):
    q_ref, z_ref, v_ref, lb_ref, tri_ref, s0_ref = refs[:6]
    pos = 6
    if final:
        oprev_ref, g_ref, gain_ref = refs[6:9]
        pos = 9
    o_ref, sfin_ref, st_scr = refs[pos:pos + 3]
    n_chunks = HGRN_BLOCK // HGRN_CHUNK
    ref_row = HGRN_CHUNK - 1 - HGRN_CHUNK // 2 if rev else HGRN_CHUNK // 2
    end_row = 0 if rev else HGRN_CHUNK - 1

    @pl.when(pl.program_id(2) == 0)
    def _():
        st_scr[...] = s0_ref[0]

    tri = tri_ref[...]
    in_chunk = tri > 0
    heads = []
    for h in range(n_heads):
        cols = slice(h * HEAD_DIM, (h + 1) * HEAD_DIM)
        z = z_ref[:, cols]
        lb = lb_ref[:, cols]
        log_f = jnp.log(lb + (1.0 - lb) * jax.nn.sigmoid(z))
        k = (1.0 - lb) * jax.nn.sigmoid(-z)
        q = jax.nn.silu(q_ref[:, cols])
        v = v_ref[:, cols]
        g1 = log_f.astype(BF16)
        r1 = log_f - g1.astype(F32)
        g2 = r1.astype(BF16)
        g3 = (r1 - g2.astype(F32)).astype(BF16)
        cum = (jnp.dot(tri, g1, preferred_element_type=F32) + jnp.dot(tri, g2, preferred_element_type=F32)
               + jnp.dot(tri, g3, preferred_element_type=F32))
        cum3 = cum.reshape(n_chunks, HGRN_CHUNK, HEAD_DIM)
        ref3 = cum3[:, ref_row:ref_row + 1, :]
        end3 = cum3[:, end_row:end_row + 1, :]
        refb = jnp.broadcast_to(ref3, cum3.shape).reshape(cum.shape)
        endb = jnp.broadcast_to(end3, cum3.shape).reshape(cum.shape)
        qa = (q * jnp.exp(cum - refb)).astype(BF16)
        ka = (k * jnp.exp(refb - cum)).astype(BF16)
        attn = lax.dot_general(qa, ka, (((1,), (1,)), ((), ())), preferred_element_type=F32)
        attn = jnp.where(in_chunk, attn, 0.0)
        heads.append(dict(
            o_intra=jnp.dot(attn.astype(BF16), v.astype(BF16), preferred_element_type=F32),
            qe=(q * jnp.exp(cum)).astype(BF16),
            kd=(k * jnp.exp(endb - cum)).astype(BF16),
            v_t=v.T,
            dec=jnp.exp(end3),
            state=st_scr[h],
        ))

    lane_chunk = lax.broadcasted_iota(jnp.int32, (HEAD_DIM, HGRN_BLOCK), 1) // HGRN_CHUNK
    order = range(n_chunks - 1, -1, -1) if rev else range(n_chunks)
    for c in order:
        rows = slice(c * HGRN_CHUNK, (c + 1) * HGRN_CHUNK)
        sel = lane_chunk == c
        for h, hd in enumerate(heads):
            cols = slice(h * HEAD_DIM, (h + 1) * HEAD_DIM)
            v_c = jnp.where(sel, hd["v_t"], 0.0).astype(BF16)
            upd = jnp.dot(v_c, hd["kd"], preferred_element_type=F32)
            inter = lax.dot_general(hd["qe"][rows], hd["state"].astype(BF16), (((1,), (1,)), ((), ())),
                                    preferred_element_type=F32)
            o = hd["o_intra"][rows] + inter
            hd["state"] = hd["state"] * hd["dec"][c] + upd
            if final:
                o = o + oprev_ref[rows, cols]
                y = o * lax.rsqrt(jnp.mean(o * o, axis=-1, keepdims=True) + EPS) * gain_ref[...]
                o = y * jax.nn.silu(g_ref[rows, cols])
            o_ref[rows, cols] = o.astype(o_ref.dtype)
    for h, hd in enumerate(heads):
        st_scr[h] = hd["state"]
        sfin_ref[0, h] = hd["state"]


def _hgrn_scan(p, lb_row, s0, batch, cols, rev, o_prev=None, out_gain=None):
    t = p.shape[0] // batch
    nsb = t // HGRN_BLOCK
    n_heads = s0.shape[1]
    hps = next(g for g in (HGRN_HEADS_PER_STEP, 2, 1)
               if n_heads % g == 0 and all(c % (g * HEAD_DIM) == 0 for c in cols))
    gw = hps * HEAD_DIM
    final = o_prev is not None
    q0, z0, v0, g0 = (c // gw for c in cols)

    def row_blk(b, s):
        return b * nsb + (nsb - 1 - s if rev else s)

    blk = (HGRN_BLOCK, gw)
    st_blk = (1, hps, HEAD_DIM, HEAD_DIM)
    in_specs = [
        pl.BlockSpec(blk, lambda b, h, s: (row_blk(b, s), q0 + h)),
        pl.BlockSpec(blk, lambda b, h, s: (row_blk(b, s), z0 + h)),
        pl.BlockSpec(blk, lambda b, h, s: (row_blk(b, s), v0 + h)),
        pl.BlockSpec((1, gw), lambda b, h, s: (0, h)),
        pl.BlockSpec((HGRN_BLOCK, HGRN_BLOCK), lambda b, h, s: (0, 0)),
        pl.BlockSpec(st_blk, lambda b, h, s: (b, h, 0, 0)),
    ]
    args = [p, p, p, lb_row, _chunk_tri(rev), s0]
    if final:
        in_specs += [
            pl.BlockSpec(blk, lambda b, h, s: (row_blk(b, s), h)),
            pl.BlockSpec(blk, lambda b, h, s: (row_blk(b, s), g0 + h)),
            pl.BlockSpec((1, HEAD_DIM), lambda b, h, s: (0, 0)),
        ]
        args += [o_prev, p, out_gain.reshape(1, HEAD_DIM)]
    return pl.pallas_call(
        functools.partial(_hgrn_kernel, rev=rev, final=final, n_heads=hps),
        grid=(batch, n_heads // hps, nsb),
        in_specs=in_specs,
        out_specs=[
            pl.BlockSpec(blk, lambda b, h, s: (row_blk(b, s), h)),
            pl.BlockSpec(st_blk, lambda b, h, s: (b, h, 0, 0)),
        ],
        out_shape=[
            jax.ShapeDtypeStruct((batch * t, n_heads * HEAD_DIM), BF16 if final else F32),
            jax.ShapeDtypeStruct(s0.shape, F32),
        ],
        scratch_shapes=[pltpu.VMEM((hps, HEAD_DIM, HEAD_DIM), F32)],
        compiler_params=_params("parallel", "parallel", "arbitrary"),
        name="hgrn_bwd" if rev else "hgrn_fwd",
    )(*args)


def _conv_kernel(b_ref, c_ref, u_ref, w_ref, o_ref):
    u = c_ref[...] * u_ref[...]
    t = u.shape[0]
    row = lax.broadcasted_iota(jnp.int32, u.shape, 0)
    prev = jnp.where(row == 0, 0.0, pltpu.roll(u, 1, 0))
    nxt = jnp.where(row == t - 1, 0.0, pltpu.roll(u, t - 1, 0))
    y = prev * w_ref[0:1, :] + u * w_ref[1:2, :] + nxt * w_ref[2:3, :]
    o_ref[...] = (b_ref[...] * y).astype(o_ref.dtype)


def _gated_conv(p, conv_w, batch):
    d = conv_w.shape[1]
    t = p.shape[0] // batch
    td = _pick(d, (256, 128))
    nd = d // td
    return pl.pallas_call(
        _conv_kernel,
        grid=(batch, nd),
        in_specs=[
            pl.BlockSpec((t, td), lambda b, j: (b, j)),
            pl.BlockSpec((t, td), lambda b, j: (b, nd + j)),
            pl.BlockSpec((t, td), lambda b, j: (b, 2 * nd + j)),
            pl.BlockSpec((conv_w.shape[0], td), lambda b, j: (0, j)),
        ],
        out_specs=pl.BlockSpec((t, td), lambda b, j: (b, j)),
        out_shape=jax.ShapeDtypeStruct((batch * t, d), BF16),
        compiler_params=_params("parallel", "parallel"),
        name="gated_conv3",
    )(p, p, p, conv_w)


def kernel(x, c, ctx, c_ctx, mod_w, mod_b, norm_g, ffn_w1, ffn_w2, ab_w_in, ab_w_out, attn_q_gain, attn_k_gain, hgrn_lb_logits, hgrn_out_gain, conv_w_in, conv_w, conv_w_out):
    batch, seq, d = x.shape
    ctx_len = ctx.shape[1]
    depth = mod_w.shape[0]
    attn_w = d // 2
    kv_w = attn_w // KV_GROUP
    hg_w = d - attn_w
    n_hg_heads = hg_w // HEAD_DIM
    col_q, col_k, col_v = 0, attn_w, attn_w + kv_w
    col_qb = attn_w + 2 * kv_w
    col_zf, col_zb, col_ib, col_gb = (col_qb + hg_w * i for i in range(1, 5))

    lb_table = jnp.cumsum(jax.nn.softmax(hgrn_lb_logits.astype(F32), axis=1), axis=1)
    tables = _rope_tables(seq)

    pad = (-(batch + 1)) % 8
    cc = jnp.concatenate([c, c_ctx[None], jnp.zeros((pad, d), F32)], axis=0)
    mod_all = _mod_all(cc, mod_w, mod_b).reshape(depth, batch + 1 + pad, N_MOD, d)

    xl = x.reshape(batch * seq, d)
    xc = ctx.reshape(batch * ctx_len, d)
    n_ctx = batch * ctx_len

    def ffn(xs, mod, gain, w1, w2, rows3, gate_row, rpm):
        a = _ffn_up(_adaln(xs, mod, gain, rows3, rpm), w1)
        return _down([a], w2, xs, mod, gate_row, 0.5, rpm, "ffn_down")

    for layer in range(depth):
        last = layer == depth - 1
        even = layer % 2 == 0
        ctx_needed = even or not last
        gain = norm_g[layer]
        mod_l = mod_all[layer, :batch]
        mod_c = mod_all[layer, batch:batch + 1]
        w1a, w1b = ffn_w1[layer, 0].astype(BF16), ffn_w1[layer, 1].astype(BF16)
        w2a, w2b = ffn_w2[layer, 0].astype(BF16), ffn_w2[layer, 1].astype(BF16)

        xl = ffn(xl, mod_l, gain, w1a, w2a, (0, 0, 1), 2, seq)
        if ctx_needed:
            xc = ffn(xc, mod_c, gain, w1a, w2a, (0, 0, 1), 2, n_ctx)

        if even:
            e = layer // 2
            w_in = ab_w_in[e].astype(BF16)
            w_out = ab_w_out[e].astype(BF16)
            pl_ = _proj(_adaln(xl, mod_l, gain, (1, 3, 4), seq), w_in, "ab_proj")
            pc_ = _proj(_adaln(xc, mod_c, gain, (1, 3, 4), n_ctx), w_in, "ab_proj")
            o_attn = _attention(pl_, [pl_, pc_], attn_q_gain[e], attn_k_gain[e], tables, batch, col_q, col_k, col_v)
            s_zero = jnp.zeros((batch, n_hg_heads, HEAD_DIM, HEAD_DIM), F32)
            lbs = [lb_table[dd, layer].reshape(1, hg_w) for dd in range(2)]
            zcols = (col_zf, col_zb)
            sc_dir = []
            o_prev = None
            for dd in range(2):
                o_prev, s_c = _hgrn_scan(pc_, lbs[dd], s_zero, batch, (col_qb, zcols[dd], col_ib, col_gb), dd == 1,
                                         o_prev, hgrn_out_gain[e] if dd == 1 else None)
                sc_dir.append(s_c)
            o_rec_c = o_prev
            o_prev = None
            for dd in range(2):
                o_prev, _ = _hgrn_scan(pl_, lbs[dd], sc_dir[dd], batch, (col_qb, zcols[dd], col_ib, col_gb), dd == 1,
                                       o_prev, hgrn_out_gain[e] if dd == 1 else None)
            o_rec = o_prev
            xl = _down([o_attn, o_rec], w_out, xl, mod_l, 5, 1.0, seq, "mixer_out")
            if not last:
                o_attn_c = _attention(pc_, [pc_], attn_q_gain[e], attn_k_gain[e], None, batch, col_q, col_k, col_v)
                xc = _down([o_attn_c, o_rec_c], w_out, xc, mod_c, 5, 1.0, n_ctx, "mixer_out")
        else:
            o = layer // 2
            w_in = conv_w_in[o].astype(BF16)
            w_out = conv_w_out[o].astype(BF16)
            yl = _gated_conv(_proj(_adaln(xl, mod_l, gain, (1, 3, 4), seq), w_in, "conv_proj"), conv_w[o], batch)
            xl = _down([yl], w_out, xl, mod_l, 5, 1.0, seq, "mixer_out")
            if not last:
                yc = _gated_conv(_proj(_adaln(xc, mod_c, gain, (1, 3, 4), n_ctx), w_in, "conv_proj"), conv_w[o], batch)
                xc = _down([yc], w_out, xc, mod_c, 5, 1.0, n_ctx, "mixer_out")

        xl = ffn(xl, mod_l, gain, w1b, w2b, (2, 6, 7), 8, seq)
        if not last:
            xc = ffn(xc, mod_c, gain, w1b, w2b, (2, 6, 7), 8, n_ctx)

    return xl.reshape(batch, seq, d)
```

```python
import functools

import numpy as np
import jax
import jax.numpy as jnp
from jax import lax
from jax.experimental import pallas as pl
from jax.experimental.pallas import tpu as pltpu

GRID_W = 64
HEAD_DIM = 128
KV_GROUP = 4
ROPE_THETA = 10000.0
ROPE_AXIS_DIM = HEAD_DIM // 2
HGRN_CHUNK = 32
N_MOD = 9
EPS = 1e-6
LOG2_E = 1.4426950408889634

VMEM_LIMIT_BYTES = 56 * 1024 * 1024
MATMUL_VMEM_BUDGET = 46 * 1024 * 1024
MATMUL_ROWS = 1024
HGRN_BLOCK = 256
HGRN_HEADS_PER_STEP = 4
NORM_ROWS = 32
CAST_BLOCK_BYTES = 8 * 1024 * 1024

BF16 = jnp.bfloat16
F32 = jnp.float32


def _params(*sem):
    return pltpu.CompilerParams(dimension_semantics=sem, vmem_limit_bytes=VMEM_LIMIT_BYTES)


def _pick(n, candidates):
    for c in candidates:
        if n % c == 0:
            return c
    return n


def _mod_kernel(c_ref, w_ref, b_ref, o_ref):
    a = jax.nn.silu(c_ref[...]).astype(BF16)
    o_ref[0] = jnp.dot(a, w_ref[0].astype(BF16), preferred_element_type=F32) + b_ref[0]


def _mod_all(cc, mod_w, mod_b):
    depth, d, n = mod_w.shape
    rows = cc.shape[0]
    tn = _pick(n, (512, 256, 128))
    return pl.pallas_call(
        _mod_kernel,
        grid=(depth, n // tn),
        in_specs=[
            pl.BlockSpec((rows, d), lambda l, j: (0, 0)),
            pl.BlockSpec((1, d, tn), lambda l, j: (l, 0, j)),
            pl.BlockSpec((1, 1, tn), lambda l, j: (l, 0, j)),
        ],
        out_specs=pl.BlockSpec((1, rows, tn), lambda l, j: (l, 0, j)),
        out_shape=jax.ShapeDtypeStruct((depth, rows, n), F32),
        compiler_params=_params("parallel", "parallel"),
        name="mod_proj",
    )(cc, mod_w, mod_b.reshape(depth, 1, n))


def _cast_kernel(w_ref, o_ref):
    o_ref[...] = w_ref[...].astype(o_ref.dtype)


def _weight_bf16(w, lead):
    r, c = w.shape[-2:]
    tc = _pick(c, (4096, 2048, 1024, 512, 256, 128))
    tr = _pick(r, tuple(t for t in (2048, 1024, 512, 256, 128, 64, 32, 16, 8) if t * tc * 4 <= CAST_BLOCK_BYTES))
    return pl.pallas_call(
        _cast_kernel,
        grid=(r // tr, c // tc),
        in_specs=[pl.BlockSpec((None,) * len(lead) + (tr, tc), lambda i, j: tuple(lead) + (i, j))],
        out_specs=pl.BlockSpec((tr, tc), lambda i, j: (i, j)),
        out_shape=jax.ShapeDtypeStruct((r, c), BF16),
        compiler_params=_params("parallel", "parallel"),
        name="weight_cast",
    )(w)


def _adaln_kernel(x_ref, mod_ref, gain_ref, o_ref, *, rows3):
    gain_row, shift_row, scale_row = rows3
    gain = gain_ref[gain_row:gain_row + 1, :]
    scale1 = 1.0 + mod_ref[0, scale_row:scale_row + 1, :]
    shift = mod_ref[0, shift_row:shift_row + 1, :]
    rows = min(NORM_ROWS, x_ref.shape[0])

    def body(r, carry):
        sl = pl.ds(pl.multiple_of(r * rows, rows), rows)
        x = x_ref[sl, :]
        y = x * lax.rsqrt(jnp.mean(x * x, axis=-1, keepdims=True) + EPS)
        o_ref[sl, :] = ((y * gain) * scale1 + shift).astype(o_ref.dtype)
        return carry

    lax.fori_loop(0, x_ref.shape[0] // rows, body, 0)


def _adaln(x2d, mod, gain, rows3, rows_per_mod):
    m, d = x2d.shape
    tr = _pick(rows_per_mod, (256, 128, 64, 32, 16, 8))
    bpm = rows_per_mod // tr
    return pl.pallas_call(
        functools.partial(_adaln_kernel, rows3=rows3),
        grid=(m // tr,),
        in_specs=[
            pl.BlockSpec((tr, d), lambda i: (i, 0)),
            pl.BlockSpec((1, N_MOD, d), lambda i: (i // bpm, 0, 0)),
            pl.BlockSpec(gain.shape, lambda i: (0, 0)),
        ],
        out_specs=pl.BlockSpec((tr, d), lambda i: (i, 0)),
        out_shape=jax.ShapeDtypeStruct((m, d), BF16),
        compiler_params=_params("parallel"),
        name="adaln",
    )(x2d, mod, gain)


def _tiles(rows_per_mod, n, a_bytes_per_row, w_bytes_per_col, io_bytes_per_elem):
    tm = _pick(rows_per_mod, (MATMUL_ROWS, 512, 256, 128, 64, 32, 16, 8))
    for tn in (512, 256, 128):
        vmem = 2 * (tm * a_bytes_per_row + tn * w_bytes_per_col + tm * tn * io_bytes_per_elem)
        if n % tn == 0 and vmem <= MATMUL_VMEM_BUDGET:
            return tm, tn
    return tm, 128


def _up_kernel(h_ref, wg_ref, wu_ref, o_ref):
    h = h_ref[...]
    g = jnp.dot(h, wg_ref[...], preferred_element_type=F32)
    u = jnp.dot(h, wu_ref[...], preferred_element_type=F32)
    o_ref[...] = (jax.nn.silu(g) * u).astype(o_ref.dtype)


def _proj_kernel(h_ref, w_ref, o_ref):
    o_ref[...] = jnp.dot(h_ref[...], w_ref[...], preferred_element_type=F32).astype(o_ref.dtype)


def _ffn_up(h, w1):
    m, d = h.shape
    f = w1.shape[1] // 2
    tm, tn = _tiles(m, f, 2 * d, 2 * 2 * d, 2)
    nj = f // tn
    return pl.pallas_call(
        _up_kernel,
        grid=(m // tm, nj),
        in_specs=[
            pl.BlockSpec((tm, d), lambda i, j: (i, 0)),
            pl.BlockSpec((d, tn), lambda i, j: (0, j)),
            pl.BlockSpec((d, tn), lambda i, j: (0, j + nj)),
        ],
        out_specs=pl.BlockSpec((tm, tn), lambda i, j: (i, j)),
        out_shape=jax.ShapeDtypeStruct((m, f), BF16),
        compiler_params=_params("parallel", "arbitrary"),
        name="ffn_up",
    )(h, w1, w1)


def _proj(h, w, name):
    m, d = h.shape
    n = w.shape[1]
    tm, tn = _tiles(m, n, 2 * d, 2 * d, 4)
    return pl.pallas_call(
        _proj_kernel,
        grid=(m // tm, n // tn),
        in_specs=[
            pl.BlockSpec((tm, d), lambda i, j: (i, 0)),
            pl.BlockSpec((d, tn), lambda i, j: (0, j)),
        ],
        out_specs=pl.BlockSpec((tm, tn), lambda i, j: (i, j)),
        out_shape=jax.ShapeDtypeStruct((m, n), F32),
        compiler_params=_params("parallel", "arbitrary"),
        name=name,
    )(h, w)


def _down_kernel(*refs, n_a, gate_row, coef):
    a_refs = refs[:n_a]
    w_refs = refs[n_a:2 * n_a]
    x_ref, mod_ref, o_ref = refs[2 * n_a:]
    acc = jnp.dot(a_refs[0][...], w_refs[0][...], preferred_element_type=F32)
    for a_ref, w_ref in zip(a_refs[1:], w_refs[1:]):
        acc = acc + jnp.dot(a_ref[...], w_ref[...], preferred_element_type=F32)
    gate = mod_ref[0, gate_row:gate_row + 1, :]
    if coef != 1.0:
        gate = coef * gate
    o_ref[...] = x_ref[...] + gate * acc


def _down(a_list, w, x2d, mod, gate_row, coef, rows_per_mod, name):
    m, d = x2d.shape
    n_a = len(a_list)
    kk = a_list[0].shape[1]
    tm, tn = _tiles(rows_per_mod, d, 2 * kk * n_a, 2 * kk * n_a, 2 * 4)
    bpm = rows_per_mod // tm
    in_specs = [pl.BlockSpec((tm, kk), lambda i, j: (i, 0)) for _ in a_list]
    in_specs += [pl.BlockSpec((kk, tn), functools.partial(lambda i, j, r: (r, j), r=r)) for r in range(n_a)]
    in_specs += [
        pl.BlockSpec((tm, tn), lambda i, j: (i, j)),
        pl.BlockSpec((1, N_MOD, tn), lambda i, j: (i // bpm, 0, j)),
    ]
    return pl.pallas_call(
        functools.partial(_down_kernel, n_a=n_a, gate_row=gate_row, coef=coef),
        grid=(m // tm, d // tn),
        in_specs=in_specs,
        out_specs=pl.BlockSpec((tm, tn), lambda i, j: (i, j)),
        out_shape=jax.ShapeDtypeStruct((m, d), F32),
        compiler_params=_params("parallel", "arbitrary"),
        name=name,
    )(*a_list, *([w] * n_a), x2d, mod)


def _head_norm(x, gain):
    return x * lax.rsqrt(jnp.mean(x * x, axis=-1, keepdims=True) + EPS) * gain


def _rope(x, c, s_hi, s_lo):
    return x * c + pltpu.roll(x, HEAD_DIM - ROPE_AXIS_DIM // 2, 1) * s_hi + pltpu.roll(x, ROPE_AXIS_DIM // 2, 1) * s_lo


def _attn_kernel(*refs, kv_rows, rope):
    n_kv = len(kv_rows)
    q_ref = refs[0]
    kv_refs = refs[1:1 + 2 * n_kv]
    qg_ref, kg_ref = refs[1 + 2 * n_kv:3 + 2 * n_kv]
    pos = 3 + 2 * n_kv
    if rope:
        cq_ref, hq_ref, lq_ref, ck_ref, hk_ref, lk_ref = refs[pos:pos + 6]
        pos += 6
    o_ref, k_scr, v_scr = refs[pos:pos + 3]

    @pl.when(pl.program_id(2) == 0)
    def _():
        off = 0
        for part, rows in enumerate(kv_rows):
            kn = _head_norm(kv_refs[2 * part][...], kg_ref[...])
            if rope and part == 0:
                kn = _rope(kn, ck_ref[...], hk_ref[...], lk_ref[...])
            k_scr[off:off + rows, :] = kn.astype(BF16)
            v_scr[off:off + rows, :HEAD_DIM] = kv_refs[2 * part + 1][...].astype(BF16)
            off += rows
        v_scr[:, HEAD_DIM:] = jnp.ones((off, HEAD_DIM), BF16)

    scale = HEAD_DIM ** -0.5 * LOG2_E
    for g in range(KV_GROUP):
        cols = slice(g * HEAD_DIM, (g + 1) * HEAD_DIM)
        q = _head_norm(q_ref[:, cols], qg_ref[...])
        if rope:
            q = _rope(q, cq_ref[...], hq_ref[...], lq_ref[...])
        q = (q * scale).astype(BF16)
        s = lax.dot_general(q, k_scr[...], (((1,), (1,)), ((), ())), preferred_element_type=F32)
        p = jnp.exp2(s - jnp.max(s, axis=-1, keepdims=True))
        o = jnp.dot(p.astype(BF16), v_scr[...], preferred_element_type=F32)
        o_ref[:, cols] = (o[:, :HEAD_DIM] / o[:, HEAD_DIM:]).astype(o_ref.dtype)


def _attention(pq, kv_sources, q_gain, k_gain, tables, batch, q_col0, k_col0, v_col0):
    t = pq.shape[0] // batch
    n_kv_heads = (k_col0 - q_col0) // (KV_GROUP * HEAD_DIM)
    tq = _pick(t, (256, 128, 64, 32, 16, 8))
    nq = t // tq
    gw = KV_GROUP * HEAD_DIM
    kv_rows = tuple(src.shape[0] // batch for src in kv_sources)
    rope = tables is not None
    in_specs = [pl.BlockSpec((tq, gw), lambda b, h, i: (b * nq + i, q_col0 // gw + h))]
    args = [pq]
    for src, rows in zip(kv_sources, kv_rows):
        in_specs.append(pl.BlockSpec((rows, HEAD_DIM), lambda b, h, i: (b, k_col0 // HEAD_DIM + h)))
        in_specs.append(pl.BlockSpec((rows, HEAD_DIM), lambda b, h, i: (b, v_col0 // HEAD_DIM + h)))
        args += [src, src]
    in_specs += [pl.BlockSpec((1, HEAD_DIM), lambda b, h, i: (0, 0))] * 2
    args += [q_gain.reshape(1, HEAD_DIM), k_gain.reshape(1, HEAD_DIM)]
    if rope:
        in_specs += [pl.BlockSpec((tq, HEAD_DIM), lambda b, h, i: (i, 0))] * 3
        in_specs += [pl.BlockSpec((t, HEAD_DIM), lambda b, h, i: (0, 0))] * 3
        args += list(tables) * 2
    return pl.pallas_call(
        functools.partial(_attn_kernel, kv_rows=kv_rows, rope=rope),
        grid=(batch, n_kv_heads, nq),
        in_specs=in_specs,
        out_specs=pl.BlockSpec((tq, gw), lambda b, h, i: (b * nq + i, h)),
        out_shape=jax.ShapeDtypeStruct((batch * t, n_kv_heads * gw), BF16),
        scratch_shapes=[pltpu.VMEM((sum(kv_rows), HEAD_DIM), BF16), pltpu.VMEM((sum(kv_rows), 2 * HEAD_DIM), BF16)],
        compiler_params=_params("parallel", "parallel", "arbitrary"),
        name="gqa_rope" if rope else "gqa_ctx",
    )(*args)


def _rope_tables(t):
    rows = t // GRID_W
    row_ids = np.repeat(np.arange(rows, dtype=np.float32), GRID_W)
    col_ids = np.tile(np.arange(GRID_W, dtype=np.float32), rows)
    inv_freq = jnp.asarray(ROPE_THETA, F32) ** (-jnp.arange(0, ROPE_AXIS_DIM, 2, dtype=F32) / ROPE_AXIS_DIM)
    ang_r = jnp.asarray(row_ids)[:, None] * inv_freq
    ang_c = jnp.asarray(col_ids)[:, None] * inv_freq
    zero = jnp.zeros_like(ang_r)
    cos = jnp.concatenate([jnp.cos(ang_r), jnp.cos(ang_r), jnp.cos(ang_c), jnp.cos(ang_c)], axis=1)
    s_hi = jnp.concatenate([-jnp.sin(ang_r), zero, -jnp.sin(ang_c), zero], axis=1)
    s_lo = jnp.concatenate([zero, jnp.sin(ang_r), zero, jnp.sin(ang_c)], axis=1)
    return cos, s_hi, s_lo


def _chunk_tri(rev):
    idx = np.arange(HGRN_BLOCK)
    same = (idx[:, None] // HGRN_CHUNK) == (idx[None, :] // HGRN_CHUNK)
    tri = (idx[None, :] >= idx[:, None]) if rev else (idx[None, :] <= idx[:, None])
    return jnp.asarray((same & tri).astype(np.float32), BF16)


def _chunk_cols():
    idx = np.arange(HGRN_BLOCK)
    col = np.arange((HGRN_BLOCK // HGRN_CHUNK) * HEAD_DIM)
    return jnp.asarray(((idx[:, None] // HGRN_CHUNK) == (col[None, :] // HEAD_DIM)).astype(np.float32), BF16)


def _hgrn_kernel(*refs, rev, final, n_heads):
    q_ref, z_ref, v_ref, lb_ref, tri_ref, cmask_ref, s0_ref = refs[:7]
    pos = 7
    if final:
        oprev_ref, g_ref, gain_ref = refs[7:10]
        pos = 10
    o_ref, sfin_ref, st_scr = refs[pos:pos + 3]
    n_chunks = HGRN_BLOCK // HGRN_CHUNK
    ref_row = HGRN_CHUNK - 1 - HGRN_CHUNK // 2 if rev else HGRN_CHUNK // 2
    end_row = 0 if rev else HGRN_CHUNK - 1

    @pl.when(pl.program_id(2) == 0)
    def _():
        st_scr[...] = s0_ref[0]

    gw = n_heads * HEAD_DIM
    tri = tri_ref[...]
    in_chunk = tri > 0
    z = z_ref[...]
    lb = lb_ref[...]
    sig = jax.nn.sigmoid(z)
    log_f = jnp.log(lb + (1.0 - lb) * sig)
    k = (1.0 - lb) * (1.0 - sig)
    q = jax.nn.silu(q_ref[...])
    v = v_ref[...]
    g1 = log_f.astype(BF16)
    r1 = log_f - g1.astype(F32)
    g2 = r1.astype(BF16)
    g3 = (r1 - g2.astype(F32)).astype(BF16)
    cum = (jnp.dot(tri, g1, preferred_element_type=F32) + jnp.dot(tri, g2, preferred_element_type=F32)
           + jnp.dot(tri, g3, preferred_element_type=F32))
    shape3 = (n_chunks, HGRN_CHUNK, gw)
    cum3 = cum.reshape(shape3)
    ref3 = cum3[:, ref_row:ref_row + 1, :]
    end3 = cum3[:, end_row:end_row + 1, :]
    rel = cum3 - ref3
    qa3 = q.reshape(shape3) * jnp.exp(rel)
    ka3 = k.reshape(shape3) * jnp.exp(-rel)
    qa = qa3.reshape(cum.shape).astype(BF16)
    ka = ka3.reshape(cum.shape).astype(BF16)
    qe = (qa3 * jnp.exp(ref3)).reshape(cum.shape).astype(BF16)
    kd = (ka3 * jnp.exp(end3 - ref3)).reshape(cum.shape).astype(BF16)
    dec = jnp.exp(end3)
    vb = v.astype(BF16)
    chunk_cols = cmask_ref[...] > 0

    heads = []
    for h in range(n_heads):
        cols = slice(h * HEAD_DIM, (h + 1) * HEAD_DIM)
        attn = lax.dot_general(qa[:, cols], ka[:, cols], (((1,), (1,)), ((), ())), preferred_element_type=F32)
        attn = jnp.where(in_chunk, attn, 0.0)
        o_intra = jnp.dot(attn.astype(BF16), vb[:, cols], preferred_element_type=F32)
        kd_wide = jnp.where(chunk_cols, jnp.concatenate([kd[:, cols]] * n_chunks, axis=1), jnp.zeros((), BF16))
        upd = jnp.dot(v[:, cols].T.astype(BF16), kd_wide, preferred_element_type=F32)
        heads.append(dict(o_intra=o_intra, upd=upd, state=st_scr[h]))

    order = range(n_chunks - 1, -1, -1) if rev else range(n_chunks)
    for c in order:
        rows = slice(c * HGRN_CHUNK, (c + 1) * HGRN_CHUNK)
        for h, hd in enumerate(heads):
            cols = slice(h * HEAD_DIM, (h + 1) * HEAD_DIM)
            inter = lax.dot_general(qe[rows, cols], hd["state"].astype(BF16), (((1,), (1,)), ((), ())),
                                    preferred_element_type=F32)
            o = hd["o_intra"][rows] + inter
            hd["state"] = hd["state"] * dec[c][:, cols] + hd["upd"][:, c * HEAD_DIM:(c + 1) * HEAD_DIM]
            if final:
                o = o + oprev_ref[rows, cols]
                y = o * lax.rsqrt(jnp.mean(o * o, axis=-1, keepdims=True) + EPS) * gain_ref[...]
                o = y * jax.nn.silu(g_ref[rows, cols])
            o_ref[rows, cols] = o.astype(o_ref.dtype)
    for h, hd in enumerate(heads):
        st_scr[h] = hd["state"]
        sfin_ref[0, h] = hd["state"]


def _hgrn_scan(p, lb_row, s0, batch, cols, rev, o_prev=None, out_gain=None):
    t = p.shape[0] // batch
    nsb = t // HGRN_BLOCK
    n_heads = s0.shape[1]
    hps = next(g for g in (HGRN_HEADS_PER_STEP, 2, 1)
               if n_heads % g == 0 and all(c % (g * HEAD_DIM) == 0 for c in cols))
    gw = hps * HEAD_DIM
    final = o_prev is not None
    q0, z0, v0, g0 = (c // gw for c in cols)

    def row_blk(b, s):
        return b * nsb + (nsb - 1 - s if rev else s)

    blk = (HGRN_BLOCK, gw)
    st_blk = (1, hps, HEAD_DIM, HEAD_DIM)
    in_specs = [
        pl.BlockSpec(blk, lambda b, h, s: (row_blk(b, s), q0 + h)),
        pl.BlockSpec(blk, lambda b, h, s: (row_blk(b, s), z0 + h)),
        pl.BlockSpec(blk, lambda b, h, s: (row_blk(b, s), v0 + h)),
        pl.BlockSpec((1, gw), lambda b, h, s: (0, h)),
        pl.BlockSpec((HGRN_BLOCK, HGRN_BLOCK), lambda b, h, s: (0, 0)),
        pl.BlockSpec((HGRN_BLOCK, (HGRN_BLOCK // HGRN_CHUNK) * HEAD_DIM), lambda b, h, s: (0, 0)),
        pl.BlockSpec(st_blk, lambda b, h, s: (b, h, 0, 0)),
    ]
    args = [p, p, p, lb_row, _chunk_tri(rev), _chunk_cols(), s0]
    if final:
        in_specs += [
            pl.BlockSpec(blk, lambda b, h, s: (row_blk(b, s), h)),
            pl.BlockSpec(blk, lambda b, h, s: (row_blk(b, s), g0 + h)),
            pl.BlockSpec((1, HEAD_DIM), lambda b, h, s: (0, 0)),
        ]
        args += [o_prev, p, out_gain.reshape(1, HEAD_DIM)]
    return pl.pallas_call(
        functools.partial(_hgrn_kernel, rev=rev, final=final, n_heads=hps),
        grid=(batch, n_heads // hps, nsb),
        in_specs=in_specs,
        out_specs=[
            pl.BlockSpec(blk, lambda b, h, s: (row_blk(b, s), h)),
            pl.BlockSpec(st_blk, lambda b, h, s: (b, h, 0, 0)),
        ],
        out_shape=[
            jax.ShapeDtypeStruct((batch * t, n_heads * HEAD_DIM), BF16 if final else F32),
            jax.ShapeDtypeStruct(s0.shape, F32),
        ],
        scratch_shapes=[pltpu.VMEM((hps, HEAD_DIM, HEAD_DIM), F32)],
        compiler_params=_params("parallel", "parallel", "arbitrary"),
        name="hgrn_bwd" if rev else "hgrn_fwd",
    )(*args)


def _conv_kernel(b_ref, c_ref, u_ref, w_ref, o_ref):
    u = c_ref[...] * u_ref[...]
    t = u.shape[0]
    row = lax.broadcasted_iota(jnp.int32, u.shape, 0)
    prev = jnp.where(row == 0, 0.0, pltpu.roll(u, 1, 0))
    nxt = jnp.where(row == t - 1, 0.0, pltpu.roll(u, t - 1, 0))
    y = prev * w_ref[0:1, :] + u * w_ref[1:2, :] + nxt * w_ref[2:3, :]
    o_ref[...] = (b_ref[...] * y).astype(o_ref.dtype)


def _gated_conv(p, conv_w, batch):
    d = conv_w.shape[1]
    t = p.shape[0] // batch
    td = _pick(d, (256, 128))
    nd = d // td
    return pl.pallas_call(
        _conv_kernel,
        grid=(batch, nd),
        in_specs=[
            pl.BlockSpec((t, td), lambda b, j: (b, j)),
            pl.BlockSpec((t, td), lambda b, j: (b, nd + j)),
            pl.BlockSpec((t, td), lambda b, j: (b, 2 * nd + j)),
            pl.BlockSpec((conv_w.shape[0], td), lambda b, j: (0, j)),
        ],
        out_specs=pl.BlockSpec((t, td), lambda b, j: (b, j)),
        out_shape=jax.ShapeDtypeStruct((batch * t, d), BF16),
        compiler_params=_params("parallel", "parallel"),
        name="gated_conv3",
    )(p, p, p, conv_w)


def kernel(x, c, ctx, c_ctx, mod_w, mod_b, norm_g, ffn_w1, ffn_w2, ab_w_in, ab_w_out, attn_q_gain, attn_k_gain, hgrn_lb_logits, hgrn_out_gain, conv_w_in, conv_w, conv_w_out):
    batch, seq, d = x.shape
    ctx_len = ctx.shape[1]
    depth = mod_w.shape[0]
    attn_w = d // 2
    kv_w = attn_w // KV_GROUP
    hg_w = d - attn_w
    n_hg_heads = hg_w // HEAD_DIM
    col_q, col_k, col_v = 0, attn_w, attn_w + kv_w
    col_qb = attn_w + 2 * kv_w
    col_zf, col_zb, col_ib, col_gb = (col_qb + hg_w * i for i in range(1, 5))

    lb_table = jnp.cumsum(jax.nn.softmax(hgrn_lb_logits.astype(F32), axis=1), axis=1)
    tables = _rope_tables(seq)

    pad = (-(batch + 1)) % 8
    cc = jnp.concatenate([c, c_ctx[None], jnp.zeros((pad, d), F32)], axis=0)
    mod_all = _mod_all(cc, mod_w, mod_b).reshape(depth, batch + 1 + pad, N_MOD, d)

    xl = x.reshape(batch * seq, d)
    xc = ctx.reshape(batch * ctx_len, d)
    n_ctx = batch * ctx_len

    def ffn(xs, mod, gain, w1, w2, rows3, gate_row, rpm):
        a = _ffn_up(_adaln(xs, mod, gain, rows3, rpm), w1)
        return _down([a], w2, xs, mod, gate_row, 0.5, rpm, "ffn_down")

    for layer in range(depth):
        last = layer == depth - 1
        even = layer % 2 == 0
        ctx_needed = even or not last
        gain = norm_g[layer]
        mod_l = mod_all[layer, :batch]
        mod_c = mod_all[layer, batch:batch + 1]
        w1a, w1b = _weight_bf16(ffn_w1, (layer, 0)), _weight_bf16(ffn_w1, (layer, 1))
        w2a, w2b = _weight_bf16(ffn_w2, (layer, 0)), _weight_bf16(ffn_w2, (layer, 1))

        xl = ffn(xl, mod_l, gain, w1a, w2a, (0, 0, 1), 2, seq)
        if ctx_needed:
            xc = ffn(xc, mod_c, gain, w1a, w2a, (0, 0, 1), 2, n_ctx)

        if even:
            e = layer // 2
            w_in = _weight_bf16(ab_w_in, (e,))
            w_out = _weight_bf16(ab_w_out, (e,))
            pl_ = _proj(_adaln(xl, mod_l, gain, (1, 3, 4), seq), w_in, "ab_proj")
            pc_ = _proj(_adaln(xc, mod_c, gain, (1, 3, 4), n_ctx), w_in, "ab_proj")
            o_attn = _attention(pl_, [pl_, pc_], attn_q_gain[e], attn_k_gain[e], tables, batch, col_q, col_k, col_v)
            s_zero = jnp.zeros((batch, n_hg_heads, HEAD_DIM, HEAD_DIM), F32)
            lbs = [lb_table[dd, layer].reshape(1, hg_w) for dd in range(2)]
            zcols = (col_zf, col_zb)
            sc_dir = []
            o_prev = None
            for dd in range(2):
                o_prev, s_c = _hgrn_scan(pc_, lbs[dd], s_zero, batch, (col_qb, zcols[dd], col_ib, col_gb), dd == 1,
                                         o_prev, hgrn_out_gain[e] if dd == 1 else None)
                sc_dir.append(s_c)
            o_rec_c = o_prev
            o_prev = None
            for dd in range(2):
                o_prev, _ = _hgrn_scan(pl_, lbs[dd], sc_dir[dd], batch, (col_qb, zcols[dd], col_ib, col_gb), dd == 1,
                                       o_prev, hgrn_out_gain[e] if dd == 1 else None)
            o_rec = o_prev
            xl = _down([o_attn, o_rec], w_out, xl, mod_l, 5, 1.0, seq, "mixer_out")
            if not last:
                o_attn_c = _attention(pc_, [pc_], attn_q_gain[e], attn_k_gain[e], None, batch, col_q, col_k, col_v)
                xc = _down([o_attn_c, o_rec_c], w_out, xc, mod_c, 5, 1.0, n_ctx, "mixer_out")
        else:
            o = layer // 2
            w_in = _weight_bf16(conv_w_in, (o,))
            w_out = _weight_bf16(conv_w_out, (o,))
            yl = _gated_conv(_proj(_adaln(xl, mod_l, gain, (1, 3, 4), seq), w_in, "conv_proj"), conv_w[o], batch)
            xl = _down([yl], w_out, xl, mod_l, 5, 1.0, seq, "mixer_out")
            if not last:
                yc = _gated_conv(_proj(_adaln(xc, mod_c, gain, (1, 3, 4), n_ctx), w_in, "conv_proj"), conv_w[o], batch)
                xc = _down([yc], w_out, xc, mod_c, 5, 1.0, n_ctx, "mixer_out")

        xl = ffn(xl, mod_l, gain, w1b, w2b, (2, 6, 7), 8, seq)
        if not last:
            xc = ffn(xc, mod_c, gain, w1b, w2b, (2, 6, 7), 8, n_ctx)

    return xl.reshape(batch, seq, d)
```

```python
import functools

import numpy as np
import jax
import jax.numpy as jnp
from jax import lax
from jax.experimental import pallas as pl
from jax.experimental.pallas import tpu as pltpu

GRID_W = 64
HEAD_DIM = 128
KV_GROUP = 4
ROPE_THETA = 10000.0
ROPE_AXIS_DIM = HEAD_DIM // 2
HGRN_CHUNK = 32
N_MOD = 9
EPS = 1e-6
LOG2_E = 1.4426950408889634

VMEM_LIMIT_BYTES = 56 * 1024 * 1024
MATMUL_VMEM_BUDGET = 46 * 1024 * 1024
MATMUL_ROWS = 1024
ATTN_Q_ROWS = 256
HGRN_BLOCK = 256
HGRN_HEADS_PER_STEP = 4
CAST_BLOCK_BYTES = 8 * 1024 * 1024
CAST_SLAB_ROWS = 16

BF16 = jnp.bfloat16
F32 = jnp.float32


def _params(*sem):
    return pltpu.CompilerParams(dimension_semantics=sem, vmem_limit_bytes=VMEM_LIMIT_BYTES)


def _pick(n, candidates):
    for c in candidates:
        if n % c == 0:
            return c
    return n


def _mod_kernel(c_ref, w_ref, b_ref, o_ref):
    a = jax.nn.silu(c_ref[...]).astype(BF16)
    o_ref[0] = jnp.dot(a, w_ref[0].astype(BF16), preferred_element_type=F32) + b_ref[0]


def _mod_all(cc, mod_w, mod_b):
    depth, d, n = mod_w.shape
    rows = cc.shape[0]
    tn = _pick(n, (512, 256, 128))
    return pl.pallas_call(
        _mod_kernel,
        grid=(depth, n // tn),
        in_specs=[
            pl.BlockSpec((rows, d), lambda l, j: (0, 0)),
            pl.BlockSpec((1, d, tn), lambda l, j: (l, 0, j)),
            pl.BlockSpec((1, 1, tn), lambda l, j: (l, 0, j)),
        ],
        out_specs=pl.BlockSpec((1, rows, tn), lambda l, j: (l, 0, j)),
        out_shape=jax.ShapeDtypeStruct((depth, rows, n), F32),
        compiler_params=_params("parallel", "parallel"),
        name="mod_proj",
    )(cc, mod_w, mod_b.reshape(depth, 1, n))


def _cast_kernel(w_ref, o_ref):
    o_ref[...] = w_ref[...].astype(o_ref.dtype)


def _weight_bf16(w, lead):
    r, c = w.shape[-2:]
    tc = _pick(c, (4096, 2048, 1024, 512, 256, 128))
    tr = _pick(r, tuple(t for t in (2048, 1024, 512, 256, 128, 64, 32, 16, 8) if t * tc * 4 <= CAST_BLOCK_BYTES))
    return pl.pallas_call(
        _cast_kernel,
        grid=(r // tr, c // tc),
        in_specs=[pl.BlockSpec((None,) * len(lead) + (tr, tc), lambda i, j: tuple(lead) + (i, j))],
        out_specs=pl.BlockSpec((tr, tc), lambda i, j: (i, j)),
        out_shape=jax.ShapeDtypeStruct((r, c), BF16),
        compiler_params=_params("parallel", "parallel"),
        name="weight_cast",
    )(w)


def _adaln_kernel(x_ref, mod_ref, gain_ref, o_ref, *, rows3):
    gain_row, shift_row, scale_row = rows3
    x = x_ref[...]
    y = x * lax.rsqrt(jnp.mean(x * x, axis=-1, keepdims=True) + EPS)
    y = y * gain_ref[gain_row:gain_row + 1, :]
    h = y * (1.0 + mod_ref[0, scale_row:scale_row + 1, :]) + mod_ref[0, shift_row:shift_row + 1, :]
    o_ref[...] = h.astype(o_ref.dtype)


def _adaln(x2d, mod, gain, rows3, rows_per_mod):
    m, d = x2d.shape
    tr = _pick(rows_per_mod, (256, 128, 64, 32, 16, 8))
    bpm = rows_per_mod // tr
    return pl.pallas_call(
        functools.partial(_adaln_kernel, rows3=rows3),
        grid=(m // tr,),
        in_specs=[
            pl.BlockSpec((tr, d), lambda i: (i, 0)),
            pl.BlockSpec((1, N_MOD, d), lambda i: (i // bpm, 0, 0)),
            pl.BlockSpec(gain.shape, lambda i: (0, 0)),
        ],
        out_specs=pl.BlockSpec((tr, d), lambda i: (i, 0)),
        out_shape=jax.ShapeDtypeStruct((m, d), BF16),
        compiler_params=_params("parallel"),
        name="adaln",
    )(x2d, mod, gain)


def _tiles(rows_per_mod, n, a_bytes_per_row, w_bytes_per_col, io_bytes_per_elem):
    tm = _pick(rows_per_mod, (MATMUL_ROWS, 512, 256, 128, 64, 32, 16, 8))
    for tn in (1024, 512, 256, 128):
        vmem = 2 * (tm * a_bytes_per_row + tn * w_bytes_per_col + tm * tn * io_bytes_per_elem)
        if n % tn == 0 and vmem <= MATMUL_VMEM_BUDGET:
            return tm, tn
    return tm, 128


def _up_kernel(h_ref, wg_ref, wu_ref, o_ref):
    h = h_ref[...]
    g = jnp.dot(h, wg_ref[...], preferred_element_type=F32)
    u = jnp.dot(h, wu_ref[...], preferred_element_type=F32)
    o_ref[...] = (jax.nn.silu(g) * u).astype(o_ref.dtype)


def _proj_kernel(h_ref, w_ref, o_ref):
    o_ref[...] = jnp.dot(h_ref[...], w_ref[...], preferred_element_type=F32).astype(o_ref.dtype)


def _up_cast_kernel(h_ref, wg_ref, wu_ref, src_ref, o_ref, dst_ref):
    dst_ref[...] = src_ref[...].astype(dst_ref.dtype)
    _up_kernel(h_ref, wg_ref, wu_ref, o_ref)


def _ffn_up(h, w1, cast=None):
    m, d = h.shape
    f = w1.shape[1] // 2
    tm, tn = _tiles(m, f, 2 * d, 2 * 2 * d, 2)
    nj = f // tn
    steps = (m // tm) * nj
    in_specs = [
        pl.BlockSpec((tm, d), lambda i, j: (i, 0)),
        pl.BlockSpec((d, tn), lambda i, j: (0, j)),
        pl.BlockSpec((d, tn), lambda i, j: (0, j + nj)),
    ]
    out_specs = [pl.BlockSpec((tm, tn), lambda i, j: (i, j))]
    out_shape = [jax.ShapeDtypeStruct((m, f), BF16)]
    args = [h, w1, w1]
    if cast is not None:
        w, lead = cast
        r, c = w.shape[-2:]
        slab = next(t for t in range(CAST_SLAB_ROWS, r + 1, CAST_SLAB_ROWS) if r % t == 0 and r // t <= steps)
        last = r // slab - 1
        in_specs.append(pl.BlockSpec((None,) * len(lead) + (slab, c),
                                     lambda i, j: tuple(lead) + (jnp.minimum(i * nj + j, last), 0)))
        out_specs.append(pl.BlockSpec((slab, c), lambda i, j: (jnp.minimum(i * nj + j, last), 0)))
        out_shape.append(jax.ShapeDtypeStruct((r, c), BF16))
        args.append(w)
    outs = pl.pallas_call(
        _up_kernel if cast is None else _up_cast_kernel,
        grid=(m // tm, nj),
        in_specs=in_specs,
        out_specs=out_specs,
        out_shape=out_shape,
        compiler_params=_params("parallel", "arbitrary"),
        name="ffn_up",
    )(*args)
    return outs[0] if cast is None else tuple(outs)


def _proj(h, w, name):
    m, d = h.shape
    n = w.shape[1]
    tm, tn = _tiles(m, n, 2 * d, 2 * d, 4)
    return pl.pallas_call(
        _proj_kernel,
        grid=(m // tm, n // tn),
        in_specs=[
            pl.BlockSpec((tm, d), lambda i, j: (i, 0)),
            pl.BlockSpec((d, tn), lambda i, j: (0, j)),
        ],
        out_specs=pl.BlockSpec((tm, tn), lambda i, j: (i, j)),
        out_shape=jax.ShapeDtypeStruct((m, n), F32),
        compiler_params=_params("parallel", "arbitrary"),
        name=name,
    )(h, w)


def _down_kernel(*refs, n_a, coef):
    a_refs = refs[:n_a]
    w_refs = refs[n_a:2 * n_a]
    x_ref, gate_ref, o_ref = refs[2 * n_a:]
    acc = jnp.dot(a_refs[0][...], w_refs[0][...], preferred_element_type=F32)
    for a_ref, w_ref in zip(a_refs[1:], w_refs[1:]):
        acc = acc + jnp.dot(a_ref[...], w_ref[...], preferred_element_type=F32)
    gate = gate_ref[0, pl.program_id(1)]
    if coef != 1.0:
        gate = coef * gate
    o_ref[...] = x_ref[...] + gate * acc


def _down(a_list, w, x2d, mod, gate_row, coef, rows_per_mod, name):
    m, d = x2d.shape
    n_a = len(a_list)
    kk = a_list[0].shape[1]
    tm, tn = _tiles(rows_per_mod, d, 2 * kk * n_a, 2 * kk * n_a, 2 * 4)
    bpm = rows_per_mod // tm
    nj = d // tn
    gate = mod[:, gate_row].reshape(mod.shape[0], nj, 1, tn)
    in_specs = [pl.BlockSpec((tm, kk), lambda i, j: (i, 0)) for _ in a_list]
    in_specs += [pl.BlockSpec((kk, tn), functools.partial(lambda i, j, r: (r, j), r=r)) for r in range(n_a)]
    in_specs += [
        pl.BlockSpec((tm, tn), lambda i, j: (i, j)),
        pl.BlockSpec((1, nj, 1, tn), lambda i, j: (i // bpm, 0, 0, 0)),
    ]
    return pl.pallas_call(
        functools.partial(_down_kernel, n_a=n_a, coef=coef),
        grid=(m // tm, d // tn),
        in_specs=in_specs,
        out_specs=pl.BlockSpec((tm, tn), lambda i, j: (i, j)),
        out_shape=jax.ShapeDtypeStruct((m, d), F32),
        compiler_params=_params("parallel", "arbitrary"),
        name=name,
    )(*a_list, *([w] * n_a), x2d, gate)


def _head_norm(x, gain):
    return x * lax.rsqrt(jnp.mean(x * x, axis=-1, keepdims=True) + EPS) * gain


def _rope(x, c, s_hi, s_lo):
    return x * c + pltpu.roll(x, HEAD_DIM - ROPE_AXIS_DIM // 2, 1) * s_hi + pltpu.roll(x, ROPE_AXIS_DIM // 2, 1) * s_lo


def _attn_kernel(*refs, kv_rows, rope):
    n_kv = len(kv_rows)
    q_ref = refs[0]
    kv_refs = refs[1:1 + 2 * n_kv]
    qg_ref, kg_ref = refs[1 + 2 * n_kv:3 + 2 * n_kv]
    pos = 3 + 2 * n_kv
    if rope:
        cq_ref, hq_ref, lq_ref, ck_ref, hk_ref, lk_ref = refs[pos:pos + 6]
        pos += 6
    o_ref, k_scr, v_scr = refs[pos:pos + 3]

    @pl.when(pl.program_id(2) == 0)
    def _():
        off = 0
        for part, rows in enumerate(kv_rows):
            kn = _head_norm(kv_refs[2 * part][...], kg_ref[...])
            if rope and part == 0:
                kn = _rope(kn, ck_ref[...], hk_ref[...], lk_ref[...])
            k_scr[off:off + rows, :] = kn.astype(BF16)
            v_scr[off:off + rows, :HEAD_DIM] = kv_refs[2 * part + 1][...].astype(BF16)
            off += rows
        v_scr[:, HEAD_DIM:] = jnp.ones((off, HEAD_DIM), BF16)

    scale = HEAD_DIM ** -0.5 * LOG2_E
    for g in range(KV_GROUP):
        cols = slice(g * HEAD_DIM, (g + 1) * HEAD_DIM)
        q = _head_norm(q_ref[:, cols], qg_ref[...])
        if rope:
            q = _rope(q, cq_ref[...], hq_ref[...], lq_ref[...])
        q = (q * scale).astype(BF16)
        s = lax.dot_general(q, k_scr[...], (((1,), (1,)), ((), ())), preferred_element_type=F32)
        p = jnp.exp2(s - jnp.max(s, axis=-1, keepdims=True))
        o = jnp.dot(p.astype(BF16), v_scr[...], preferred_element_type=F32)
        o_ref[:, cols] = (o[:, :HEAD_DIM] / o[:, HEAD_DIM:]).astype(o_ref.dtype)


def _attention(pq, kv_sources, q_gain, k_gain, tables, batch, q_col0, k_col0, v_col0):
    t = pq.shape[0] // batch
    n_kv_heads = (k_col0 - q_col0) // (KV_GROUP * HEAD_DIM)
    tq = _pick(t, (ATTN_Q_ROWS, 256, 128, 64, 32, 16, 8))
    nq = t // tq
    gw = KV_GROUP * HEAD_DIM
    kv_rows = tuple(src.shape[0] // batch for src in kv_sources)
    rope = tables is not None
    in_specs = [pl.BlockSpec((tq, gw), lambda b, h, i: (b * nq + i, q_col0 // gw + h))]
    args = [pq]
    for src, rows in zip(kv_sources, kv_rows):
        in_specs.append(pl.BlockSpec((rows, HEAD_DIM), lambda b, h, i: (b, k_col0 // HEAD_DIM + h)))
        in_specs.append(pl.BlockSpec((rows, HEAD_DIM), lambda b, h, i: (b, v_col0 // HEAD_DIM + h)))
        args += [src, src]
    in_specs += [pl.BlockSpec((1, HEAD_DIM), lambda b, h, i: (0, 0))] * 2
    args += [q_gain.reshape(1, HEAD_DIM), k_gain.reshape(1, HEAD_DIM)]
    if rope:
        in_specs += [pl.BlockSpec((tq, HEAD_DIM), lambda b, h, i: (i, 0))] * 3
        in_specs += [pl.BlockSpec((t, HEAD_DIM), lambda b, h, i: (0, 0))] * 3
        args += list(tables) * 2
    return pl.pallas_call(
        functools.partial(_attn_kernel, kv_rows=kv_rows, rope=rope),
        grid=(batch, n_kv_heads, nq),
        in_specs=in_specs,
        out_specs=pl.BlockSpec((tq, gw), lambda b, h, i: (b * nq + i, h)),
        out_shape=jax.ShapeDtypeStruct((batch * t, n_kv_heads * gw), BF16),
        scratch_shapes=[pltpu.VMEM((sum(kv_rows), HEAD_DIM), BF16), pltpu.VMEM((sum(kv_rows), 2 * HEAD_DIM), BF16)],
        compiler_params=_params("parallel", "parallel", "arbitrary"),
        name="gqa_rope" if rope else "gqa_ctx",
    )(*args)


def _rope_tables(t):
    rows = t // GRID_W
    row_ids = np.repeat(np.arange(rows, dtype=np.float32), GRID_W)
    col_ids = np.tile(np.arange(GRID_W, dtype=np.float32), rows)
    inv_freq = jnp.asarray(ROPE_THETA, F32) ** (-jnp.arange(0, ROPE_AXIS_DIM, 2, dtype=F32) / ROPE_AXIS_DIM)
    ang_r = jnp.asarray(row_ids)[:, None] * inv_freq
    ang_c = jnp.asarray(col_ids)[:, None] * inv_freq
    zero = jnp.zeros_like(ang_r)
    cos = jnp.concatenate([jnp.cos(ang_r), jnp.cos(ang_r), jnp.cos(ang_c), jnp.cos(ang_c)], axis=1)
    s_hi = jnp.concatenate([-jnp.sin(ang_r), zero, -jnp.sin(ang_c), zero], axis=1)
    s_lo = jnp.concatenate([zero, jnp.sin(ang_r), zero, jnp.sin(ang_c)], axis=1)
    return cos, s_hi, s_lo


def _chunk_tri(rev):
    idx = np.arange(HGRN_BLOCK)
    same = (idx[:, None] // HGRN_CHUNK) == (idx[None, :] // HGRN_CHUNK)
    tri = (idx[None, :] >= idx[:, None]) if rev else (idx[None, :] <= idx[:, None])
    return jnp.asarray((same & tri).astype(np.float32), BF16)


def _chunk_cols():
    idx = np.arange(HGRN_BLOCK)
    col = np.arange((HGRN_BLOCK // HGRN_CHUNK) * HEAD_DIM)
    return jnp.asarray(((idx[:, None] // HGRN_CHUNK) == (col[None, :] // HEAD_DIM)).astype(np.float32), BF16)


def _hgrn_kernel(*refs, rev, final, n_heads):
    q_ref, z_ref, v_ref, lb_ref, tri_ref, cmask_ref, s0_ref = refs[:7]
    pos = 7
    if final:
        oprev_ref, g_ref, gain_ref = refs[7:10]
        pos = 10
    o_ref, sfin_ref, st_scr = refs[pos:pos + 3]
    n_chunks = HGRN_BLOCK // HGRN_CHUNK
    ref_row = HGRN_CHUNK - 1 - HGRN_CHUNK // 2 if rev else HGRN_CHUNK // 2
    end_row = 0 if rev else HGRN_CHUNK - 1

    @pl.when(pl.program_id(2) == 0)
    def _():
        st_scr[...] = s0_ref[0]

    gw = n_heads * HEAD_DIM
    tri = tri_ref[...]
    in_chunk = tri > 0
    z = z_ref[...]
    lb = lb_ref[...]
    sig = jax.nn.sigmoid(z)
    log_f = jnp.log(lb + (1.0 - lb) * sig)
    k = (1.0 - lb) * (1.0 - sig)
    q = jax.nn.silu(q_ref[...])
    v = v_ref[...]
    g1 = log_f.astype(BF16)
    r1 = log_f - g1.astype(F32)
    g2 = r1.astype(BF16)
    g3 = (r1 - g2.astype(F32)).astype(BF16)
    cum = (jnp.dot(tri, g1, preferred_element_type=F32) + jnp.dot(tri, g2, preferred_element_type=F32)
           + jnp.dot(tri, g3, preferred_element_type=F32))
    shape3 = (n_chunks, HGRN_CHUNK, gw)
    cum3 = cum.reshape(shape3)
    ref3 = cum3[:, ref_row:ref_row + 1, :]
    end3 = cum3[:, end_row:end_row + 1, :]
    rel = cum3 - ref3
    qa3 = q.reshape(shape3) * jnp.exp(rel)
    ka3 = k.reshape(shape3) * jnp.exp(-rel)
    qa = qa3.reshape(cum.shape).astype(BF16)
    ka = ka3.reshape(cum.shape).astype(BF16)
    qe = (qa3 * jnp.exp(ref3)).reshape(cum.shape).astype(BF16)
    kd = (ka3 * jnp.exp(end3 - ref3)).reshape(cum.shape).astype(BF16)
    dec = jnp.exp(end3)
    vb = v.astype(BF16)
    chunk_cols = cmask_ref[...] > 0

    heads = []
    for h in range(n_heads):
        cols = slice(h * HEAD_DIM, (h + 1) * HEAD_DIM)
        attn = lax.dot_general(qa[:, cols], ka[:, cols], (((1,), (1,)), ((), ())), preferred_element_type=F32)
        attn = jnp.where(in_chunk, attn, 0.0)
        o_intra = jnp.dot(attn.astype(BF16), vb[:, cols], preferred_element_type=F32)
        kd_wide = jnp.where(chunk_cols, jnp.concatenate([kd[:, cols]] * n_chunks, axis=1), jnp.zeros((), BF16))
        upd = jnp.dot(v[:, cols].T.astype(BF16), kd_wide, preferred_element_type=F32)
        heads.append(dict(o_intra=o_intra, upd=upd, state=st_scr[h]))

    order = range(n_chunks - 1, -1, -1) if rev else range(n_chunks)
    for c in order:
        rows = slice(c * HGRN_CHUNK, (c + 1) * HGRN_CHUNK)
        for h, hd in enumerate(heads):
            cols = slice(h * HEAD_DIM, (h + 1) * HEAD_DIM)
            inter = lax.dot_general(qe[rows, cols], hd["state"].astype(BF16), (((1,), (1,)), ((), ())),
                                    preferred_element_type=F32)
            o = hd["o_intra"][rows] + inter
            hd["state"] = hd["state"] * dec[c][:, cols] + hd["upd"][:, c * HEAD_DIM:(c + 1) * HEAD_DIM]
            if final:
                o = o + oprev_ref[rows, cols]
                y = o * lax.rsqrt(jnp.mean(o * o, axis=-1, keepdims=True) + EPS) * gain_ref[...]
                o = y * jax.nn.silu(g_ref[rows, cols])
            o_ref[rows, cols] = o.astype(o_ref.dtype)
    for h, hd in enumerate(heads):
        st_scr[h] = hd["state"]
        sfin_ref[0, h] = hd["state"]


def _hgrn_scan(p, lb_row, s0, batch, cols, rev, o_prev=None, out_gain=None):
    t = p.shape[0] // batch
    nsb = t // HGRN_BLOCK
    n_heads = s0.shape[1]
    hps = next(g for g in (HGRN_HEADS_PER_STEP, 2, 1)
               if n_heads % g == 0 and all(c % (g * HEAD_DIM) == 0 for c in cols))
    gw = hps * HEAD_DIM
    final = o_prev is not None
    q0, z0, v0, g0 = (c // gw for c in cols)

    def row_blk(b, s):
        return b * nsb + (nsb - 1 - s if rev else s)

    blk = (HGRN_BLOCK, gw)
    st_blk = (1, hps, HEAD_DIM, HEAD_DIM)
    in_specs = [
        pl.BlockSpec(blk, lambda b, h, s: (row_blk(b, s), q0 + h)),
        pl.BlockSpec(blk, lambda b, h, s: (row_blk(b, s), z0 + h)),
        pl.BlockSpec(blk, lambda b, h, s: (row_blk(b, s), v0 + h)),
        pl.BlockSpec((1, gw), lambda b, h, s: (0, h)),
        pl.BlockSpec((HGRN_BLOCK, HGRN_BLOCK), lambda b, h, s: (0, 0)),
        pl.BlockSpec((HGRN_BLOCK, (HGRN_BLOCK // HGRN_CHUNK) * HEAD_DIM), lambda b, h, s: (0, 0)),
        pl.BlockSpec(st_blk, lambda b, h, s: (b, h, 0, 0)),
    ]
    args = [p, p, p, lb_row, _chunk_tri(rev), _chunk_cols(), s0]
    if final:
        in_specs += [
            pl.BlockSpec(blk, lambda b, h, s: (row_blk(b, s), h)),
            pl.BlockSpec(blk, lambda b, h, s: (row_blk(b, s), g0 + h)),
            pl.BlockSpec((1, HEAD_DIM), lambda b, h, s: (0, 0)),
        ]
        args += [o_prev, p, out_gain.reshape(1, HEAD_DIM)]
    return pl.pallas_call(
        functools.partial(_hgrn_kernel, rev=rev, final=final, n_heads=hps),
        grid=(batch, n_heads // hps, nsb),
        in_specs=in_specs,
        out_specs=[
            pl.BlockSpec(blk, lambda b, h, s: (row_blk(b, s), h)),
            pl.BlockSpec(st_blk, lambda b, h, s: (b, h, 0, 0)),
        ],
        out_shape=[
            jax.ShapeDtypeStruct((batch * t, n_heads * HEAD_DIM), BF16 if final else F32),
            jax.ShapeDtypeStruct(s0.shape, F32),
        ],
        scratch_shapes=[pltpu.VMEM((hps, HEAD_DIM, HEAD_DIM), F32)],
        compiler_params=_params("parallel", "parallel", "arbitrary"),
        name="hgrn_bwd" if rev else "hgrn_fwd",
    )(*args)


def _conv_kernel(b_ref, c_ref, u_ref, w_ref, o_ref):
    u = c_ref[...] * u_ref[...]
    t = u.shape[0]
    row = lax.broadcasted_iota(jnp.int32, u.shape, 0)
    prev = jnp.where(row == 0, 0.0, pltpu.roll(u, 1, 0))
    nxt = jnp.where(row == t - 1, 0.0, pltpu.roll(u, t - 1, 0))
    y = prev * w_ref[0:1, :] + u * w_ref[1:2, :] + nxt * w_ref[2:3, :]
    o_ref[...] = (b_ref[...] * y).astype(o_ref.dtype)


def _gated_conv(p, conv_w, batch):
    d = conv_w.shape[1]
    t = p.shape[0] // batch
    td = _pick(d, (256, 128))
    nd = d // td
    return pl.pallas_call(
        _conv_kernel,
        grid=(batch, nd),
        in_specs=[
            pl.BlockSpec((t, td), lambda b, j: (b, j)),
            pl.BlockSpec((t, td), lambda b, j: (b, nd + j)),
            pl.BlockSpec((t, td), lambda b, j: (b, 2 * nd + j)),
            pl.BlockSpec((conv_w.shape[0], td), lambda b, j: (0, j)),
        ],
        out_specs=pl.BlockSpec((t, td), lambda b, j: (b, j)),
        out_shape=jax.ShapeDtypeStruct((batch * t, d), BF16),
        compiler_params=_params("parallel", "parallel"),
        name="gated_conv3",
    )(p, p, p, conv_w)


def kernel(x, c, ctx, c_ctx, mod_w, mod_b, norm_g, ffn_w1, ffn_w2, ab_w_in, ab_w_out, attn_q_gain, attn_k_gain, hgrn_lb_logits, hgrn_out_gain, conv_w_in, conv_w, conv_w_out):
    batch, seq, d = x.shape
    ctx_len = ctx.shape[1]
    depth = mod_w.shape[0]
    attn_w = d // 2
    kv_w = attn_w // KV_GROUP
    hg_w = d - attn_w
    n_hg_heads = hg_w // HEAD_DIM
    col_q, col_k, col_v = 0, attn_w, attn_w + kv_w
    col_qb = attn_w + 2 * kv_w
    col_zf, col_zb, col_ib, col_gb = (col_qb + hg_w * i for i in range(1, 5))

    lb_table = jnp.cumsum(jax.nn.softmax(hgrn_lb_logits.astype(F32), axis=1), axis=1)
    tables = _rope_tables(seq)

    pad = (-(batch + 1)) % 8
    cc = jnp.concatenate([c, c_ctx[None], jnp.zeros((pad, d), F32)], axis=0)
    mod_all = _mod_all(cc, mod_w, mod_b).reshape(depth, batch + 1 + pad, N_MOD, d)

    xl = x.reshape(batch * seq, d)
    xc = ctx.reshape(batch * ctx_len, d)
    n_ctx = batch * ctx_len

    ready = {}

    def ffn_weight(w, lead):
        if (id(w), lead) not in ready:
            ready[(id(w), lead)] = _weight_bf16(w, lead)
        return ready[(id(w), lead)]

    def ffn(xs, mod, gain, lead, rows3, gate_row, rpm, host):
        w1 = ffn_weight(ffn_w1, lead)
        h = _adaln(xs, mod, gain, rows3, rpm)
        if host:
            nxt = (lead[0], 1) if lead[1] == 0 else (lead[0] + 1, 0)
            job = (ffn_w1, nxt) if nxt[0] < depth else (ffn_w2, lead)
            a, ready[(id(job[0]), job[1])] = _ffn_up(h, w1, job)
        else:
            a = _ffn_up(h, w1)
        return _down([a], ffn_weight(ffn_w2, lead), xs, mod, gate_row, 0.5, rpm, "ffn_down")

    for layer in range(depth):
        last = layer == depth - 1
        even = layer % 2 == 0
        ctx_needed = even or not last
        gain = norm_g[layer]
        mod_l = mod_all[layer, :batch]
        mod_c = mod_all[layer, batch:batch + 1]

        if ctx_needed:
            xc = ffn(xc, mod_c, gain, (layer, 0), (0, 0, 1), 2, n_ctx, False)
        xl = ffn(xl, mod_l, gain, (layer, 0), (0, 0, 1), 2, seq, True)

        if even:
            e = layer // 2
            w_in = _weight_bf16(ab_w_in, (e,))
            w_out = _weight_bf16(ab_w_out, (e,))
            pl_ = _proj(_adaln(xl, mod_l, gain, (1, 3, 4), seq), w_in, "ab_proj")
            pc_ = _proj(_adaln(xc, mod_c, gain, (1, 3, 4), n_ctx), w_in, "ab_proj")
            o_attn = _attention(pl_, [pl_, pc_], attn_q_gain[e], attn_k_gain[e], tables, batch, col_q, col_k, col_v)
            s_zero = jnp.zeros((batch, n_hg_heads, HEAD_DIM, HEAD_DIM), F32)
            lbs = [lb_table[dd, layer].reshape(1, hg_w) for dd in range(2)]
            zcols = (col_zf, col_zb)
            sc_dir = []
            o_prev = None
            for dd in range(2):
                o_prev, s_c = _hgrn_scan(pc_, lbs[dd], s_zero, batch, (col_qb, zcols[dd], col_ib, col_gb), dd == 1,
                                         o_prev, hgrn_out_gain[e] if dd == 1 else None)
                sc_dir.append(s_c)
            o_rec_c = o_prev
            o_prev = None
            for dd in range(2):
                o_prev, _ = _hgrn_scan(pl_, lbs[dd], sc_dir[dd], batch, (col_qb, zcols[dd], col_ib, col_gb), dd == 1,
                                       o_prev, hgrn_out_gain[e] if dd == 1 else None)
            o_rec = o_prev
            xl = _down([o_attn, o_rec], w_out, xl, mod_l, 5, 1.0, seq, "mixer_out")
            if not last:
                o_attn_c = _attention(pc_, [pc_], attn_q_gain[e], attn_k_gain[e], None, batch, col_q, col_k, col_v)
                xc = _down([o_attn_c, o_rec_c], w_out, xc, mod_c, 5, 1.0, n_ctx, "mixer_out")
        else:
            o = layer // 2
            w_in = _weight_bf16(conv_w_in, (o,))
            w_out = _weight_bf16(conv_w_out, (o,))
            yl = _gated_conv(_proj(_adaln(xl, mod_l, gain, (1, 3, 4), seq), w_in, "conv_proj"), conv_w[o], batch)
            xl = _down([yl], w_out, xl, mod_l, 5, 1.0, seq, "mixer_out")
            if not last:
                yc = _gated_conv(_proj(_adaln(xc, mod_c, gain, (1, 3, 4), n_ctx), w_in, "conv_proj"), conv_w[o], batch)
                xc = _down([yc], w_out, xc, mod_c, 5, 1.0, n_ctx, "mixer_out")

        if not last:
            xc = ffn(xc, mod_c, gain, (layer, 1), (2, 6, 7), 8, n_ctx, False)
        xl = ffn(xl, mod_l, gain, (layer, 1), (2, 6, 7), 8, seq, True)

    return xl.reshape(batch, seq, d)
```

```python
import functools

import numpy as np
import jax
import jax.numpy as jnp
from jax import lax
from jax.experimental import pallas as pl
from jax.experimental.pallas import tpu as pltpu

GRID_W = 64
HEAD_DIM = 128
KV_GROUP = 4
ROPE_THETA = 10000.0
ROPE_AXIS_DIM = HEAD_DIM // 2
HGRN_CHUNK = 32
N_MOD = 9
EPS = 1e-6
LOG2_E = 1.4426950408889634

VMEM_LIMIT_BYTES = 56 * 1024 * 1024
MATMUL_VMEM_BUDGET = 46 * 1024 * 1024
MATMUL_ROWS = 1024
ATTN_Q_ROWS = 256
HGRN_BLOCK = 256
HGRN_HEADS_PER_STEP = 4
CAST_BLOCK_BYTES = 8 * 1024 * 1024
CAST_SLAB_ROWS = 16
HOST_CAST_FRACTION = 0.2

BF16 = jnp.bfloat16
F32 = jnp.float32


def _params(*sem):
    return pltpu.CompilerParams(dimension_semantics=sem, vmem_limit_bytes=VMEM_LIMIT_BYTES)


def _pick(n, candidates):
    for c in candidates:
        if n % c == 0:
            return c
    return n


def _mod_kernel(c_ref, w_ref, b_ref, o_ref):
    a = jax.nn.silu(c_ref[...]).astype(BF16)
    o_ref[0] = jnp.dot(a, w_ref[0].astype(BF16), preferred_element_type=F32) + b_ref[0]


def _mod_all(cc, mod_w, mod_b):
    depth, d, n = mod_w.shape
    rows = cc.shape[0]
    tn = _pick(n, (512, 256, 128))
    return pl.pallas_call(
        _mod_kernel,
        grid=(depth, n // tn),
        in_specs=[
            pl.BlockSpec((rows, d), lambda l, j: (0, 0)),
            pl.BlockSpec((1, d, tn), lambda l, j: (l, 0, j)),
            pl.BlockSpec((1, 1, tn), lambda l, j: (l, 0, j)),
        ],
        out_specs=pl.BlockSpec((1, rows, tn), lambda l, j: (l, 0, j)),
        out_shape=jax.ShapeDtypeStruct((depth, rows, n), F32),
        compiler_params=_params("parallel", "parallel"),
        name="mod_proj",
    )(cc, mod_w, mod_b.reshape(depth, 1, n))


def _cast_rows(src_ref, dst_ref):
    tn = dst_ref.shape[2]
    for t in range(dst_ref.shape[0]):
        dst_ref[t] = src_ref[:, t * tn:(t + 1) * tn].astype(dst_ref.dtype)


def _weight_bf16(w, lead, tn):
    r, c = w.shape[-2:]
    tr = _pick(r, tuple(t for t in (2048, 1024, 512, 256, 128, 64, 32, 16) if t * c * 4 <= CAST_BLOCK_BYTES))
    return pl.pallas_call(
        _cast_rows,
        grid=(r // tr,),
        in_specs=[pl.BlockSpec((None,) * len(lead) + (tr, c), lambda i: tuple(lead) + (i, 0))],
        out_specs=pl.BlockSpec((c // tn, tr, tn), lambda i: (0, i, 0)),
        out_shape=jax.ShapeDtypeStruct((c // tn, r, tn), BF16),
        compiler_params=_params("parallel"),
        name="weight_cast",
    )(w)


def _call_with_casts(kernel_fn, grid, in_specs, out_specs, out_shape, args, jobs, name):
    n_in, n_out, n_jobs = len(in_specs), len(out_specs), len(jobs)
    steps, nj = grid[0] * grid[1], grid[1]
    in_specs, out_specs, out_shape, args = list(in_specs), list(out_specs), list(out_shape), list(args)
    for w, lead, tn in jobs:
        r, c = w.shape[-2:]
        slab = next(t for t in range(CAST_SLAB_ROWS, r + 1, CAST_SLAB_ROWS) if r % t == 0 and r // t <= steps)

        def slab_idx(i, j, last=r // slab - 1):
            return jnp.minimum(i * nj + j, last)

        in_specs.append(pl.BlockSpec((None,) * len(lead) + (slab, c),
                                     lambda i, j, lead=tuple(lead), f=slab_idx: lead + (f(i, j), 0)))
        out_specs.append(pl.BlockSpec((c // tn, slab, tn), lambda i, j, f=slab_idx: (0, f(i, j), 0)))
        out_shape.append(jax.ShapeDtypeStruct((c // tn, r, tn), BF16))
        args.append(w)

    def body(*refs):
        ins, srcs = refs[:n_in], refs[n_in:n_in + n_jobs]
        outs, dsts = refs[n_in + n_jobs:n_in + n_jobs + n_out], refs[n_in + n_jobs + n_out:]
        for src, dst in zip(srcs, dsts):
            _cast_rows(src, dst)
        kernel_fn(*ins, *outs)

    res = pl.pallas_call(
        body,
        grid=grid,
        in_specs=in_specs,
        out_specs=out_specs,
        out_shape=out_shape,
        compiler_params=_params("parallel", "arbitrary"),
        name=name,
    )(*args)
    return res[:n_out], res[n_out:]


def _adaln_kernel(x_ref, mod_ref, gain_ref, o_ref, *, rows3):
    gain_row, shift_row, scale_row = rows3
    x = x_ref[...]
    y = x * lax.rsqrt(jnp.mean(x * x, axis=-1, keepdims=True) + EPS)
    y = y * gain_ref[gain_row:gain_row + 1, :]
    h = y * (1.0 + mod_ref[0, scale_row:scale_row + 1, :]) + mod_ref[0, shift_row:shift_row + 1, :]
    o_ref[...] = h.astype(o_ref.dtype)


def _adaln(x2d, mod, gain, rows3, rows_per_mod):
    m, d = x2d.shape
    tr = _pick(rows_per_mod, (256, 128, 64, 32, 16, 8))
    bpm = rows_per_mod // tr
    return pl.pallas_call(
        functools.partial(_adaln_kernel, rows3=rows3),
        grid=(m // tr,),
        in_specs=[
            pl.BlockSpec((tr, d), lambda i: (i, 0)),
            pl.BlockSpec((1, N_MOD, d), lambda i: (i // bpm, 0, 0)),
            pl.BlockSpec(gain.shape, lambda i: (0, 0)),
        ],
        out_specs=pl.BlockSpec((tr, d), lambda i: (i, 0)),
        out_shape=jax.ShapeDtypeStruct((m, d), BF16),
        compiler_params=_params("parallel"),
        name="adaln",
    )(x2d, mod, gain)


def _row_tile(rows):
    return _pick(rows, (MATMUL_ROWS, 512, 256, 128, 64, 32, 16, 8))


def _col_tile(tm, n, a_bytes_per_row, w_bytes_per_col, io_bytes_per_elem):
    for tn in (1024, 512, 256, 128):
        vmem = 2 * (tm * a_bytes_per_row + tn * w_bytes_per_col + tm * tn * io_bytes_per_elem)
        if n % tn == 0 and vmem <= MATMUL_VMEM_BUDGET:
            return tn
    return 128


def _up_tn(rows, d, f):
    return _col_tile(_row_tile(rows), f, 2 * d, 2 * 2 * d, 2)


def _proj_tn(rows, d, n):
    return _col_tile(_row_tile(rows), n, 2 * d, 2 * d, 4)


def _down_tn(rows, k, d):
    return _col_tile(_row_tile(rows), d, 2 * k, 2 * k, 2 * 4)


def _up_kernel(h_ref, wg_ref, wu_ref, o_ref):
    h = h_ref[...]
    g = jnp.dot(h, wg_ref[...], preferred_element_type=F32)
    u = jnp.dot(h, wu_ref[...], preferred_element_type=F32)
    o_ref[...] = (jax.nn.silu(g) * u).astype(o_ref.dtype)


def _proj_kernel(h_ref, w_ref, o_ref):
    o_ref[...] = jnp.dot(h_ref[...], w_ref[...], preferred_element_type=F32).astype(o_ref.dtype)


def _ffn_up(h, w1, jobs=()):
    m, d = h.shape
    tn = w1.shape[2]
    nj = w1.shape[0] // 2
    tm = _row_tile(m)
    (a,), casts = _call_with_casts(
        _up_kernel,
        (m // tm, nj),
        [
            pl.BlockSpec((tm, d), lambda i, j: (i, 0)),
            pl.BlockSpec((None, d, tn), lambda i, j: (j, 0, 0)),
            pl.BlockSpec((None, d, tn), lambda i, j: (j + nj, 0, 0)),
        ],
        [pl.BlockSpec((tm, tn), lambda i, j: (i, j))],
        [jax.ShapeDtypeStruct((m, nj * tn), BF16)],
        [h, w1, w1], jobs, "ffn_up")
    return a, casts


def _proj(h, w, name, jobs=()):
    m, d = h.shape
    nj, _, tn = w.shape
    tm = _row_tile(m)
    (p,), casts = _call_with_casts(
        _proj_kernel,
        (m // tm, nj),
        [
            pl.BlockSpec((tm, d), lambda i, j: (i, 0)),
            pl.BlockSpec((None, d, tn), lambda i, j: (j, 0, 0)),
        ],
        [pl.BlockSpec((tm, tn), lambda i, j: (i, j))],
        [jax.ShapeDtypeStruct((m, nj * tn), F32)],
        [h, w], jobs, name)
    return p, casts


def _down_kernel(*refs, n_a, coef):
    a_refs = refs[:n_a]
    w_refs = refs[n_a:2 * n_a]
    x_ref, gate_ref, o_ref = refs[2 * n_a:]
    acc = jnp.dot(a_refs[0][...], w_refs[0][...], preferred_element_type=F32)
    for a_ref, w_ref in zip(a_refs[1:], w_refs[1:]):
        acc = acc + jnp.dot(a_ref[...], w_ref[...], preferred_element_type=F32)
    gate = gate_ref[0, pl.program_id(1)]
    if coef != 1.0:
        gate = coef * gate
    o_ref[...] = x_ref[...] + gate * acc


def _down(a_list, w, x2d, mod, gate_row, coef, rows_per_mod, name, jobs=()):
    m, d = x2d.shape
    n_a = len(a_list)
    kk = a_list[0].shape[1]
    nj, _, tn = w.shape
    tm = _row_tile(rows_per_mod)
    bpm = rows_per_mod // tm
    gate = mod[:, gate_row].reshape(mod.shape[0], nj, 1, tn)
    in_specs = [pl.BlockSpec((tm, kk), lambda i, j: (i, 0)) for _ in a_list]
    in_specs += [pl.BlockSpec((None, kk, tn), lambda i, j, r=r: (j, r, 0)) for r in range(n_a)]
    in_specs += [
        pl.BlockSpec((tm, tn), lambda i, j: (i, j)),
        pl.BlockSpec((1, nj, 1, tn), lambda i, j: (i // bpm, 0, 0, 0)),
    ]
    (y,), casts = _call_with_casts(
        functools.partial(_down_kernel, n_a=n_a, coef=coef),
        (m // tm, nj),
        in_specs,
        [pl.BlockSpec((tm, tn), lambda i, j: (i, j))],
        [jax.ShapeDtypeStruct((m, d), F32)],
        [*a_list, *([w] * n_a), x2d, gate], jobs, name)
    return y, casts


def _head_norm(x, gain):
    return x * lax.rsqrt(jnp.mean(x * x, axis=-1, keepdims=True) + EPS) * gain


def _rope(x, c, s_hi, s_lo):
    return x * c + pltpu.roll(x, HEAD_DIM - ROPE_AXIS_DIM // 2, 1) * s_hi + pltpu.roll(x, ROPE_AXIS_DIM // 2, 1) * s_lo


def _attn_kernel(*refs, kv_rows, rope):
    n_kv = len(kv_rows)
    q_ref = refs[0]
    kv_refs = refs[1:1 + 2 * n_kv]
    qg_ref, kg_ref = refs[1 + 2 * n_kv:3 + 2 * n_kv]
    pos = 3 + 2 * n_kv
    if rope:
        cq_ref, hq_ref, lq_ref, ck_ref, hk_ref, lk_ref = refs[pos:pos + 6]
        pos += 6
    o_ref, k_scr, v_scr = refs[pos:pos + 3]

    @pl.when(pl.program_id(2) == 0)
    def _():
        off = 0
        for part, rows in enumerate(kv_rows):
            kn = _head_norm(kv_refs[2 * part][...], kg_ref[...])
            if rope and part == 0:
                kn = _rope(kn, ck_ref[...], hk_ref[...], lk_ref[...])
            k_scr[off:off + rows, :] = kn.astype(BF16)
            v_scr[off:off + rows, :HEAD_DIM] = kv_refs[2 * part + 1][...].astype(BF16)
            off += rows
        v_scr[:, HEAD_DIM:] = jnp.ones((off, HEAD_DIM), BF16)

    scale = HEAD_DIM ** -0.5 * LOG2_E
    for g in range(KV_GROUP):
        cols = slice(g * HEAD_DIM, (g + 1) * HEAD_DIM)
        q = _head_norm(q_ref[:, cols], qg_ref[...])
        if rope:
            q = _rope(q, cq_ref[...], hq_ref[...], lq_ref[...])
        q = (q * scale).astype(BF16)
        s = lax.dot_general(q, k_scr[...], (((1,), (1,)), ((), ())), preferred_element_type=F32)
        p = jnp.exp2(s - jnp.max(s, axis=-1, keepdims=True))
        o = jnp.dot(p.astype(BF16), v_scr[...], preferred_element_type=F32)
        o_ref[:, cols] = (o[:, :HEAD_DIM] / o[:, HEAD_DIM:]).astype(o_ref.dtype)


def _attention(pq, kv_sources, q_gain, k_gain, tables, batch, q_col0, k_col0, v_col0):
    t = pq.shape[0] // batch
    n_kv_heads = (k_col0 - q_col0) // (KV_GROUP * HEAD_DIM)
    tq = _pick(t, (ATTN_Q_ROWS, 256, 128, 64, 32, 16, 8))
    nq = t // tq
    gw = KV_GROUP * HEAD_DIM
    kv_rows = tuple(src.shape[0] // batch for src in kv_sources)
    rope = tables is not None
    in_specs = [pl.BlockSpec((tq, gw), lambda b, h, i: (b * nq + i, q_col0 // gw + h))]
    args = [pq]
    for src, rows in zip(kv_sources, kv_rows):
        in_specs.append(pl.BlockSpec((rows, HEAD_DIM), lambda b, h, i: (b, k_col0 // HEAD_DIM + h)))
        in_specs.append(pl.BlockSpec((rows, HEAD_DIM), lambda b, h, i: (b, v_col0 // HEAD_DIM + h)))
        args += [src, src]
    in_specs += [pl.BlockSpec((1, HEAD_DIM), lambda b, h, i: (0, 0))] * 2
    args += [q_gain.reshape(1, HEAD_DIM), k_gain.reshape(1, HEAD_DIM)]
    if rope:
        in_specs += [pl.BlockSpec((tq, HEAD_DIM), lambda b, h, i: (i, 0))] * 3
        in_specs += [pl.BlockSpec((t, HEAD_DIM), lambda b, h, i: (0, 0))] * 3
        args += list(tables) * 2
    return pl.pallas_call(
        functools.partial(_attn_kernel, kv_rows=kv_rows, rope=rope),
        grid=(batch, n_kv_heads, nq),
        in_specs=in_specs,
        out_specs=pl.BlockSpec((tq, gw), lambda b, h, i: (b * nq + i, h)),
        out_shape=jax.ShapeDtypeStruct((batch * t, n_kv_heads * gw), BF16),
        scratch_shapes=[pltpu.VMEM((sum(kv_rows), HEAD_DIM), BF16), pltpu.VMEM((sum(kv_rows), 2 * HEAD_DIM), BF16)],
        compiler_params=_params("parallel", "parallel", "arbitrary"),
        name="gqa_rope" if rope else "gqa_ctx",
    )(*args)


def _rope_tables(t):
    rows = t // GRID_W
    row_ids = np.repeat(np.arange(rows, dtype=np.float32), GRID_W)
    col_ids = np.tile(np.arange(GRID_W, dtype=np.float32), rows)
    inv_freq = jnp.asarray(ROPE_THETA, F32) ** (-jnp.arange(0, ROPE_AXIS_DIM, 2, dtype=F32) / ROPE_AXIS_DIM)
    ang_r = jnp.asarray(row_ids)[:, None] * inv_freq
    ang_c = jnp.asarray(col_ids)[:, None] * inv_freq
    zero = jnp.zeros_like(ang_r)
    cos = jnp.concatenate([jnp.cos(ang_r), jnp.cos(ang_r), jnp.cos(ang_c), jnp.cos(ang_c)], axis=1)
    s_hi = jnp.concatenate([-jnp.sin(ang_r), zero, -jnp.sin(ang_c), zero], axis=1)
    s_lo = jnp.concatenate([zero, jnp.sin(ang_r), zero, jnp.sin(ang_c)], axis=1)
    return cos, s_hi, s_lo


def _chunk_tri(rev):
    idx = np.arange(HGRN_BLOCK)
    same = (idx[:, None] // HGRN_CHUNK) == (idx[None, :] // HGRN_CHUNK)
    tri = (idx[None, :] >= idx[:, None]) if rev else (idx[None, :] <= idx[:, None])
    return jnp.asarray((same & tri).astype(np.float32), BF16)


def _chunk_cols():
    idx = np.arange(HGRN_BLOCK)
    col = np.arange((HGRN_BLOCK // HGRN_CHUNK) * HEAD_DIM)
    return jnp.asarray(((idx[:, None] // HGRN_CHUNK) == (col[None, :] // HEAD_DIM)).astype(np.float32), BF16)


def _hgrn_kernel(*refs, rev, final, n_heads):
    q_ref, z_ref, v_ref, lb_ref, tri_ref, cmask_ref, s0_ref = refs[:7]
    pos = 7
    if final:
        oprev_ref, g_ref, gain_ref = refs[7:10]
        pos = 10
    o_ref, sfin_ref, st_scr = refs[pos:pos + 3]
    n_chunks = HGRN_BLOCK // HGRN_CHUNK
    ref_row = HGRN_CHUNK - 1 - HGRN_CHUNK // 2 if rev else HGRN_CHUNK // 2
    end_row = 0 if rev else HGRN_CHUNK - 1

    @pl.when(pl.program_id(2) == 0)
    def _():
        st_scr[...] = s0_ref[0]

    gw = n_heads * HEAD_DIM
    tri = tri_ref[...]
    in_chunk = tri > 0
    z = z_ref[...]
    lb = lb_ref[...]
    sig = jax.nn.sigmoid(z)
    log_f = jnp.log(lb + (1.0 - lb) * sig)
    k = (1.0 - lb) * (1.0 - sig)
    q = jax.nn.silu(q_ref[...])
    v = v_ref[...]
    g1 = log_f.astype(BF16)
    r1 = log_f - g1.astype(F32)
    g2 = r1.astype(BF16)
    g3 = (r1 - g2.astype(F32)).astype(BF16)
    cum = (jnp.dot(tri, g1, preferred_element_type=F32) + jnp.dot(tri, g2, preferred_element_type=F32)
           + jnp.dot(tri, g3, preferred_element_type=F32))
    shape3 = (n_chunks, HGRN_CHUNK, gw)
    cum3 = cum.reshape(shape3)
    ref3 = cum3[:, ref_row:ref_row + 1, :]
    end3 = cum3[:, end_row:end_row + 1, :]
    rel = cum3 - ref3
    qa3 = q.reshape(shape3) * jnp.exp(rel)
    ka3 = k.reshape(shape3) * jnp.exp(-rel)
    qa = qa3.reshape(cum.shape).astype(BF16)
    ka = ka3.reshape(cum.shape).astype(BF16)
    qe = (qa3 * jnp.exp(ref3)).reshape(cum.shape).astype(BF16)
    kd = (ka3 * jnp.exp(end3 - ref3)).reshape(cum.shape).astype(BF16)
    dec = jnp.exp(end3)
    vb = v.astype(BF16)
    chunk_cols = cmask_ref[...] > 0

    heads = []
    for h in range(n_heads):
        cols = slice(h * HEAD_DIM, (h + 1) * HEAD_DIM)
        attn = lax.dot_general(qa[:, cols], ka[:, cols], (((1,), (1,)), ((), ())), preferred_element_type=F32)
        attn = jnp.where(in_chunk, attn, 0.0)
        o_intra = jnp.dot(attn.astype(BF16), vb[:, cols], preferred_element_type=F32)
        kd_wide = jnp.where(chunk_cols, jnp.concatenate([kd[:, cols]] * n_chunks, axis=1), jnp.zeros((), BF16))
        upd = jnp.dot(v[:, cols].T.astype(BF16), kd_wide, preferred_element_type=F32)
        heads.append(dict(o_intra=o_intra, upd=upd, state=st_scr[h]))

    order = range(n_chunks - 1, -1, -1) if rev else range(n_chunks)
    for c in order:
        rows = slice(c * HGRN_CHUNK, (c + 1) * HGRN_CHUNK)
        for h, hd in enumerate(heads):
            cols = slice(h * HEAD_DIM, (h + 1) * HEAD_DIM)
            inter = lax.dot_general(qe[rows, cols], hd["state"].astype(BF16), (((1,), (1,)), ((), ())),
                                    preferred_element_type=F32)
            o = hd["o_intra"][rows] + inter
            hd["state"] = hd["state"] * dec[c][:, cols] + hd["upd"][:, c * HEAD_DIM:(c + 1) * HEAD_DIM]
            if final:
                o = o + oprev_ref[rows, cols]
                y = o * lax.rsqrt(jnp.mean(o * o, axis=-1, keepdims=True) + EPS) * gain_ref[...]
                o = y * jax.nn.silu(g_ref[rows, cols])
            o_ref[rows, cols] = o.astype(o_ref.dtype)
    for h, hd in enumerate(heads):
        st_scr[h] = hd["state"]
        sfin_ref[0, h] = hd["state"]


def _hgrn_scan(p, lb_row, s0, batch, cols, rev, o_prev=None, out_gain=None):
    t = p.shape[0] // batch
    nsb = t // HGRN_BLOCK
    n_heads = s0.shape[1]
    hps = next(g for g in (HGRN_HEADS_PER_STEP, 2, 1)
               if n_heads % g == 0 and all(c % (g * HEAD_DIM) == 0 for c in cols))
    gw = hps * HEAD_DIM
    final = o_prev is not None
    q0, z0, v0, g0 = (c // gw for c in cols)

    def row_blk(b, s):
        return b * nsb + (nsb - 1 - s if rev else s)

    blk = (HGRN_BLOCK, gw)
    st_blk = (1, hps, HEAD_DIM, HEAD_DIM)
    in_specs = [
        pl.BlockSpec(blk, lambda b, h, s: (row_blk(b, s), q0 + h)),
        pl.BlockSpec(blk, lambda b, h, s: (row_blk(b, s), z0 + h)),
        pl.BlockSpec(blk, lambda b, h, s: (row_blk(b, s), v0 + h)),
        pl.BlockSpec((1, gw), lambda b, h, s: (0, h)),
        pl.BlockSpec((HGRN_BLOCK, HGRN_BLOCK), lambda b, h, s: (0, 0)),
        pl.BlockSpec((HGRN_BLOCK, (HGRN_BLOCK // HGRN_CHUNK) * HEAD_DIM), lambda b, h, s: (0, 0)),
        pl.BlockSpec(st_blk, lambda b, h, s: (b, h, 0, 0)),
    ]
    args = [p, p, p, lb_row, _chunk_tri(rev), _chunk_cols(), s0]
    if final:
        in_specs += [
            pl.BlockSpec(blk, lambda b, h, s: (row_blk(b, s), h)),
            pl.BlockSpec(blk, lambda b, h, s: (row_blk(b, s), g0 + h)),
            pl.BlockSpec((1, HEAD_DIM), lambda b, h, s: (0, 0)),
        ]
        args += [o_prev, p, out_gain.reshape(1, HEAD_DIM)]
    return pl.pallas_call(
        functools.partial(_hgrn_kernel, rev=rev, final=final, n_heads=hps),
        grid=(batch, n_heads // hps, nsb),
        in_specs=in_specs,
        out_specs=[
            pl.BlockSpec(blk, lambda b, h, s: (row_blk(b, s), h)),
            pl.BlockSpec(st_blk, lambda b, h, s: (b, h, 0, 0)),
        ],
        out_shape=[
            jax.ShapeDtypeStruct((batch * t, n_heads * HEAD_DIM), BF16 if final else F32),
            jax.ShapeDtypeStruct(s0.shape, F32),
        ],
        scratch_shapes=[pltpu.VMEM((hps, HEAD_DIM, HEAD_DIM), F32)],
        compiler_params=_params("parallel", "parallel", "arbitrary"),
        name="hgrn_bwd" if rev else "hgrn_fwd",
    )(*args)


def _conv_kernel(b_ref, c_ref, u_ref, w_ref, o_ref):
    u = c_ref[...] * u_ref[...]
    t = u.shape[0]
    row = lax.broadcasted_iota(jnp.int32, u.shape, 0)
    prev = jnp.where(row == 0, 0.0, pltpu.roll(u, 1, 0))
    nxt = jnp.where(row == t - 1, 0.0, pltpu.roll(u, t - 1, 0))
    y = prev * w_ref[0:1, :] + u * w_ref[1:2, :] + nxt * w_ref[2:3, :]
    o_ref[...] = (b_ref[...] * y).astype(o_ref.dtype)


def _gated_conv(p, conv_w, batch):
    d = conv_w.shape[1]
    t = p.shape[0] // batch
    td = _pick(d, (256, 128))
    nd = d // td
    return pl.pallas_call(
        _conv_kernel,
        grid=(batch, nd),
        in_specs=[
            pl.BlockSpec((t, td), lambda b, j: (b, j)),
            pl.BlockSpec((t, td), lambda b, j: (b, nd + j)),
            pl.BlockSpec((t, td), lambda b, j: (b, 2 * nd + j)),
            pl.BlockSpec((conv_w.shape[0], td), lambda b, j: (0, j)),
        ],
        out_specs=pl.BlockSpec((t, td), lambda b, j: (b, j)),
        out_shape=jax.ShapeDtypeStruct((batch * t, d), BF16),
        compiler_params=_params("parallel", "parallel"),
        name="gated_conv3",
    )(p, p, p, conv_w)


def kernel(x, c, ctx, c_ctx, mod_w, mod_b, norm_g, ffn_w1, ffn_w2, ab_w_in, ab_w_out, attn_q_gain, attn_k_gain, hgrn_lb_logits, hgrn_out_gain, conv_w_in, conv_w, conv_w_out):
    batch, seq, d = x.shape
    ctx_len = ctx.shape[1]
    depth = mod_w.shape[0]
    attn_w = d // 2
    kv_w = attn_w // KV_GROUP
    hg_w = d - attn_w
    n_hg_heads = hg_w // HEAD_DIM
    col_q, col_k, col_v = 0, attn_w, attn_w + kv_w
    col_qb = attn_w + 2 * kv_w
    col_zf, col_zb, col_ib, col_gb = (col_qb + hg_w * i for i in range(1, 5))

    lb_table = jnp.cumsum(jax.nn.softmax(hgrn_lb_logits.astype(F32), axis=1), axis=1)
    tables = _rope_tables(seq)

    pad = (-(batch + 1)) % 8
    cc = jnp.concatenate([c, c_ctx[None], jnp.zeros((pad, d), F32)], axis=0)
    mod_all = _mod_all(cc, mod_w, mod_b).reshape(depth, batch + 1 + pad, N_MOD, d)

    xl = x.reshape(batch * seq, d)
    xc = ctx.reshape(batch * ctx_len, d)
    n_ctx = batch * ctx_len

    n_lat = batch * seq
    f_ff = ffn_w2.shape[2]
    tn_up, tn_down = _up_tn(n_lat, d, f_ff), _down_tn(seq, f_ff, d)
    needs = {}
    for layer in range(depth):
        mix_in, mix_out = (ab_w_in, ab_w_out) if layer % 2 == 0 else (conv_w_in, conv_w_out)
        needs[("w1", layer, 0)] = (ffn_w1, (layer, 0), tn_up)
        needs[("w2", layer, 0)] = (ffn_w2, (layer, 0), tn_down)
        needs[("in", layer)] = (mix_in, (layer // 2,), _proj_tn(n_lat, d, mix_in.shape[2]))
        needs[("out", layer)] = (mix_out, (layer // 2,), _down_tn(seq, mix_out.shape[1], d))
        needs[("w1", layer, 1)] = (ffn_w1, (layer, 1), tn_up)
        needs[("w2", layer, 1)] = (ffn_w2, (layer, 1), tn_down)
    ready = {}

    def weight(key):
        if key not in ready:
            ready[key] = _weight_bf16(*needs[key])
        return ready[key]

    def hosted(call, w_key):
        w = weight(w_key)
        tiles = 2 if w_key[0] == "w1" else 1
        steps = (n_lat // _row_tile(seq)) * (w.shape[0] // tiles)
        budget = HOST_CAST_FRACTION * steps * tiles * w.shape[1] * w.shape[2] * 2
        keys = []
        for key, (src, _, _) in needs.items():
            size = src.shape[-2] * src.shape[-1] * 4
            if key not in ready and size <= budget:
                keys.append(key)
                budget -= size
        out, casts = call([needs[k] for k in keys])
        ready.update(zip(keys, casts))
        return out

    def ffn(xs, mod, gain, layer, idx, rows3, gate_row, rpm, latent):
        h = _adaln(xs, mod, gain, rows3, rpm)
        k1, k2 = ("w1", layer, idx), ("w2", layer, idx)
        if latent:
            a = hosted(lambda jobs: _ffn_up(h, weight(k1), jobs), k1)
            return hosted(lambda jobs: _down([a], weight(k2), xs, mod, gate_row, 0.5, rpm, "ffn_down", jobs), k2)
        a, _ = _ffn_up(h, weight(k1))
        return _down([a], weight(k2), xs, mod, gate_row, 0.5, rpm, "ffn_down")[0]

    for layer in range(depth):
        last = layer == depth - 1
        even = layer % 2 == 0
        ctx_needed = even or not last
        gain = norm_g[layer]
        mod_l = mod_all[layer, :batch]
        mod_c = mod_all[layer, batch:batch + 1]

        xl = ffn(xl, mod_l, gain, layer, 0, (0, 0, 1), 2, seq, True)
        if ctx_needed:
            xc = ffn(xc, mod_c, gain, layer, 0, (0, 0, 1), 2, n_ctx, False)

        k_in, k_out = ("in", layer), ("out", layer)
        hl = _adaln(xl, mod_l, gain, (1, 3, 4), seq)
        if even:
            e = layer // 2
            pl_ = hosted(lambda jobs: _proj(hl, weight(k_in), "ab_proj", jobs), k_in)
            pc_, _ = _proj(_adaln(xc, mod_c, gain, (1, 3, 4), n_ctx), weight(k_in), "ab_proj")
            o_attn = _attention(pl_, [pl_, pc_], attn_q_gain[e], attn_k_gain[e], tables, batch, col_q, col_k, col_v)
            s_zero = jnp.zeros((batch, n_hg_heads, HEAD_DIM, HEAD_DIM), F32)
            lbs = [lb_table[dd, layer].reshape(1, hg_w) for dd in range(2)]
            zcols = (col_zf, col_zb)
            sc_dir = []
            o_prev = None
            for dd in range(2):
                o_prev, s_c = _hgrn_scan(pc_, lbs[dd], s_zero, batch, (col_qb, zcols[dd], col_ib, col_gb), dd == 1,
                                         o_prev, hgrn_out_gain[e] if dd == 1 else None)
                sc_dir.append(s_c)
            o_rec_c = o_prev
            o_prev = None
            for dd in range(2):
                o_prev, _ = _hgrn_scan(pl_, lbs[dd], sc_dir[dd], batch, (col_qb, zcols[dd], col_ib, col_gb), dd == 1,
                                       o_prev, hgrn_out_gain[e] if dd == 1 else None)
            mixed = [o_attn, o_prev]
            if not last:
                o_attn_c = _attention(pc_, [pc_], attn_q_gain[e], attn_k_gain[e], None, batch, col_q, col_k, col_v)
                mixed_c = [o_attn_c, o_rec_c]
        else:
            o = layer // 2
            pl_ = hosted(lambda jobs: _proj(hl, weight(k_in), "conv_proj", jobs), k_in)
            mixed = [_gated_conv(pl_, conv_w[o], batch)]
            if not last:
                pc_, _ = _proj(_adaln(xc, mod_c, gain, (1, 3, 4), n_ctx), weight(k_in), "conv_proj")
                mixed_c = [_gated_conv(pc_, conv_w[o], batch)]
        x_res = xl
        xl = hosted(lambda jobs: _down(mixed, weight(k_out), x_res, mod_l, 5, 1.0, seq, "mixer_out", jobs), k_out)
        if not last:
            xc = _down(mixed_c, weight(k_out), xc, mod_c, 5, 1.0, n_ctx, "mixer_out")[0]

        xl = ffn(xl, mod_l, gain, layer, 1, (2, 6, 7), 8, seq, True)
        if not last:
            xc = ffn(xc, mod_c, gain, layer, 1, (2, 6, 7), 8, n_ctx, False)

    return xl.reshape(batch, seq, d)
```

```python
import functools

import numpy as np
import jax
import jax.numpy as jnp
from jax import lax
from jax.experimental import pallas as pl
from jax.experimental.pallas import tpu as pltpu

GRID_W = 64
HEAD_DIM = 128
KV_GROUP = 4
ROPE_THETA = 10000.0
ROPE_AXIS_DIM = HEAD_DIM // 2
HGRN_CHUNK = 32
N_MOD = 9
EPS = 1e-6
LOG2_E = 1.4426950408889634

VMEM_LIMIT_BYTES = 56 * 1024 * 1024
MATMUL_VMEM_BUDGET = 46 * 1024 * 1024
MATMUL_ROWS = 1024
MXU_COLS = 256
ATTN_Q_ROWS = 256
HGRN_BLOCK = 256
HGRN_HEADS_PER_STEP = 4
CAST_BLOCK_BYTES = 8 * 1024 * 1024
CAST_SLAB_ROWS = 16
HOST_CAST_FRACTION = 0.2

BF16 = jnp.bfloat16
F32 = jnp.float32


def _params(*sem):
    return pltpu.CompilerParams(dimension_semantics=sem, vmem_limit_bytes=VMEM_LIMIT_BYTES)


def _pick(n, candidates):
    for c in candidates:
        if n % c == 0:
            return c
    return n


def _mod_kernel(c_ref, w_ref, b_ref, o_ref):
    a = jax.nn.silu(c_ref[...]).astype(BF16)
    o_ref[0] = jnp.dot(a, w_ref[0].astype(BF16), preferred_element_type=F32) + b_ref[0]


def _mod_all(cc, mod_w, mod_b):
    depth, d, n = mod_w.shape
    rows = cc.shape[0]
    tn = _pick(n, (512, 256, 128))
    return pl.pallas_call(
        _mod_kernel,
        grid=(depth, n // tn),
        in_specs=[
            pl.BlockSpec((rows, d), lambda l, j: (0, 0)),
            pl.BlockSpec((1, d, tn), lambda l, j: (l, 0, j)),
            pl.BlockSpec((1, 1, tn), lambda l, j: (l, 0, j)),
        ],
        out_specs=pl.BlockSpec((1, rows, tn), lambda l, j: (l, 0, j)),
        out_shape=jax.ShapeDtypeStruct((depth, rows, n), F32),
        compiler_params=_params("parallel", "parallel"),
        name="mod_proj",
    )(cc, mod_w, mod_b.reshape(depth, 1, n))


def _cast_rows(src_ref, dst_ref):
    tn = dst_ref.shape[2]
    for t in range(dst_ref.shape[0]):
        dst_ref[t] = src_ref[:, t * tn:(t + 1) * tn].astype(dst_ref.dtype)


def _weight_bf16(w, lead, tn):
    r, c = w.shape[-2:]
    tr = _pick(r, tuple(t for t in (2048, 1024, 512, 256, 128, 64, 32, 16) if t * c * 4 <= CAST_BLOCK_BYTES))
    return pl.pallas_call(
        _cast_rows,
        grid=(r // tr,),
        in_specs=[pl.BlockSpec((None,) * len(lead) + (tr, c), lambda i: tuple(lead) + (i, 0))],
        out_specs=pl.BlockSpec((c // tn, tr, tn), lambda i: (0, i, 0)),
        out_shape=jax.ShapeDtypeStruct((c // tn, r, tn), BF16),
        compiler_params=_params("parallel"),
        name="weight_cast",
    )(w)


def _call_with_casts(kernel_fn, grid, in_specs, out_specs, out_shape, args, jobs, name):
    n_in, n_out, n_jobs = len(in_specs), len(out_specs), len(jobs)
    steps, nj = grid[0] * grid[1], grid[1]
    in_specs, out_specs, out_shape, args = list(in_specs), list(out_specs), list(out_shape), list(args)
    for w, lead, tn in jobs:
        r, c = w.shape[-2:]
        slab = next(t for t in range(CAST_SLAB_ROWS, r + 1, CAST_SLAB_ROWS) if r % t == 0 and r // t <= steps)

        def slab_idx(i, j, last=r // slab - 1):
            return jnp.minimum(i * nj + j, last)

        in_specs.append(pl.BlockSpec((None,) * len(lead) + (slab, c),
                                     lambda i, j, lead=tuple(lead), f=slab_idx: lead + (f(i, j), 0)))
        out_specs.append(pl.BlockSpec((c // tn, slab, tn), lambda i, j, f=slab_idx: (0, f(i, j), 0)))
        out_shape.append(jax.ShapeDtypeStruct((c // tn, r, tn), BF16))
        args.append(w)

    def body(*refs):
        ins, srcs = refs[:n_in], refs[n_in:n_in + n_jobs]
        outs, dsts = refs[n_in + n_jobs:n_in + n_jobs + n_out], refs[n_in + n_jobs + n_out:]
        for src, dst in zip(srcs, dsts):
            _cast_rows(src, dst)
        kernel_fn(*ins, *outs)

    res = pl.pallas_call(
        body,
        grid=grid,
        in_specs=in_specs,
        out_specs=out_specs,
        out_shape=out_shape,
        compiler_params=_params("arbitrary" if jobs else "parallel", "arbitrary"),
        name=name,
    )(*args)
    return res[:n_out], res[n_out:]


def _adaln_kernel(x_ref, mod_ref, gain_ref, o_ref, *, rows3):
    gain_row, shift_row, scale_row = rows3
    gs = gain_ref[gain_row:gain_row + 1, :] * (1.0 + mod_ref[0, scale_row:scale_row + 1, :])
    shift = mod_ref[0, shift_row:shift_row + 1, :]

    def body(r, carry):
        sl = pl.ds(pl.multiple_of(r * 8, 8), 8)
        x = x_ref[sl, :]
        y = x * lax.rsqrt(jnp.mean(x * x, axis=-1, keepdims=True) + EPS)
        o_ref[sl, :] = (y * gs + shift).astype(o_ref.dtype)
        return carry

    lax.fori_loop(0, x_ref.shape[0] // 8, body, 0, unroll=8)


def _adaln(x2d, mod, gain, rows3, rows_per_mod):
    m, d = x2d.shape
    tr = _pick(rows_per_mod, (512, 256, 128, 64, 32, 16, 8))
    bpm = rows_per_mod // tr
    return pl.pallas_call(
        functools.partial(_adaln_kernel, rows3=rows3),
        grid=(m // tr,),
        in_specs=[
            pl.BlockSpec((tr, d), lambda i: (i, 0)),
            pl.BlockSpec((1, N_MOD, d), lambda i: (i // bpm, 0, 0)),
            pl.BlockSpec(gain.shape, lambda i: (0, 0)),
        ],
        out_specs=pl.BlockSpec((tr, d), lambda i: (i, 0)),
        out_shape=jax.ShapeDtypeStruct((m, d), BF16),
        compiler_params=_params("parallel"),
        name="adaln",
    )(x2d, mod, gain)


def _row_tile(rows):
    return _pick(rows, (MATMUL_ROWS, 512, 256, 128, 64, 32, 16, 8))


def _col_tile(tm, n, a_bytes_per_row, w_bytes_per_col, io_bytes_per_elem):
    for tn in (1024, 512, 256, 128):
        vmem = 2 * (tm * a_bytes_per_row + tn * w_bytes_per_col + tm * tn * io_bytes_per_elem)
        if n % tn == 0 and vmem <= MATMUL_VMEM_BUDGET:
            return tn
    return 128


def _up_tn(rows, d, f):
    return _col_tile(_row_tile(rows), f, 2 * d, 2 * 2 * d, 2)


def _proj_tn(rows, d, n):
    return _col_tile(_row_tile(rows), n, 2 * d, 2 * d, 4)


def _down_tn(rows, k, d):
    return _col_tile(_row_tile(rows), d, 2 * k, 2 * k, 2 * 4)


def _up_kernel(h_ref, wg_ref, wu_ref, o_ref):
    h = h_ref[...]
    tn = o_ref.shape[1]
    part = MXU_COLS if tn % MXU_COLS == 0 else tn
    for c in range(0, tn, part):
        g = jnp.dot(h, wg_ref[:, c:c + part], preferred_element_type=F32)
        u = jnp.dot(h, wu_ref[:, c:c + part], preferred_element_type=F32)
        o_ref[:, c:c + part] = (jax.nn.silu(g) * u).astype(o_ref.dtype)


def _proj_kernel(h_ref, w_ref, o_ref):
    o_ref[...] = jnp.dot(h_ref[...], w_ref[...], preferred_element_type=F32).astype(o_ref.dtype)


def _ffn_up(h, w1, jobs=()):
    m, d = h.shape
    tn = w1.shape[2]
    nj = w1.shape[0] // 2
    tm = _row_tile(m)
    (a,), casts = _call_with_casts(
        _up_kernel,
        (m // tm, nj),
        [
            pl.BlockSpec((tm, d), lambda i, j: (i, 0)),
            pl.BlockSpec((None, d, tn), lambda i, j: (j, 0, 0)),
            pl.BlockSpec((None, d, tn), lambda i, j: (j + nj, 0, 0)),
        ],
        [pl.BlockSpec((tm, tn), lambda i, j: (i, j))],
        [jax.ShapeDtypeStruct((m, nj * tn), BF16)],
        [h, w1, w1], jobs, "ffn_up")
    return a, casts


def _proj(h, w, name, jobs=()):
    m, d = h.shape
    nj, _, tn = w.shape
    tm = _row_tile(m)
    (p,), casts = _call_with_casts(
        _proj_kernel,
        (m // tm, nj),
        [
            pl.BlockSpec((tm, d), lambda i, j: (i, 0)),
            pl.BlockSpec((None, d, tn), lambda i, j: (j, 0, 0)),
        ],
        [pl.BlockSpec((tm, tn), lambda i, j: (i, j))],
        [jax.ShapeDtypeStruct((m, nj * tn), F32)],
        [h, w], jobs, name)
    return p, casts


def _down_kernel(*refs, n_a, coef):
    a_refs = refs[:n_a]
    w_refs = refs[n_a:2 * n_a]
    x_ref, gate_ref, o_ref = refs[2 * n_a:]
    acc = jnp.dot(a_refs[0][...], w_refs[0][...], preferred_element_type=F32)
    for a_ref, w_ref in zip(a_refs[1:], w_refs[1:]):
        acc = acc + jnp.dot(a_ref[...], w_ref[...], preferred_element_type=F32)
    gate = gate_ref[0, pl.program_id(1)]
    if coef != 1.0:
        gate = coef * gate
    o_ref[...] = x_ref[...] + gate * acc


def _down(a_list, w, x2d, mod, gate_row, coef, rows_per_mod, name, jobs=()):
    m, d = x2d.shape
    n_a = len(a_list)
    kk = a_list[0].shape[1]
    nj, _, tn = w.shape
    tm = _row_tile(rows_per_mod)
    bpm = rows_per_mod // tm
    gate = mod[:, gate_row].reshape(mod.shape[0], nj, 1, tn)
    in_specs = [pl.BlockSpec((tm, kk), lambda i, j: (i, 0)) for _ in a_list]
    in_specs += [pl.BlockSpec((None, kk, tn), lambda i, j, r=r: (j, r, 0)) for r in range(n_a)]
    in_specs += [
        pl.BlockSpec((tm, tn), lambda i, j: (i, j)),
        pl.BlockSpec((1, nj, 1, tn), lambda i, j: (i // bpm, 0, 0, 0)),
    ]
    (y,), casts = _call_with_casts(
        functools.partial(_down_kernel, n_a=n_a, coef=coef),
        (m // tm, nj),
        in_specs,
        [pl.BlockSpec((tm, tn), lambda i, j: (i, j))],
        [jax.ShapeDtypeStruct((m, d), F32)],
        [*a_list, *([w] * n_a), x2d, gate], jobs, name)
    return y, casts


def _head_norm(x, gain):
    return x * lax.rsqrt(jnp.mean(x * x, axis=-1, keepdims=True) + EPS) * gain


def _rope(x, c, s_hi, s_lo):
    return x * c + pltpu.roll(x, HEAD_DIM - ROPE_AXIS_DIM // 2, 1) * s_hi + pltpu.roll(x, ROPE_AXIS_DIM // 2, 1) * s_lo


def _attn_kernel(*refs, kv_rows, rope):
    n_kv = len(kv_rows)
    q_ref = refs[0]
    kv_refs = refs[1:1 + 2 * n_kv]
    qg_ref, kg_ref = refs[1 + 2 * n_kv:3 + 2 * n_kv]
    pos = 3 + 2 * n_kv
    if rope:
        cq_ref, hq_ref, lq_ref, ck_ref, hk_ref, lk_ref = refs[pos:pos + 6]
        pos += 6
    o_ref, k_scr, v_scr = refs[pos:pos + 3]

    @pl.when(pl.program_id(2) == 0)
    def _():
        off = 0
        for part, rows in enumerate(kv_rows):
            kn = _head_norm(kv_refs[2 * part][...], kg_ref[...])
            if rope and part == 0:
                kn = _rope(kn, ck_ref[...], hk_ref[...], lk_ref[...])
            k_scr[off:off + rows, :] = kn.astype(BF16)
            v_scr[off:off + rows, :HEAD_DIM] = kv_refs[2 * part + 1][...].astype(BF16)
            off += rows
        v_scr[:, HEAD_DIM:] = jnp.ones((off, HEAD_DIM), BF16)

    scale = HEAD_DIM ** -0.5 * LOG2_E
    for g in range(KV_GROUP):
        cols = slice(g * HEAD_DIM, (g + 1) * HEAD_DIM)
        q = _head_norm(q_ref[:, cols], qg_ref[...])
        if rope:
            q = _rope(q, cq_ref[...], hq_ref[...], lq_ref[...])
        q = (q * scale).astype(BF16)
        s = lax.dot_general(q, k_scr[...], (((1,), (1,)), ((), ())), preferred_element_type=F32)
        p = jnp.exp2(s - jnp.max(s, axis=-1, keepdims=True))
        o = jnp.dot(p.astype(BF16), v_scr[...], preferred_element_type=F32)
        o_ref[:, cols] = (o[:, :HEAD_DIM] / o[:, HEAD_DIM:]).astype(o_ref.dtype)


def _attention(pq, kv_sources, q_gain, k_gain, tables, batch, q_col0, k_col0, v_col0):
    t = pq.shape[0] // batch
    n_kv_heads = (k_col0 - q_col0) // (KV_GROUP * HEAD_DIM)
    tq = _pick(t, (ATTN_Q_ROWS, 256, 128, 64, 32, 16, 8))
    nq = t // tq
    gw = KV_GROUP * HEAD_DIM
    kv_rows = tuple(src.shape[0] // batch for src in kv_sources)
    rope = tables is not None
    in_specs = [pl.BlockSpec((tq, gw), lambda b, h, i: (b * nq + i, q_col0 // gw + h))]
    args = [pq]
    for src, rows in zip(kv_sources, kv_rows):
        in_specs.append(pl.BlockSpec((rows, HEAD_DIM), lambda b, h, i: (b, k_col0 // HEAD_DIM + h)))
        in_specs.append(pl.BlockSpec((rows, HEAD_DIM), lambda b, h, i: (b, v_col0 // HEAD_DIM + h)))
        args += [src, src]
    in_specs += [pl.BlockSpec((1, HEAD_DIM), lambda b, h, i: (0, 0))] * 2
    args += [q_gain.reshape(1, HEAD_DIM), k_gain.reshape(1, HEAD_DIM)]
    if rope:
        in_specs += [pl.BlockSpec((tq, HEAD_DIM), lambda b, h, i: (i, 0))] * 3
        in_specs += [pl.BlockSpec((t, HEAD_DIM), lambda b, h, i: (0, 0))] * 3
        args += list(tables) * 2
    return pl.pallas_call(
        functools.partial(_attn_kernel, kv_rows=kv_rows, rope=rope),
        grid=(batch, n_kv_heads, nq),
        in_specs=in_specs,
        out_specs=pl.BlockSpec((tq, gw), lambda b, h, i: (b * nq + i, h)),
        out_shape=jax.ShapeDtypeStruct((batch * t, n_kv_heads * gw), BF16),
        scratch_shapes=[pltpu.VMEM((sum(kv_rows), HEAD_DIM), BF16), pltpu.VMEM((sum(kv_rows), 2 * HEAD_DIM), BF16)],
        compiler_params=_params("parallel", "parallel", "arbitrary"),
        name="gqa_rope" if rope else "gqa_ctx",
    )(*args)


def _rope_tables(t):
    rows = t // GRID_W
    row_ids = np.repeat(np.arange(rows, dtype=np.float32), GRID_W)
    col_ids = np.tile(np.arange(GRID_W, dtype=np.float32), rows)
    inv_freq = jnp.asarray(ROPE_THETA, F32) ** (-jnp.arange(0, ROPE_AXIS_DIM, 2, dtype=F32) / ROPE_AXIS_DIM)
    ang_r = jnp.asarray(row_ids)[:, None] * inv_freq
    ang_c = jnp.asarray(col_ids)[:, None] * inv_freq
    zero = jnp.zeros_like(ang_r)
    cos = jnp.concatenate([jnp.cos(ang_r), jnp.cos(ang_r), jnp.cos(ang_c), jnp.cos(ang_c)], axis=1)
    s_hi = jnp.concatenate([-jnp.sin(ang_r), zero, -jnp.sin(ang_c), zero], axis=1)
    s_lo = jnp.concatenate([zero, jnp.sin(ang_r), zero, jnp.sin(ang_c)], axis=1)
    return cos, s_hi, s_lo


def _chunk_tri(rev):
    idx = np.arange(HGRN_BLOCK)
    same = (idx[:, None] // HGRN_CHUNK) == (idx[None, :] // HGRN_CHUNK)
    tri = (idx[None, :] >= idx[:, None]) if rev else (idx[None, :] <= idx[:, None])
    return jnp.asarray((same & tri).astype(np.float32), BF16)


def _chunk_cols():
    idx = np.arange(HGRN_BLOCK)
    col = np.arange((HGRN_BLOCK // HGRN_CHUNK) * HEAD_DIM)
    return jnp.asarray(((idx[:, None] // HGRN_CHUNK) == (col[None, :] // HEAD_DIM)).astype(np.float32), BF16)


def _hgrn_kernel(*refs, rev, final, n_heads):
    q_ref, z_ref, v_ref, lb_ref, tri_ref, cmask_ref, s0_ref = refs[:7]
    pos = 7
    if final:
        oprev_ref, g_ref, gain_ref = refs[7:10]
        pos = 10
    o_ref, sfin_ref, st_scr = refs[pos:pos + 3]
    n_chunks = HGRN_BLOCK // HGRN_CHUNK
    ref_row = HGRN_CHUNK - 1 - HGRN_CHUNK // 2 if rev else HGRN_CHUNK // 2
    end_row = 0 if rev else HGRN_CHUNK - 1

    @pl.when(pl.program_id(2) == 0)
    def _():
        st_scr[...] = s0_ref[0]

    gw = n_heads * HEAD_DIM
    tri = tri_ref[...]
    in_chunk = tri > 0
    z = z_ref[...]
    lb = lb_ref[...]
    sig = jax.nn.sigmoid(z)
    log_f = jnp.log(lb + (1.0 - lb) * sig)
    k = (1.0 - lb) * (1.0 - sig)
    q = jax.nn.silu(q_ref[...])
    v = v_ref[...]
    g1 = log_f.astype(BF16)
    r1 = log_f - g1.astype(F32)
    g2 = r1.astype(BF16)
    g3 = (r1 - g2.astype(F32)).astype(BF16)
    cum = (jnp.dot(tri, g1, preferred_element_type=F32) + jnp.dot(tri, g2, preferred_element_type=F32)
           + jnp.dot(tri, g3, preferred_element_type=F32))
    shape3 = (n_chunks, HGRN_CHUNK, gw)
    cum3 = cum.reshape(shape3)
    ref3 = cum3[:, ref_row:ref_row + 1, :]
    end3 = cum3[:, end_row:end_row + 1, :]
    rel = cum3 - ref3
    qa3 = q.reshape(shape3) * jnp.exp(rel)
    ka3 = k.reshape(shape3) * jnp.exp(-rel)
    qa = qa3.reshape(cum.shape).astype(BF16)
    ka = ka3.reshape(cum.shape).astype(BF16)
    qe = (qa3 * jnp.exp(ref3)).reshape(cum.shape).astype(BF16)
    kd = (ka3 * jnp.exp(end3 - ref3)).reshape(cum.shape).astype(BF16)
    dec = jnp.exp(end3)
    vb = v.astype(BF16)
    chunk_cols = cmask_ref[...]

    heads = []
    for h in range(n_heads):
        cols = slice(h * HEAD_DIM, (h + 1) * HEAD_DIM)
        attn = lax.dot_general(qa[:, cols], ka[:, cols], (((1,), (1,)), ((), ())), preferred_element_type=F32)
        attn = jnp.where(in_chunk, attn, 0.0)
        o_intra = jnp.dot(attn.astype(BF16), vb[:, cols], preferred_element_type=F32)
        kd_wide = jnp.concatenate([kd[:, cols]] * n_chunks, axis=1) * chunk_cols
        upd = jnp.dot(v[:, cols].T.astype(BF16), kd_wide, preferred_element_type=F32)
        heads.append(dict(o_intra=o_intra, upd=upd, state=st_scr[h]))

    order = range(n_chunks - 1, -1, -1) if rev else range(n_chunks)
    for c in order:
        rows = slice(c * HGRN_CHUNK, (c + 1) * HGRN_CHUNK)
        for h, hd in enumerate(heads):
            cols = slice(h * HEAD_DIM, (h + 1) * HEAD_DIM)
            inter = lax.dot_general(qe[rows, cols], hd["state"].astype(BF16), (((1,), (1,)), ((), ())),
                                    preferred_element_type=F32)
            o = hd["o_intra"][rows] + inter
            hd["state"] = hd["state"] * dec[c][:, cols] + hd["upd"][:, c * HEAD_DIM:(c + 1) * HEAD_DIM]
            if final:
                o = o + oprev_ref[rows, cols]
                y = o * lax.rsqrt(jnp.mean(o * o, axis=-1, keepdims=True) + EPS) * gain_ref[...]
                o = y * jax.nn.silu(g_ref[rows, cols])
            o_ref[rows, cols] = o.astype(o_ref.dtype)
    for h, hd in enumerate(heads):
        st_scr[h] = hd["state"]
        sfin_ref[0, h] = hd["state"]


def _hgrn_scan(p, lb_row, s0, batch, cols, rev, o_prev=None, out_gain=None):
    t = p.shape[0] // batch
    nsb = t // HGRN_BLOCK
    n_heads = s0.shape[1]
    hps = next(g for g in (HGRN_HEADS_PER_STEP, 2, 1)
               if n_heads % g == 0 and all(c % (g * HEAD_DIM) == 0 for c in cols))
    gw = hps * HEAD_DIM
    final = o_prev is not None
    q0, z0, v0, g0 = (c // gw for c in cols)

    def row_blk(b, s):
        return b * nsb + (nsb - 1 - s if rev else s)

    blk = (HGRN_BLOCK, gw)
    st_blk = (1, hps, HEAD_DIM, HEAD_DIM)
    in_specs = [
        pl.BlockSpec(blk, lambda b, h, s: (row_blk(b, s), q0 + h)),
        pl.BlockSpec(blk, lambda b, h, s: (row_blk(b, s), z0 + h)),
        pl.BlockSpec(blk, lambda b, h, s: (row_blk(b, s), v0 + h)),
        pl.BlockSpec((1, gw), lambda b, h, s: (0, h)),
        pl.BlockSpec((HGRN_BLOCK, HGRN_BLOCK), lambda b, h, s: (0, 0)),
        pl.BlockSpec((HGRN_BLOCK, (HGRN_BLOCK // HGRN_CHUNK) * HEAD_DIM), lambda b, h, s: (0, 0)),
        pl.BlockSpec(st_blk, lambda b, h, s: (b, h, 0, 0)),
    ]
    args = [p, p, p, lb_row, _chunk_tri(rev), _chunk_cols(), s0]
    if final:
        in_specs += [
            pl.BlockSpec(blk, lambda b, h, s: (row_blk(b, s), h)),
            pl.BlockSpec(blk, lambda b, h, s: (row_blk(b, s), g0 + h)),
            pl.BlockSpec((1, HEAD_DIM), lambda b, h, s: (0, 0)),
        ]
        args += [o_prev, p, out_gain.reshape(1, HEAD_DIM)]
    return pl.pallas_call(
        functools.partial(_hgrn_kernel, rev=rev, final=final, n_heads=hps),
        grid=(batch, n_heads // hps, nsb),
        in_specs=in_specs,
        out_specs=[
            pl.BlockSpec(blk, lambda b, h, s: (row_blk(b, s), h)),
            pl.BlockSpec(st_blk, lambda b, h, s: (b, h, 0, 0)),
        ],
        out_shape=[
            jax.ShapeDtypeStruct((batch * t, n_heads * HEAD_DIM), BF16 if final else F32),
            jax.ShapeDtypeStruct(s0.shape, F32),
        ],
        scratch_shapes=[pltpu.VMEM((hps, HEAD_DIM, HEAD_DIM), F32)],
        compiler_params=_params("parallel", "parallel", "arbitrary"),
        name="hgrn_bwd" if rev else "hgrn_fwd",
    )(*args)


def _conv_kernel(b_ref, c_ref, u_ref, w_ref, o_ref):
    u = c_ref[...] * u_ref[...]
    t = u.shape[0]
    row = lax.broadcasted_iota(jnp.int32, u.shape, 0)
    prev = jnp.where(row == 0, 0.0, pltpu.roll(u, 1, 0))
    nxt = jnp.where(row == t - 1, 0.0, pltpu.roll(u, t - 1, 0))
    y = prev * w_ref[0:1, :] + u * w_ref[1:2, :] + nxt * w_ref[2:3, :]
    o_ref[...] = (b_ref[...] * y).astype(o_ref.dtype)


def _gated_conv(p, conv_w, batch):
    d = conv_w.shape[1]
    t = p.shape[0] // batch
    td = _pick(d, (256, 128))
    nd = d // td
    return pl.pallas_call(
        _conv_kernel,
        grid=(batch, nd),
        in_specs=[
            pl.BlockSpec((t, td), lambda b, j: (b, j)),
            pl.BlockSpec((t, td), lambda b, j: (b, nd + j)),
            pl.BlockSpec((t, td), lambda b, j: (b, 2 * nd + j)),
            pl.BlockSpec((conv_w.shape[0], td), lambda b, j: (0, j)),
        ],
        out_specs=pl.BlockSpec((t, td), lambda b, j: (b, j)),
        out_shape=jax.ShapeDtypeStruct((batch * t, d), BF16),
        compiler_params=_params("parallel", "parallel"),
        name="gated_conv3",
    )(p, p, p, conv_w)


def kernel(x, c, ctx, c_ctx, mod_w, mod_b, norm_g, ffn_w1, ffn_w2, ab_w_in, ab_w_out, attn_q_gain, attn_k_gain, hgrn_lb_logits, hgrn_out_gain, conv_w_in, conv_w, conv_w_out):
    batch, seq, d = x.shape
    ctx_len = ctx.shape[1]
    depth = mod_w.shape[0]
    attn_w = d // 2
    kv_w = attn_w // KV_GROUP
    hg_w = d - attn_w
    n_hg_heads = hg_w // HEAD_DIM
    col_q, col_k, col_v = 0, attn_w, attn_w + kv_w
    col_qb = attn_w + 2 * kv_w
    col_zf, col_zb, col_ib, col_gb = (col_qb + hg_w * i for i in range(1, 5))

    lb_table = jnp.cumsum(jax.nn.softmax(hgrn_lb_logits.astype(F32), axis=1), axis=1)
    tables = _rope_tables(seq)

    pad = (-(batch + 1)) % 8
    cc = jnp.concatenate([c, c_ctx[None], jnp.zeros((pad, d), F32)], axis=0)
    mod_all = _mod_all(cc, mod_w, mod_b).reshape(depth, batch + 1 + pad, N_MOD, d)

    xl = x.reshape(batch * seq, d)
    xc = ctx.reshape(batch * ctx_len, d)
    n_ctx = batch * ctx_len

    n_lat = batch * seq
    f_ff = ffn_w2.shape[2]
    tn_up, tn_down = _up_tn(n_lat, d, f_ff), _down_tn(seq, f_ff, d)
    needs = {}
    for layer in range(depth):
        mix_in, mix_out = (ab_w_in, ab_w_out) if layer % 2 == 0 else (conv_w_in, conv_w_out)
        needs[("w1", layer, 0)] = (ffn_w1, (layer, 0), tn_up)
        needs[("w2", layer, 0)] = (ffn_w2, (layer, 0), tn_down)
        needs[("in", layer)] = (mix_in, (layer // 2,), _proj_tn(n_lat, d, mix_in.shape[2]))
        needs[("out", layer)] = (mix_out, (layer // 2,), _down_tn(seq, mix_out.shape[1], d))
        needs[("w1", layer, 1)] = (ffn_w1, (layer, 1), tn_up)
        needs[("w2", layer, 1)] = (ffn_w2, (layer, 1), tn_down)
    ready = {}

    def weight(key):
        if key not in ready:
            ready[key] = _weight_bf16(*needs[key])
        return ready[key]

    def hosted(call, w_key, tiles=1):
        w = weight(w_key)
        steps = (n_lat // _row_tile(seq)) * (w.shape[0] // tiles)
        budget = HOST_CAST_FRACTION * steps * tiles * w.shape[1] * w.shape[2] * 2
        keys = []
        for key, (src, _, _) in needs.items():
            size = src.shape[-2] * src.shape[-1] * 4
            if key not in ready and size <= budget:
                keys.append(key)
                budget -= size
        out, casts = call([needs[k] for k in keys])
        ready.update(zip(keys, casts))
        return out

    def ffn(xs, mod, gain, layer, idx, rows3, gate_row, rpm, latent):
        h = _adaln(xs, mod, gain, rows3, rpm)
        k1, k2 = ("w1", layer, idx), ("w2", layer, idx)
        if latent:
            a = hosted(lambda jobs: _ffn_up(h, weight(k1), jobs), k1, tiles=2)
            return hosted(lambda jobs: _down([a], weight(k2), xs, mod, gate_row, 0.5, rpm, "ffn_down", jobs), k2)
        a, _ = _ffn_up(h, weight(k1))
        return _down([a], weight(k2), xs, mod, gate_row, 0.5, rpm, "ffn_down")[0]

    for layer in range(depth):
        last = layer == depth - 1
        even = layer % 2 == 0
        ctx_needed = even or not last
        gain = norm_g[layer]
        mod_l = mod_all[layer, :batch]
        mod_c = mod_all[layer, batch:batch + 1]

        xl = ffn(xl, mod_l, gain, layer, 0, (0, 0, 1), 2, seq, True)
        if ctx_needed:
            xc = ffn(xc, mod_c, gain, layer, 0, (0, 0, 1), 2, n_ctx, False)

        k_in, k_out = ("in", layer), ("out", layer)
        hl = _adaln(xl, mod_l, gain, (1, 3, 4), seq)
        if even:
            e = layer // 2
            pl_ = hosted(lambda jobs: _proj(hl, weight(k_in), "ab_proj", jobs), k_in)
            pc_, _ = _proj(_adaln(xc, mod_c, gain, (1, 3, 4), n_ctx), weight(k_in), "ab_proj")
            o_attn = _attention(pl_, [pl_, pc_], attn_q_gain[e], attn_k_gain[e], tables, batch, col_q, col_k, col_v)
            s_zero = jnp.zeros((batch, n_hg_heads, HEAD_DIM, HEAD_DIM), F32)
            lbs = [lb_table[dd, layer].reshape(1, hg_w) for dd in range(2)]
            zcols = (col_zf, col_zb)
            sc_dir = []
            o_prev = None
            for dd in range(2):
                o_prev, s_c = _hgrn_scan(pc_, lbs[dd], s_zero, batch, (col_qb, zcols[dd], col_ib, col_gb), dd == 1,
                                         o_prev, hgrn_out_gain[e] if dd == 1 else None)
                sc_dir.append(s_c)
            o_rec_c = o_prev
            o_prev = None
            for dd in range(2):
                o_prev, _ = _hgrn_scan(pl_, lbs[dd], sc_dir[dd], batch, (col_qb, zcols[dd], col_ib, col_gb), dd == 1,
                                       o_prev, hgrn_out_gain[e] if dd == 1 else None)
            mixed = [o_attn, o_prev]
            if not last:
                o_attn_c = _attention(pc_, [pc_], attn_q_gain[e], attn_k_gain[e], None, batch, col_q, col_k, col_v)
                mixed_c = [o_attn_c, o_rec_c]
        else:
            o = layer // 2
            pl_ = hosted(lambda jobs: _proj(hl, weight(k_in), "conv_proj", jobs), k_in)
            mixed = [_gated_conv(pl_, conv_w[o], batch)]
            if not last:
                pc_, _ = _proj(_adaln(xc, mod_c, gain, (1, 3, 4), n_ctx), weight(k_in), "conv_proj")
                mixed_c = [_gated_conv(pc_, conv_w[o], batch)]
        x_res = xl
        xl = hosted(lambda jobs: _down(mixed, weight(k_out), x_res, mod_l, 5, 1.0, seq, "mixer_out", jobs), k_out)
        if not last:
            xc = _down(mixed_c, weight(k_out), xc, mod_c, 5, 1.0, n_ctx, "mixer_out")[0]

        xl = ffn(xl, mod_l, gain, layer, 1, (2, 6, 7), 8, seq, True)
        if not last:
            xc = ffn(xc, mod_c, gain, layer, 1, (2, 6, 7), 8, n_ctx, False)

    return xl.reshape(batch, seq, d)
```

```python
import functools

import numpy as np
import jax
import jax.numpy as jnp
from jax import lax
from jax.experimental import pallas as pl
from jax.experimental.pallas import tpu as pltpu

GRID_W = 64
HEAD_DIM = 128
KV_GROUP = 4
ROPE_THETA = 10000.0
ROPE_AXIS_DIM = HEAD_DIM // 2
HGRN_CHUNK = 32
N_MOD = 9
EPS = 1e-6
LOG2_E = 1.4426950408889634

VMEM_LIMIT_BYTES = 56 * 1024 * 1024
MATMUL_VMEM_BUDGET = 46 * 1024 * 1024
MATMUL_ROWS = 1024
MXU_COLS = 256
W_RING = 3
ATTN_Q_ROWS = 256
ATTN_ONES_ROWS = 16
HGRN_BLOCK = 256
HGRN_HEADS_PER_STEP = 4
CAST_BLOCK_BYTES = 8 * 1024 * 1024
CAST_SLAB_ROWS = 16
HOST_CAST_FRACTION = 0.2

BF16 = jnp.bfloat16
F32 = jnp.float32


def _params(*sem):
    return pltpu.CompilerParams(dimension_semantics=sem, vmem_limit_bytes=VMEM_LIMIT_BYTES)


def _pick(n, candidates):
    for c in candidates:
        if n % c == 0:
            return c
    return n


def _mod_kernel(c_ref, w_ref, b_ref, o_ref):
    a = jax.nn.silu(c_ref[...]).astype(BF16)
    o_ref[0] = jnp.dot(a, w_ref[0].astype(BF16), preferred_element_type=F32) + b_ref[0]


def _mod_all(cc, mod_w, mod_b):
    depth, d, n = mod_w.shape
    rows = cc.shape[0]
    tn = _pick(n, (512, 256, 128))
    return pl.pallas_call(
        _mod_kernel,
        grid=(depth, n // tn),
        in_specs=[
            pl.BlockSpec((rows, d), lambda l, j: (0, 0)),
            pl.BlockSpec((1, d, tn), lambda l, j: (l, 0, j)),
            pl.BlockSpec((1, 1, tn), lambda l, j: (l, 0, j)),
        ],
        out_specs=pl.BlockSpec((1, rows, tn), lambda l, j: (l, 0, j)),
        out_shape=jax.ShapeDtypeStruct((depth, rows, n), F32),
        compiler_params=_params("parallel", "parallel"),
        name="mod_proj",
    )(cc, mod_w, mod_b.reshape(depth, 1, n))


def _cast_rows(src_ref, dst_ref):
    tn = dst_ref.shape[2]
    for t in range(dst_ref.shape[0]):
        dst_ref[t] = src_ref[:, t * tn:(t + 1) * tn].astype(dst_ref.dtype)


def _weight_bf16(w, lead, tn):
    r, c = w.shape[-2:]
    tr = _pick(r, tuple(t for t in (2048, 1024, 512, 256, 128, 64, 32, 16) if t * c * 4 <= CAST_BLOCK_BYTES))
    return pl.pallas_call(
        _cast_rows,
        grid=(r // tr,),
        in_specs=[pl.BlockSpec((None,) * len(lead) + (tr, c), lambda i: tuple(lead) + (i, 0))],
        out_specs=pl.BlockSpec((c // tn, tr, tn), lambda i: (0, i, 0)),
        out_shape=jax.ShapeDtypeStruct((c // tn, r, tn), BF16),
        compiler_params=_params("parallel"),
        name="weight_cast",
    )(w)


def _call_with_casts(kernel_fn, grid, in_specs, out_specs, out_shape, args, jobs, name, scratch_shapes=()):
    n_in, n_out, n_jobs = len(in_specs), len(out_specs), len(jobs)
    steps, nj = grid[0] * grid[1], grid[1]
    in_specs, out_specs, out_shape, args = list(in_specs), list(out_specs), list(out_shape), list(args)
    for w, lead, tn in jobs:
        r, c = w.shape[-2:]
        slab = next(t for t in range(CAST_SLAB_ROWS, r + 1, CAST_SLAB_ROWS) if r % t == 0 and r // t <= steps)

        def slab_idx(i, j, last=r // slab - 1):
            return jnp.minimum(i * nj + j, last)

        in_specs.append(pl.BlockSpec((None,) * len(lead) + (slab, c),
                                     lambda i, j, lead=tuple(lead), f=slab_idx: lead + (f(i, j), 0)))
        out_specs.append(pl.BlockSpec((c // tn, slab, tn), lambda i, j, f=slab_idx: (0, f(i, j), 0)))
        out_shape.append(jax.ShapeDtypeStruct((c // tn, r, tn), BF16))
        args.append(w)

    def body(*refs):
        ins, srcs = refs[:n_in], refs[n_in:n_in + n_jobs]
        outs = refs[n_in + n_jobs:n_in + n_jobs + n_out]
        dsts = refs[n_in + n_jobs + n_out:n_in + 2 * n_jobs + n_out]
        for src, dst in zip(srcs, dsts):
            _cast_rows(src, dst)
        kernel_fn(*ins, *outs, *refs[n_in + 2 * n_jobs + n_out:])

    res = pl.pallas_call(
        body,
        grid=grid,
        in_specs=in_specs,
        out_specs=out_specs,
        out_shape=out_shape,
        scratch_shapes=list(scratch_shapes),
        compiler_params=_params("arbitrary" if jobs or scratch_shapes else "parallel", "arbitrary"),
        name=name,
    )(*args)
    return res[:n_out], res[n_out:]


def _adaln_kernel(x_ref, mod_ref, gain_ref, o_ref, *, rows3):
    gain_row, shift_row, scale_row = rows3
    gs = gain_ref[gain_row:gain_row + 1, :] * (1.0 + mod_ref[0, scale_row:scale_row + 1, :])
    shift = mod_ref[0, shift_row:shift_row + 1, :]

    def body(r, carry):
        sl = pl.ds(pl.multiple_of(r * 8, 8), 8)
        x = x_ref[sl, :]
        y = x * lax.rsqrt(jnp.mean(x * x, axis=-1, keepdims=True) + EPS)
        o_ref[sl, :] = (y * gs + shift).astype(o_ref.dtype)
        return carry

    lax.fori_loop(0, x_ref.shape[0] // 8, body, 0, unroll=8)


def _adaln(x2d, mod, gain, rows3, rows_per_mod):
    m, d = x2d.shape
    tr = _pick(rows_per_mod, (512, 256, 128, 64, 32, 16, 8))
    bpm = rows_per_mod // tr
    return pl.pallas_call(
        functools.partial(_adaln_kernel, rows3=rows3),
        grid=(m // tr,),
        in_specs=[
            pl.BlockSpec((tr, d), lambda i: (i, 0)),
            pl.BlockSpec((1, N_MOD, d), lambda i: (i // bpm, 0, 0)),
            pl.BlockSpec(gain.shape, lambda i: (0, 0)),
        ],
        out_specs=pl.BlockSpec((tr, d), lambda i: (i, 0)),
        out_shape=jax.ShapeDtypeStruct((m, d), BF16),
        compiler_params=_params("parallel"),
        name="adaln",
    )(x2d, mod, gain)


def _row_tile(rows):
    return _pick(rows, (MATMUL_ROWS, 512, 256, 128, 64, 32, 16, 8))


def _col_tile(tm, n, a_bytes_per_row, w_bytes_per_col, io_bytes_per_elem):
    for tn in (1024, 512, 256, 128):
        vmem = 2 * (tm * a_bytes_per_row + tn * w_bytes_per_col + tm * tn * io_bytes_per_elem)
        if n % tn == 0 and vmem <= MATMUL_VMEM_BUDGET:
            return tn
    return 128


def _up_tn(rows, d, f):
    return _col_tile(_row_tile(rows), f, 2 * d, 2 * 2 * d, 2)


def _proj_tn(rows, d, n):
    return _col_tile(_row_tile(rows), n, 2 * d, 2 * d, 4)


def _down_tn(rows, k, d):
    return _col_tile(_row_tile(rows), d, 2 * k, 2 * k, 2 * 4)


def _up_kernel(h_ref, wg_ref, wu_ref, o_ref):
    h = h_ref[...]
    tn = o_ref.shape[1]
    part = MXU_COLS if tn % MXU_COLS == 0 else tn
    for c in range(0, tn, part):
        g = jnp.dot(h, wg_ref[:, c:c + part], preferred_element_type=F32)
        u = jnp.dot(h, wu_ref[:, c:c + part], preferred_element_type=F32)
        o_ref[:, c:c + part] = (jax.nn.silu(g) * u).astype(o_ref.dtype)


def _proj_kernel(h_ref, w_ref, o_ref):
    o_ref[...] = jnp.dot(h_ref[...], w_ref[...], preferred_element_type=F32).astype(o_ref.dtype)


def _ffn_up(h, w1, jobs=()):
    m, d = h.shape
    tn = w1.shape[2]
    nj = w1.shape[0] // 2
    tm = _row_tile(m)
    (a,), casts = _call_with_casts(
        _up_kernel,
        (m // tm, nj),
        [
            pl.BlockSpec((tm, d), lambda i, j: (i, 0)),
            pl.BlockSpec((None, d, tn), lambda i, j: (j, 0, 0)),
            pl.BlockSpec((None, d, tn), lambda i, j: (j + nj, 0, 0)),
        ],
        [pl.BlockSpec((tm, tn), lambda i, j: (i, j))],
        [jax.ShapeDtypeStruct((m, nj * tn), BF16)],
        [h, w1, w1], jobs, "ffn_up")
    return a, casts


def _proj(h, w, name, jobs=()):
    m, d = h.shape
    nj, _, tn = w.shape
    tm = _row_tile(m)
    (p,), casts = _call_with_casts(
        _proj_kernel,
        (m // tm, nj),
        [
            pl.BlockSpec((tm, d), lambda i, j: (i, 0)),
            pl.BlockSpec((None, d, tn), lambda i, j: (j, 0, 0)),
        ],
        [pl.BlockSpec((tm, tn), lambda i, j: (i, j))],
        [jax.ShapeDtypeStruct((m, nj * tn), F32)],
        [h, w], jobs, name)
    return p, casts


def _down_kernel(*refs, n_a, coef, steps):
    a_refs = refs[:n_a]
    w_hbm, x_ref, gate_ref, o_ref, w_ring, w_sem = refs[n_a:]
    nj = pl.num_programs(1)
    step = pl.program_id(0) * nj + pl.program_id(1)
    kk = a_refs[0].shape[1]

    def w_copy(s):
        slot = s % W_RING
        return pltpu.make_async_copy(w_hbm.at[s % nj], w_ring.at[slot], w_sem.at[slot])

    @pl.when(step == 0)
    def _():
        for s in range(min(W_RING - 1, steps)):
            w_copy(s).start()

    @pl.when(step + (W_RING - 1) < steps)
    def _():
        w_copy(step + (W_RING - 1)).start()

    w_copy(step).wait()
    w_tile = w_ring.at[step % W_RING]
    acc = jnp.dot(a_refs[0][...], w_tile[0:kk, :], preferred_element_type=F32)
    for r in range(1, n_a):
        acc = acc + jnp.dot(a_refs[r][...], w_tile[r * kk:(r + 1) * kk, :], preferred_element_type=F32)
    gate = gate_ref[0, pl.program_id(1)]
    if coef != 1.0:
        gate = coef * gate
    o_ref[...] = x_ref[...] + gate * acc


def _down(a_list, w, x2d, mod, gate_row, coef, rows_per_mod, name, jobs=()):
    m, d = x2d.shape
    n_a = len(a_list)
    kk = a_list[0].shape[1]
    nj, _, tn = w.shape
    tm = _row_tile(rows_per_mod)
    bpm = rows_per_mod // tm
    gate = mod[:, gate_row].reshape(mod.shape[0], nj, 1, tn)
    in_specs = [pl.BlockSpec((tm, kk), lambda i, j: (i, 0)) for _ in a_list]
    in_specs += [
        pl.BlockSpec(memory_space=pl.ANY),
        pl.BlockSpec((tm, tn), lambda i, j: (i, j)),
        pl.BlockSpec((1, nj, 1, tn), lambda i, j: (i // bpm, 0, 0, 0)),
    ]
    grid = (m // tm, nj)
    (y,), casts = _call_with_casts(
        functools.partial(_down_kernel, n_a=n_a, coef=coef, steps=grid[0] * grid[1]),
        grid,
        in_specs,
        [pl.BlockSpec((tm, tn), lambda i, j: (i, j))],
        [jax.ShapeDtypeStruct((m, d), F32)],
        [*a_list, w, x2d, gate], jobs, name,
        scratch_shapes=[pltpu.VMEM((W_RING,) + w.shape[1:], BF16), pltpu.SemaphoreType.DMA((W_RING,))])
    return y, casts


def _head_norm(x, gain):
    return x * lax.rsqrt(jnp.mean(x * x, axis=-1, keepdims=True) + EPS) * gain


def _rope(x, c, s_hi, s_lo):
    return x * c + pltpu.roll(x, HEAD_DIM - ROPE_AXIS_DIM // 2, 1) * s_hi + pltpu.roll(x, ROPE_AXIS_DIM // 2, 1) * s_lo


def _attn_kernel(*refs, kv_rows, rope):
    n_kv = len(kv_rows)
    q_ref = refs[0]
    kv_refs = refs[1:1 + 2 * n_kv]
    qg_ref, kg_ref = refs[1 + 2 * n_kv:3 + 2 * n_kv]
    pos = 3 + 2 * n_kv
    if rope:
        cq_ref, hq_ref, lq_ref, ck_ref, hk_ref, lk_ref = refs[pos:pos + 6]
        pos += 6
    o_ref, k_scr, vt_scr = refs[pos:pos + 3]
    tq = q_ref.shape[0]

    @pl.when(pl.program_id(2) == 0)
    def _():
        off = 0
        for part, rows in enumerate(kv_rows):
            kn = _head_norm(kv_refs[2 * part][...], kg_ref[...])
            if rope and part == 0:
                kn = _rope(kn, ck_ref[...], hk_ref[...], lk_ref[...])
            k_scr[off:off + rows, :] = kn.astype(BF16)
            vt_scr[:HEAD_DIM, off:off + rows] = kv_refs[2 * part + 1][...].T.astype(BF16)
            off += rows
        vt_scr[HEAD_DIM:, :] = jnp.ones((vt_scr.shape[0] - HEAD_DIM, off), BF16)

    scale = HEAD_DIM ** -0.5 * LOG2_E
    qs = []
    for g in range(KV_GROUP):
        q = _head_norm(q_ref[:, g * HEAD_DIM:(g + 1) * HEAD_DIM], qg_ref[...])
        if rope:
            q = _rope(q, cq_ref[...], hq_ref[...], lq_ref[...])
        qs.append((q * scale).astype(BF16))
    q_all = jnp.concatenate(qs, axis=0)
    s_t = lax.dot_general(k_scr[...], q_all, (((1,), (1,)), ((), ())), preferred_element_type=F32)
    p_t = jnp.exp2(s_t - jnp.max(s_t, axis=0, keepdims=True)).astype(BF16)
    o_t = jnp.dot(vt_scr[...], p_t, preferred_element_type=F32)
    o = (o_t[:HEAD_DIM] / o_t[HEAD_DIM:HEAD_DIM + 1]).T
    for g in range(KV_GROUP):
        o_ref[:, g * HEAD_DIM:(g + 1) * HEAD_DIM] = o[g * tq:(g + 1) * tq].astype(o_ref.dtype)


def _attention(pq, kv_sources, q_gain, k_gain, tables, batch, q_col0, k_col0, v_col0):
    t = pq.shape[0] // batch
    n_kv_heads = (k_col0 - q_col0) // (KV_GROUP * HEAD_DIM)
    tq = _pick(t, (ATTN_Q_ROWS, 256, 128, 64, 32, 16, 8))
    nq = t // tq
    gw = KV_GROUP * HEAD_DIM
    kv_rows = tuple(src.shape[0] // batch for src in kv_sources)
    rope = tables is not None
    in_specs = [pl.BlockSpec((tq, gw), lambda b, h, i: (b * nq + i, q_col0 // gw + h))]
    args = [pq]
    for src, rows in zip(kv_sources, kv_rows):
        in_specs.append(pl.BlockSpec((rows, HEAD_DIM), lambda b, h, i: (b, k_col0 // HEAD_DIM + h)))
        in_specs.append(pl.BlockSpec((rows, HEAD_DIM), lambda b, h, i: (b, v_col0 // HEAD_DIM + h)))
        args += [src, src]
    in_specs += [pl.BlockSpec((1, HEAD_DIM), lambda b, h, i: (0, 0))] * 2
    args += [q_gain.reshape(1, HEAD_DIM), k_gain.reshape(1, HEAD_DIM)]
    if rope:
        in_specs += [pl.BlockSpec((tq, HEAD_DIM), lambda b, h, i: (i, 0))] * 3
        in_specs += [pl.BlockSpec((t, HEAD_DIM), lambda b, h, i: (0, 0))] * 3
        args += list(tables) * 2
    return pl.pallas_call(
        functools.partial(_attn_kernel, kv_rows=kv_rows, rope=rope),
        grid=(batch, n_kv_heads, nq),
        in_specs=in_specs,
        out_specs=pl.BlockSpec((tq, gw), lambda b, h, i: (b * nq + i, h)),
        out_shape=jax.ShapeDtypeStruct((batch * t, n_kv_heads * gw), BF16),
        scratch_shapes=[pltpu.VMEM((sum(kv_rows), HEAD_DIM), BF16),
                        pltpu.VMEM((HEAD_DIM + ATTN_ONES_ROWS, sum(kv_rows)), BF16)],
        compiler_params=_params("parallel", "parallel", "arbitrary"),
        name="gqa_rope" if rope else "gqa_ctx",
    )(*args)


def _rope_tables(t):
    rows = t // GRID_W
    row_ids = np.repeat(np.arange(rows, dtype=np.float32), GRID_W)
    col_ids = np.tile(np.arange(GRID_W, dtype=np.float32), rows)
    inv_freq = jnp.asarray(ROPE_THETA, F32) ** (-jnp.arange(0, ROPE_AXIS_DIM, 2, dtype=F32) / ROPE_AXIS_DIM)
    ang_r = jnp.asarray(row_ids)[:, None] * inv_freq
    ang_c = jnp.asarray(col_ids)[:, None] * inv_freq
    zero = jnp.zeros_like(ang_r)
    cos = jnp.concatenate([jnp.cos(ang_r), jnp.cos(ang_r), jnp.cos(ang_c), jnp.cos(ang_c)], axis=1)
    s_hi = jnp.concatenate([-jnp.sin(ang_r), zero, -jnp.sin(ang_c), zero], axis=1)
    s_lo = jnp.concatenate([zero, jnp.sin(ang_r), zero, jnp.sin(ang_c)], axis=1)
    return cos, s_hi, s_lo


def _chunk_tri(rev):
    idx = np.arange(HGRN_BLOCK)
    same = (idx[:, None] // HGRN_CHUNK) == (idx[None, :] // HGRN_CHUNK)
    tri = (idx[None, :] >= idx[:, None]) if rev else (idx[None, :] <= idx[:, None])
    return jnp.asarray((same & tri).astype(np.float32), BF16)


def _chunk_cols():
    idx = np.arange(HGRN_BLOCK)
    col = np.arange((HGRN_BLOCK // HGRN_CHUNK) * HEAD_DIM)
    return jnp.asarray(((idx[:, None] // HGRN_CHUNK) == (col[None, :] // HEAD_DIM)).astype(np.float32), BF16)


def _hgrn_kernel(*refs, rev, final, n_heads):
    q_ref, z_ref, v_ref, lb_ref, tri_ref, cmask_ref, s0_ref = refs[:7]
    pos = 7
    if final:
        oprev_ref, g_ref, gain_ref = refs[7:10]
        pos = 10
    o_ref, sfin_ref, st_scr = refs[pos:pos + 3]
    n_chunks = HGRN_BLOCK // HGRN_CHUNK
    ref_row = HGRN_CHUNK - 1 - HGRN_CHUNK // 2 if rev else HGRN_CHUNK // 2
    end_row = 0 if rev else HGRN_CHUNK - 1

    @pl.when(pl.program_id(2) == 0)
    def _():
        st_scr[...] = s0_ref[0]

    gw = n_heads * HEAD_DIM
    tri = tri_ref[...]
    in_chunk = tri > 0
    z = z_ref[...]
    lb = lb_ref[...]
    sig = jax.nn.sigmoid(z)
    log_f = jnp.log(lb + (1.0 - lb) * sig)
    k = (1.0 - lb) * (1.0 - sig)
    q = jax.nn.silu(q_ref[...])
    v = v_ref[...]
    g1 = log_f.astype(BF16)
    r1 = log_f - g1.astype(F32)
    g2 = r1.astype(BF16)
    g3 = (r1 - g2.astype(F32)).astype(BF16)
    cum = (jnp.dot(tri, g1, preferred_element_type=F32) + jnp.dot(tri, g2, preferred_element_type=F32)
           + jnp.dot(tri, g3, preferred_element_type=F32))
    shape3 = (n_chunks, HGRN_CHUNK, gw)
    cum3 = cum.reshape(shape3)
    ref3 = cum3[:, ref_row:ref_row + 1, :]
    end3 = cum3[:, end_row:end_row + 1, :]
    rel = cum3 - ref3
    qa3 = q.reshape(shape3) * jnp.exp(rel)
    ka3 = k.reshape(shape3) * jnp.exp(-rel)
    qa = qa3.reshape(cum.shape).astype(BF16)
    ka = ka3.reshape(cum.shape).astype(BF16)
    qe = (qa3 * jnp.exp(ref3)).reshape(cum.shape).astype(BF16)
    kd = (ka3 * jnp.exp(end3 - ref3)).reshape(cum.shape).astype(BF16)
    dec = jnp.exp(end3)
    vb = v.astype(BF16)
    chunk_cols = cmask_ref[...]

    heads = []
    for h in range(n_heads):
        cols = slice(h * HEAD_DIM, (h + 1) * HEAD_DIM)
        attn = lax.dot_general(qa[:, cols], ka[:, cols], (((1,), (1,)), ((), ())), preferred_element_type=F32)
        attn = jnp.where(in_chunk, attn, 0.0)
        o_intra = jnp.dot(attn.astype(BF16), vb[:, cols], preferred_element_type=F32)
        kd_wide = jnp.concatenate([kd[:, cols]] * n_chunks, axis=1) * chunk_cols
        upd = jnp.dot(v[:, cols].T.astype(BF16), kd_wide, preferred_element_type=F32)
        heads.append(dict(o_intra=o_intra, upd=upd, state=st_scr[h]))

    order = range(n_chunks - 1, -1, -1) if rev else range(n_chunks)
    for c in order:
        rows = slice(c * HGRN_CHUNK, (c + 1) * HGRN_CHUNK)
        for h, hd in enumerate(heads):
            cols = slice(h * HEAD_DIM, (h + 1) * HEAD_DIM)
            inter = lax.dot_general(qe[rows, cols], hd["state"].astype(BF16), (((1,), (1,)), ((), ())),
                                    preferred_element_type=F32)
            o = hd["o_intra"][rows] + inter
            hd["state"] = hd["state"] * dec[c][:, cols] + hd["upd"][:, c * HEAD_DIM:(c + 1) * HEAD_DIM]
            if final:
                o = o + oprev_ref[rows, cols]
                y = o * lax.rsqrt(jnp.mean(o * o, axis=-1, keepdims=True) + EPS) * gain_ref[...]
                o = y * jax.nn.silu(g_ref[rows, cols])
            o_ref[rows, cols] = o.astype(o_ref.dtype)
    for h, hd in enumerate(heads):
        st_scr[h] = hd["state"]
        sfin_ref[0, h] = hd["state"]


def _hgrn_scan(p, lb_row, s0, batch, cols, rev, o_prev=None, out_gain=None):
    t = p.shape[0] // batch
    nsb = t // HGRN_BLOCK
    n_heads = s0.shape[1]
    hps = next(g for g in (HGRN_HEADS_PER_STEP, 2, 1)
               if n_heads % g == 0 and all(c % (g * HEAD_DIM) == 0 for c in cols))
    gw = hps * HEAD_DIM
    final = o_prev is not None
    q0, z0, v0, g0 = (c // gw for c in cols)

    def row_blk(b, s):
        return b * nsb + (nsb - 1 - s if rev else s)

    blk = (HGRN_BLOCK, gw)
    st_blk = (1, hps, HEAD_DIM, HEAD_DIM)
    in_specs = [
        pl.BlockSpec(blk, lambda b, h, s: (row_blk(b, s), q0 + h)),
        pl.BlockSpec(blk, lambda b, h, s: (row_blk(b, s), z0 + h)),
        pl.BlockSpec(blk, lambda b, h, s: (row_blk(b, s), v0 + h)),
        pl.BlockSpec((1, gw), lambda b, h, s: (0, h)),
        pl.BlockSpec((HGRN_BLOCK, HGRN_BLOCK), lambda b, h, s: (0, 0)),
        pl.BlockSpec((HGRN_BLOCK, (HGRN_BLOCK // HGRN_CHUNK) * HEAD_DIM), lambda b, h, s: (0, 0)),
        pl.BlockSpec(st_blk, lambda b, h, s: (b, h, 0, 0)),
    ]
    args = [p, p, p, lb_row, _chunk_tri(rev), _chunk_cols(), s0]
    if final:
        in_specs += [
            pl.BlockSpec(blk, lambda b, h, s: (row_blk(b, s), h)),
            pl.BlockSpec(blk, lambda b, h, s: (row_blk(b, s), g0 + h)),
            pl.BlockSpec((1, HEAD_DIM), lambda b, h, s: (0, 0)),
        ]
        args += [o_prev, p, out_gain.reshape(1, HEAD_DIM)]
    return pl.pallas_call(
        functools.partial(_hgrn_kernel, rev=rev, final=final, n_heads=hps),
        grid=(batch, n_heads // hps, nsb),
        in_specs=in_specs,
        out_specs=[
            pl.BlockSpec(blk, lambda b, h, s: (row_blk(b, s), h)),
            pl.BlockSpec(st_blk, lambda b, h, s: (b, h, 0, 0)),
        ],
        out_shape=[
            jax.ShapeDtypeStruct((batch * t, n_heads * HEAD_DIM), BF16 if final else F32),
            jax.ShapeDtypeStruct(s0.shape, F32),
        ],
        scratch_shapes=[pltpu.VMEM((hps, HEAD_DIM, HEAD_DIM), F32)],
        compiler_params=_params("parallel", "parallel", "arbitrary"),
        name="hgrn_bwd" if rev else "hgrn_fwd",
    )(*args)


def _conv_kernel(b_ref, c_ref, u_ref, w_ref, o_ref):
    u = c_ref[...] * u_ref[...]
    t = u.shape[0]
    row = lax.broadcasted_iota(jnp.int32, u.shape, 0)
    prev = jnp.where(row == 0, 0.0, pltpu.roll(u, 1, 0))
    nxt = jnp.where(row == t - 1, 0.0, pltpu.roll(u, t - 1, 0))
    y = prev * w_ref[0:1, :] + u * w_ref[1:2, :] + nxt * w_ref[2:3, :]
    o_ref[...] = (b_ref[...] * y).astype(o_ref.dtype)


def _gated_conv(p, conv_w, batch):
    d = conv_w.shape[1]
    t = p.shape[0] // batch
    td = _pick(d, (256, 128))
    nd = d // td
    return pl.pallas_call(
        _conv_kernel,
        grid=(batch, nd),
        in_specs=[
            pl.BlockSpec((t, td), lambda b, j: (b, j)),
            pl.BlockSpec((t, td), lambda b, j: (b, nd + j)),
            pl.BlockSpec((t, td), lambda b, j: (b, 2 * nd + j)),
            pl.BlockSpec((conv_w.shape[0], td), lambda b, j: (0, j)),
        ],
        out_specs=pl.BlockSpec((t, td), lambda b, j: (b, j)),
        out_shape=jax.ShapeDtypeStruct((batch * t, d), BF16),
        compiler_params=_params("parallel", "parallel"),
        name="gated_conv3",
    )(p, p, p, conv_w)


def kernel(x, c, ctx, c_ctx, mod_w, mod_b, norm_g, ffn_w1, ffn_w2, ab_w_in, ab_w_out, attn_q_gain, attn_k_gain, hgrn_lb_logits, hgrn_out_gain, conv_w_in, conv_w, conv_w_out):
    batch, seq, d = x.shape
    ctx_len = ctx.shape[1]
    depth = mod_w.shape[0]
    attn_w = d // 2
    kv_w = attn_w // KV_GROUP
    hg_w = d - attn_w
    n_hg_heads = hg_w // HEAD_DIM
    col_q, col_k, col_v = 0, attn_w, attn_w + kv_w
    col_qb = attn_w + 2 * kv_w
    col_zf, col_zb, col_ib, col_gb = (col_qb + hg_w * i for i in range(1, 5))

    lb_table = jnp.cumsum(jax.nn.softmax(hgrn_lb_logits.astype(F32), axis=1), axis=1)
    tables = _rope_tables(seq)

    pad = (-(batch + 1)) % 8
    cc = jnp.concatenate([c, c_ctx[None], jnp.zeros((pad, d), F32)], axis=0)
    mod_all = _mod_all(cc, mod_w, mod_b).reshape(depth, batch + 1 + pad, N_MOD, d)

    xl = x.reshape(batch * seq, d)
    xc = ctx.reshape(batch * ctx_len, d)
    n_ctx = batch * ctx_len

    n_lat = batch * seq
    f_ff = ffn_w2.shape[2]
    tn_up, tn_down = _up_tn(n_lat, d, f_ff), _down_tn(seq, f_ff, d)
    needs = {}
    for layer in range(depth):
        mix_in, mix_out = (ab_w_in, ab_w_out) if layer % 2 == 0 else (conv_w_in, conv_w_out)
        needs[("w1", layer, 0)] = (ffn_w1, (layer, 0), tn_up)
        needs[("w2", layer, 0)] = (ffn_w2, (layer, 0), tn_down)
        needs[("in", layer)] = (mix_in, (layer // 2,), _proj_tn(n_lat, d, mix_in.shape[2]))
        needs[("out", layer)] = (mix_out, (layer // 2,), _down_tn(seq, mix_out.shape[1], d))
        needs[("w1", layer, 1)] = (ffn_w1, (layer, 1), tn_up)
        needs[("w2", layer, 1)] = (ffn_w2, (layer, 1), tn_down)
    ready = {}

    def weight(key):
        if key not in ready:
            ready[key] = _weight_bf16(*needs[key])
        return ready[key]

    def hosted(call, w_key, tiles=1):
        w = weight(w_key)
        steps = (n_lat // _row_tile(seq)) * (w.shape[0] // tiles)
        budget = HOST_CAST_FRACTION * steps * tiles * w.shape[1] * w.shape[2] * 2
        keys = []
        for key, (src, _, _) in needs.items():
            size = src.shape[-2] * src.shape[-1] * 4
            if key not in ready and size <= budget:
                keys.append(key)
                budget -= size
        out, casts = call([needs[k] for k in keys])
        ready.update(zip(keys, casts))
        return out

    def ffn(xs, mod, gain, layer, idx, rows3, gate_row, rpm, latent):
        h = _adaln(xs, mod, gain, rows3, rpm)
        k1, k2 = ("w1", layer, idx), ("w2", layer, idx)
        if latent:
            a = hosted(lambda jobs: _ffn_up(h, weight(k1), jobs), k1, tiles=2)
            return hosted(lambda jobs: _down([a], weight(k2), xs, mod, gate_row, 0.5, rpm, "ffn_down", jobs), k2)
        a, _ = _ffn_up(h, weight(k1))
        return _down([a], weight(k2), xs, mod, gate_row, 0.5, rpm, "ffn_down")[0]

    for layer in range(depth):
        last = layer == depth - 1
        even = layer % 2 == 0
        ctx_needed = even or not last
        gain = norm_g[layer]
        mod_l = mod_all[layer, :batch]
        mod_c = mod_all[layer, batch:batch + 1]

        xl = ffn(xl, mod_l, gain, layer, 0, (0, 0, 1), 2, seq, True)
        if ctx_needed:
            xc = ffn(xc, mod_c, gain, layer, 0, (0, 0, 1), 2, n_ctx, False)

        k_in, k_out = ("in", layer), ("out", layer)
        hl = _adaln(xl, mod_l, gain, (1, 3, 4), seq)
        if even:
            e = layer // 2
            pl_ = hosted(lambda jobs: _proj(hl, weight(k_in), "ab_proj", jobs), k_in)
            pc_, _ = _proj(_adaln(xc, mod_c, gain, (1, 3, 4), n_ctx), weight(k_in), "ab_proj")
            o_attn = _attention(pl_, [pl_, pc_], attn_q_gain[e], attn_k_gain[e], tables, batch, col_q, col_k, col_v)
            s_zero = jnp.zeros((batch, n_hg_heads, HEAD_DIM, HEAD_DIM), F32)
            lbs = [lb_table[dd, layer].reshape(1, hg_w) for dd in range(2)]
            zcols = (col_zf, col_zb)
            sc_dir = []
            o_prev = None
            for dd in range(2):
                o_prev, s_c = _hgrn_scan(pc_, lbs[dd], s_zero, batch, (col_qb, zcols[dd], col_ib, col_gb), dd == 1,
                                         o_prev, hgrn_out_gain[e] if dd == 1 else None)
                sc_dir.append(s_c)
            o_rec_c = o_prev
            o_prev = None
            for dd in range(2):
                o_prev, _ = _hgrn_scan(pl_, lbs[dd], sc_dir[dd], batch, (col_qb, zcols[dd], col_ib, col_gb), dd == 1,
                                       o_prev, hgrn_out_gain[e] if dd == 1 else None)
            mixed = [o_attn, o_prev]
            if not last:
                o_attn_c = _attention(pc_, [pc_], attn_q_gain[e], attn_k_gain[e], None, batch, col_q, col_k, col_v)
                mixed_c = [o_attn_c, o_rec_c]
        else:
            o = layer // 2
            pl_ = hosted(lambda jobs: _proj(hl, weight(k_in), "conv_proj", jobs), k_in)
            mixed = [_gated_conv(pl_, conv_w[o], batch)]
            if not last:
                pc_, _ = _proj(_adaln(xc, mod_c, gain, (1, 3, 4), n_ctx), weight(k_in), "conv_proj")
                mixed_c = [_gated_conv(pc_, conv_w[o], batch)]
        x_res = xl
        xl = hosted(lambda jobs: _down(mixed, weight(k_out), x_res, mod_l, 5, 1.0, seq, "mixer_out", jobs), k_out)
        if not last:
            xc = _down(mixed_c, weight(k_out), xc, mod_c, 5, 1.0, n_ctx, "mixer_out")[0]

        xl = ffn(xl, mod_l, gain, layer, 1, (2, 6, 7), 8, seq, True)
        if not last:
            xc = ffn(xc, mod_c, gain, layer, 1, (2, 6, 7), 8, n_ctx, False)

    return xl.reshape(batch, seq, d)
```

```python
import functools

import numpy as np
import jax
import jax.numpy as jnp
from jax import lax
from jax.experimental import pallas as pl
from jax.experimental.pallas import tpu as pltpu

GRID_W = 64
HEAD_DIM = 128
KV_GROUP = 4
ROPE_THETA = 10000.0
ROPE_AXIS_DIM = HEAD_DIM // 2
HGRN_CHUNK = 32
N_MOD = 9
EPS = 1e-6
LOG2_E = 1.4426950408889634

VMEM_LIMIT_BYTES = 56 * 1024 * 1024
MATMUL_VMEM_BUDGET = 46 * 1024 * 1024
MATMUL_ROWS = 1024
W_RING = 3
ATTN_Q_ROWS = 256
ATTN_ONES_ROWS = 16
HGRN_BLOCK = 256
HGRN_HEADS_PER_STEP = 4
CAST_BLOCK_BYTES = 8 * 1024 * 1024
CAST_SLAB_ROWS = 16
HOST_CAST_FRACTION = 0.2

BF16 = jnp.bfloat16
F32 = jnp.float32


def _params(*sem):
    return pltpu.CompilerParams(dimension_semantics=sem, vmem_limit_bytes=VMEM_LIMIT_BYTES)


def _pick(n, candidates):
    for c in candidates:
        if n % c == 0:
            return c
    return n


def _mod_kernel(c_ref, w_ref, b_ref, o_ref):
    a = jax.nn.silu(c_ref[...]).astype(BF16)
    o_ref[0] = jnp.dot(a, w_ref[0].astype(BF16), preferred_element_type=F32) + b_ref[0]


def _mod_all(cc, mod_w, mod_b):
    depth, d, n = mod_w.shape
    rows = cc.shape[0]
    tn = _pick(n, (512, 256, 128))
    return pl.pallas_call(
        _mod_kernel,
        grid=(depth, n // tn),
        in_specs=[
            pl.BlockSpec((rows, d), lambda l, j: (0, 0)),
            pl.BlockSpec((1, d, tn), lambda l, j: (l, 0, j)),
            pl.BlockSpec((1, 1, tn), lambda l, j: (l, 0, j)),
        ],
        out_specs=pl.BlockSpec((1, rows, tn), lambda l, j: (l, 0, j)),
        out_shape=jax.ShapeDtypeStruct((depth, rows, n), F32),
        compiler_params=_params("parallel", "parallel"),
        name="mod_proj",
    )(cc, mod_w, mod_b.reshape(depth, 1, n))


def _cast_rows(src_ref, dst_ref):
    tn = dst_ref.shape[2]
    for t in range(dst_ref.shape[0]):
        dst_ref[t] = src_ref[:, t * tn:(t + 1) * tn].astype(dst_ref.dtype)


def _weight_bf16(w, lead, tn):
    r, c = w.shape[-2:]
    tr = _pick(r, tuple(t for t in (2048, 1024, 512, 256, 128, 64, 32, 16) if t * c * 4 <= CAST_BLOCK_BYTES))
    return pl.pallas_call(
        _cast_rows,
        grid=(r // tr,),
        in_specs=[pl.BlockSpec((None,) * len(lead) + (tr, c), lambda i: tuple(lead) + (i, 0))],
        out_specs=pl.BlockSpec((c // tn, tr, tn), lambda i: (0, i, 0)),
        out_shape=jax.ShapeDtypeStruct((c // tn, r, tn), BF16),
        compiler_params=_params("parallel"),
        name="weight_cast",
    )(w)


def _call_with_casts(kernel_fn, grid, in_specs, out_specs, out_shape, args, jobs, name, scratch_shapes=()):
    n_in, n_out, n_jobs = len(in_specs), len(out_specs), len(jobs)
    steps, nj = grid[0] * grid[1], grid[1]
    in_specs, out_specs, out_shape, args = list(in_specs), list(out_specs), list(out_shape), list(args)
    for w, lead, tn in jobs:
        r, c = w.shape[-2:]
        slab = next(t for t in range(CAST_SLAB_ROWS, r + 1, CAST_SLAB_ROWS) if r % t == 0 and r // t <= steps)

        def slab_idx(i, j, last=r // slab - 1):
            return jnp.minimum(i * nj + j, last)

        in_specs.append(pl.BlockSpec((None,) * len(lead) + (slab, c),
                                     lambda i, j, lead=tuple(lead), f=slab_idx: lead + (f(i, j), 0)))
        out_specs.append(pl.BlockSpec((c // tn, slab, tn), lambda i, j, f=slab_idx: (0, f(i, j), 0)))
        out_shape.append(jax.ShapeDtypeStruct((c // tn, r, tn), BF16))
        args.append(w)

    def body(*refs):
        ins, srcs = refs[:n_in], refs[n_in:n_in + n_jobs]
        outs = refs[n_in + n_jobs:n_in + n_jobs + n_out]
        dsts = refs[n_in + n_jobs + n_out:n_in + 2 * n_jobs + n_out]
        for src, dst in zip(srcs, dsts):
            _cast_rows(src, dst)
        kernel_fn(*ins, *outs, *refs[n_in + 2 * n_jobs + n_out:])

    res = pl.pallas_call(
        body,
        grid=grid,
        in_specs=in_specs,
        out_specs=out_specs,
        out_shape=out_shape,
        scratch_shapes=list(scratch_shapes),
        compiler_params=_params("arbitrary" if jobs or scratch_shapes else "parallel", "arbitrary"),
        name=name,
    )(*args)
    return res[:n_out], res[n_out:]


def _adaln_kernel(x_ref, mod_ref, gain_ref, o_ref, *, rows3):
    gain_row, shift_row, scale_row = rows3
    gs = gain_ref[gain_row:gain_row + 1, :] * (1.0 + mod_ref[0, scale_row:scale_row + 1, :])
    shift = mod_ref[0, shift_row:shift_row + 1, :]

    def body(r, carry):
        sl = pl.ds(pl.multiple_of(r * 8, 8), 8)
        x = x_ref[sl, :]
        y = x * lax.rsqrt(jnp.mean(x * x, axis=-1, keepdims=True) + EPS)
        o_ref[sl, :] = (y * gs + shift).astype(o_ref.dtype)
        return carry

    lax.fori_loop(0, x_ref.shape[0] // 8, body, 0, unroll=8)


def _adaln(x2d, mod, gain, rows3, rows_per_mod):
    m, d = x2d.shape
    tr = _pick(rows_per_mod, (512, 256, 128, 64, 32, 16, 8))
    bpm = rows_per_mod // tr
    return pl.pallas_call(
        functools.partial(_adaln_kernel, rows3=rows3),
        grid=(m // tr,),
        in_specs=[
            pl.BlockSpec((tr, d), lambda i: (i, 0)),
            pl.BlockSpec((1, N_MOD, d), lambda i: (i // bpm, 0, 0)),
            pl.BlockSpec(gain.shape, lambda i: (0, 0)),
        ],
        out_specs=pl.BlockSpec((tr, d), lambda i: (i, 0)),
        out_shape=jax.ShapeDtypeStruct((m, d), BF16),
        compiler_params=_params("parallel"),
        name="adaln",
    )(x2d, mod, gain)


def _row_tile(rows):
    return _pick(rows, (MATMUL_ROWS, 512, 256, 128, 64, 32, 16, 8))


def _col_tile(tm, n, a_bytes_per_row, w_bytes_per_col, io_bytes_per_elem):
    for tn in (1024, 512, 256, 128):
        vmem = 2 * (tm * a_bytes_per_row + tn * w_bytes_per_col + tm * tn * io_bytes_per_elem)
        if n % tn == 0 and vmem <= MATMUL_VMEM_BUDGET:
            return tn
    return 128


def _up_tn(rows, d, f):
    return _col_tile(_row_tile(rows), f, 2 * d, 2 * 2 * d, 2)


def _proj_tn(rows, d, n):
    return _col_tile(_row_tile(rows), n, 2 * d, 2 * d, 4)


def _down_tn(rows, k, d):
    return _col_tile(_row_tile(rows), d, 2 * k, 2 * k, 2 * 4)


def _up_kernel(h_ref, wg_ref, wu_ref, o_ref):
    h = h_ref[...]
    g = jnp.dot(h, wg_ref[...], preferred_element_type=F32)
    u = jnp.dot(h, wu_ref[...], preferred_element_type=F32)
    o_ref[...] = (jax.nn.silu(g) * u).astype(o_ref.dtype)


def _proj_kernel(h_ref, w_ref, o_ref):
    o_ref[...] = jnp.dot(h_ref[...], w_ref[...], preferred_element_type=F32).astype(o_ref.dtype)


def _ffn_up(h, w1, jobs=()):
    m, d = h.shape
    tn = w1.shape[2]
    nj = w1.shape[0] // 2
    tm = _row_tile(m)
    (a,), casts = _call_with_casts(
        _up_kernel,
        (m // tm, nj),
        [
            pl.BlockSpec((tm, d), lambda i, j: (i, 0)),
            pl.BlockSpec((None, d, tn), lambda i, j: (j, 0, 0)),
            pl.BlockSpec((None, d, tn), lambda i, j: (j + nj, 0, 0)),
        ],
        [pl.BlockSpec((tm, tn), lambda i, j: (i, j))],
        [jax.ShapeDtypeStruct((m, nj * tn), BF16)],
        [h, w1, w1], jobs, "ffn_up")
    return a, casts


def _proj(h, w, name, jobs=()):
    m, d = h.shape
    nj, _, tn = w.shape
    tm = _row_tile(m)
    (p,), casts = _call_with_casts(
        _proj_kernel,
        (m // tm, nj),
        [
            pl.BlockSpec((tm, d), lambda i, j: (i, 0)),
            pl.BlockSpec((None, d, tn), lambda i, j: (j, 0, 0)),
        ],
        [pl.BlockSpec((tm, tn), lambda i, j: (i, j))],
        [jax.ShapeDtypeStruct((m, nj * tn), F32)],
        [h, w], jobs, name)
    return p, casts


def _down_kernel(*refs, n_a, coef, steps):
    a_refs = refs[:n_a]
    w_hbm, x_hbm, gate_ref, o_ref, w_ring, x_ring, w_sem, x_sem = refs[n_a:]
    nj = pl.num_programs(1)
    step = pl.program_id(0) * nj + pl.program_id(1)
    kk = a_refs[0].shape[1]
    tm, tn = o_ref.shape

    def copies(s):
        slot = s % W_RING
        rows = pl.ds(pl.multiple_of((s // nj) * tm, tm), tm)
        cols = pl.ds(pl.multiple_of((s % nj) * tn, tn), tn)
        return (pltpu.make_async_copy(w_hbm.at[s % nj], w_ring.at[slot], w_sem.at[slot]),
                pltpu.make_async_copy(x_hbm.at[rows, cols], x_ring.at[slot], x_sem.at[slot]))

    @pl.when(step == 0)
    def _():
        for s in range(min(W_RING - 1, steps)):
            for c in copies(s):
                c.start()

    @pl.when(step + (W_RING - 1) < steps)
    def _():
        for c in copies(step + (W_RING - 1)):
            c.start()

    for c in copies(step):
        c.wait()
    w_tile = w_ring.at[step % W_RING]
    acc = jnp.dot(a_refs[0][...], w_tile[0:kk, :], preferred_element_type=F32)
    for r in range(1, n_a):
        acc = acc + jnp.dot(a_refs[r][...], w_tile[r * kk:(r + 1) * kk, :], preferred_element_type=F32)
    gate = gate_ref[0, pl.program_id(1)]
    if coef != 1.0:
        gate = coef * gate
    o_ref[...] = x_ring[step % W_RING] + gate * acc


def _down(a_list, w, x2d, mod, gate_row, coef, rows_per_mod, name, jobs=()):
    m, d = x2d.shape
    n_a = len(a_list)
    kk = a_list[0].shape[1]
    nj, _, tn = w.shape
    tm = _row_tile(rows_per_mod)
    bpm = rows_per_mod // tm
    gate = mod[:, gate_row].reshape(mod.shape[0], nj, 1, tn)
    in_specs = [pl.BlockSpec((tm, kk), lambda i, j: (i, 0)) for _ in a_list]
    in_specs += [
        pl.BlockSpec(memory_space=pl.ANY),
        pl.BlockSpec(memory_space=pl.ANY),
        pl.BlockSpec((1, nj, 1, tn), lambda i, j: (i // bpm, 0, 0, 0)),
    ]
    grid = (m // tm, nj)
    (y,), casts = _call_with_casts(
        functools.partial(_down_kernel, n_a=n_a, coef=coef, steps=grid[0] * grid[1]),
        grid,
        in_specs,
        [pl.BlockSpec((tm, tn), lambda i, j: (i, j))],
        [jax.ShapeDtypeStruct((m, d), F32)],
        [*a_list, w, x2d, gate], jobs, name,
        scratch_shapes=[pltpu.VMEM((W_RING,) + w.shape[1:], BF16), pltpu.VMEM((W_RING, tm, tn), F32),
                        pltpu.SemaphoreType.DMA((W_RING,)), pltpu.SemaphoreType.DMA((W_RING,))])
    return y, casts


def _head_norm(x, gain):
    return x * lax.rsqrt(jnp.mean(x * x, axis=-1, keepdims=True) + EPS) * gain


def _rope(x, c, s_hi, s_lo):
    return x * c + pltpu.roll(x, HEAD_DIM - ROPE_AXIS_DIM // 2, 1) * s_hi + pltpu.roll(x, ROPE_AXIS_DIM // 2, 1) * s_lo


def _attn_kernel(*refs, kv_rows, rope):
    n_kv = len(kv_rows)
    q_ref = refs[0]
    kv_refs = refs[1:1 + 2 * n_kv]
    qg_ref, kg_ref = refs[1 + 2 * n_kv:3 + 2 * n_kv]
    pos = 3 + 2 * n_kv
    if rope:
        cq_ref, hq_ref, lq_ref, ck_ref, hk_ref, lk_ref = refs[pos:pos + 6]
        pos += 6
    o_ref, k_scr, vt_scr = refs[pos:pos + 3]
    tq = q_ref.shape[0]

    @pl.when(pl.program_id(2) == 0)
    def _():
        off = 0
        for part, rows in enumerate(kv_rows):
            kn = _head_norm(kv_refs[2 * part][...], kg_ref[...])
            if rope and part == 0:
                kn = _rope(kn, ck_ref[...], hk_ref[...], lk_ref[...])
            k_scr[off:off + rows, :] = kn.astype(BF16)
            vt_scr[:HEAD_DIM, off:off + rows] = kv_refs[2 * part + 1][...].T.astype(BF16)
            off += rows
        vt_scr[HEAD_DIM:, :] = jnp.ones((vt_scr.shape[0] - HEAD_DIM, off), BF16)

    scale = HEAD_DIM ** -0.5 * LOG2_E
    qs = []
    for g in range(KV_GROUP):
        q = _head_norm(q_ref[:, g * HEAD_DIM:(g + 1) * HEAD_DIM], qg_ref[...])
        if rope:
            q = _rope(q, cq_ref[...], hq_ref[...], lq_ref[...])
        qs.append((q * scale).astype(BF16))
    q_all = jnp.concatenate(qs, axis=0)
    s_t = lax.dot_general(k_scr[...], q_all, (((1,), (1,)), ((), ())), preferred_element_type=F32)
    p_t = jnp.exp2(s_t - jnp.max(s_t, axis=0, keepdims=True)).astype(BF16)
    o_t = jnp.dot(vt_scr[...], p_t, preferred_element_type=F32)
    o = (o_t[:HEAD_DIM] / o_t[HEAD_DIM:HEAD_DIM + 1]).T
    for g in range(KV_GROUP):
        o_ref[:, g * HEAD_DIM:(g + 1) * HEAD_DIM] = o[g * tq:(g + 1) * tq].astype(o_ref.dtype)


def _attention(pq, kv_sources, q_gain, k_gain, tables, batch, q_col0, k_col0, v_col0):
    t = pq.shape[0] // batch
    n_kv_heads = (k_col0 - q_col0) // (KV_GROUP * HEAD_DIM)
    tq = _pick(t, (ATTN_Q_ROWS, 256, 128, 64, 32, 16, 8))
    nq = t // tq
    gw = KV_GROUP * HEAD_DIM
    kv_rows = tuple(src.shape[0] // batch for src in kv_sources)
    rope = tables is not None
    in_specs = [pl.BlockSpec((tq, gw), lambda b, h, i: (b * nq + i, q_col0 // gw + h))]
    args = [pq]
    for src, rows in zip(kv_sources, kv_rows):
        in_specs.append(pl.BlockSpec((rows, HEAD_DIM), lambda b, h, i: (b, k_col0 // HEAD_DIM + h)))
        in_specs.append(pl.BlockSpec((rows, HEAD_DIM), lambda b, h, i: (b, v_col0 // HEAD_DIM + h)))
        args += [src, src]
    in_specs += [pl.BlockSpec((1, HEAD_DIM), lambda b, h, i: (0, 0))] * 2
    args += [q_gain.reshape(1, HEAD_DIM), k_gain.reshape(1, HEAD_DIM)]
    if rope:
        in_specs += [pl.BlockSpec((tq, HEAD_DIM), lambda b, h, i: (i, 0))] * 3
        in_specs += [pl.BlockSpec((t, HEAD_DIM), lambda b, h, i: (0, 0))] * 3
        args += list(tables) * 2
    return pl.pallas_call(
        functools.partial(_attn_kernel, kv_rows=kv_rows, rope=rope),
        grid=(batch, n_kv_heads, nq),
        in_specs=in_specs,
        out_specs=pl.BlockSpec((tq, gw), lambda b, h, i: (b * nq + i, h)),
        out_shape=jax.ShapeDtypeStruct((batch * t, n_kv_heads * gw), BF16),
        scratch_shapes=[pltpu.VMEM((sum(kv_rows), HEAD_DIM), BF16),
                        pltpu.VMEM((HEAD_DIM + ATTN_ONES_ROWS, sum(kv_rows)), BF16)],
        compiler_params=_params("parallel", "parallel", "arbitrary"),
        name="gqa_rope" if rope else "gqa_ctx",
    )(*args)


def _rope_tables(t):
    rows = t // GRID_W
    row_ids = np.repeat(np.arange(rows, dtype=np.float32), GRID_W)
    col_ids = np.tile(np.arange(GRID_W, dtype=np.float32), rows)
    inv_freq = jnp.asarray(ROPE_THETA, F32) ** (-jnp.arange(0, ROPE_AXIS_DIM, 2, dtype=F32) / ROPE_AXIS_DIM)
    ang_r = jnp.asarray(row_ids)[:, None] * inv_freq
    ang_c = jnp.asarray(col_ids)[:, None] * inv_freq
    zero = jnp.zeros_like(ang_r)
    cos = jnp.concatenate([jnp.cos(ang_r), jnp.cos(ang_r), jnp.cos(ang_c), jnp.cos(ang_c)], axis=1)
    s_hi = jnp.concatenate([-jnp.sin(ang_r), zero, -jnp.sin(ang_c), zero], axis=1)
    s_lo = jnp.concatenate([zero, jnp.sin(ang_r), zero, jnp.sin(ang_c)], axis=1)
    return cos, s_hi, s_lo


def _chunk_tri(rev):
    idx = np.arange(HGRN_BLOCK)
    same = (idx[:, None] // HGRN_CHUNK) == (idx[None, :] // HGRN_CHUNK)
    tri = (idx[None, :] >= idx[:, None]) if rev else (idx[None, :] <= idx[:, None])
    return jnp.asarray((same & tri).astype(np.float32), BF16)


def _chunk_cols():
    idx = np.arange(HGRN_BLOCK)
    col = np.arange((HGRN_BLOCK // HGRN_CHUNK) * HEAD_DIM)
    return jnp.asarray(((idx[:, None] // HGRN_CHUNK) == (col[None, :] // HEAD_DIM)).astype(np.float32), BF16)


def _hgrn_kernel(*refs, rev, final, n_heads):
    q_ref, z_ref, v_ref, lb_ref, tri_ref, cmask_ref, s0_ref = refs[:7]
    pos = 7
    if final:
        oprev_ref, g_ref, gain_ref = refs[7:10]
        pos = 10
    o_ref, sfin_ref, st_scr = refs[pos:pos + 3]
    n_chunks = HGRN_BLOCK // HGRN_CHUNK
    ref_row = HGRN_CHUNK - 1 - HGRN_CHUNK // 2 if rev else HGRN_CHUNK // 2
    end_row = 0 if rev else HGRN_CHUNK - 1

    @pl.when(pl.program_id(2) == 0)
    def _():
        st_scr[...] = s0_ref[0]

    gw = n_heads * HEAD_DIM
    tri = tri_ref[...]
    in_chunk = tri > 0
    z = z_ref[...]
    lb = lb_ref[...]
    sig = jax.nn.sigmoid(z)
    log_f = jnp.log(lb + (1.0 - lb) * sig)
    k = (1.0 - lb) * (1.0 - sig)
    q = jax.nn.silu(q_ref[...])
    v = v_ref[...]
    g1 = log_f.astype(BF16)
    r1 = log_f - g1.astype(F32)
    g2 = r1.astype(BF16)
    g3 = (r1 - g2.astype(F32)).astype(BF16)
    cum = (jnp.dot(tri, g1, preferred_element_type=F32) + jnp.dot(tri, g2, preferred_element_type=F32)
           + jnp.dot(tri, g3, preferred_element_type=F32))
    shape3 = (n_chunks, HGRN_CHUNK, gw)
    cum3 = cum.reshape(shape3)
    ref3 = cum3[:, ref_row:ref_row + 1, :]
    end3 = cum3[:, end_row:end_row + 1, :]
    rel = cum3 - ref3
    qa3 = q.reshape(shape3) * jnp.exp(rel)
    ka3 = k.reshape(shape3) * jnp.exp(-rel)
    qa = qa3.reshape(cum.shape).astype(BF16)
    ka = ka3.reshape(cum.shape).astype(BF16)
    qe = (qa3 * jnp.exp(ref3)).reshape(cum.shape).astype(BF16)
    kd = (ka3 * jnp.exp(end3 - ref3)).reshape(cum.shape).astype(BF16)
    dec = jnp.exp(end3)
    vb = v.astype(BF16)
    chunk_cols = cmask_ref[...]

    heads = []
    for h in range(n_heads):
        cols = slice(h * HEAD_DIM, (h + 1) * HEAD_DIM)
        attn = lax.dot_general(qa[:, cols], ka[:, cols], (((1,), (1,)), ((), ())), preferred_element_type=F32)
        attn = jnp.where(in_chunk, attn, 0.0)
        o_intra = jnp.dot(attn.astype(BF16), vb[:, cols], preferred_element_type=F32)
        kd_wide = jnp.concatenate([kd[:, cols]] * n_chunks, axis=1) * chunk_cols
        upd = jnp.dot(v[:, cols].T.astype(BF16), kd_wide, preferred_element_type=F32)
        heads.append(dict(o_intra=o_intra, upd=upd, state=st_scr[h]))

    order = range(n_chunks - 1, -1, -1) if rev else range(n_chunks)
    for c in order:
        rows = slice(c * HGRN_CHUNK, (c + 1) * HGRN_CHUNK)
        for h, hd in enumerate(heads):
            cols = slice(h * HEAD_DIM, (h + 1) * HEAD_DIM)
            inter = lax.dot_general(qe[rows, cols], hd["state"].astype(BF16), (((1,), (1,)), ((), ())),
                                    preferred_element_type=F32)
            o = hd["o_intra"][rows] + inter
            hd["state"] = hd["state"] * dec[c][:, cols] + hd["upd"][:, c * HEAD_DIM:(c + 1) * HEAD_DIM]
            if final:
                o = o + oprev_ref[rows, cols]
                y = o * lax.rsqrt(jnp.mean(o * o, axis=-1, keepdims=True) + EPS) * gain_ref[...]
                o = y * jax.nn.silu(g_ref[rows, cols])
            o_ref[rows, cols] = o.astype(o_ref.dtype)
    for h, hd in enumerate(heads):
        st_scr[h] = hd["state"]
        sfin_ref[0, h] = hd["state"]


def _hgrn_scan(p, lb_row, s0, batch, cols, rev, o_prev=None, out_gain=None):
    t = p.shape[0] // batch
    nsb = t // HGRN_BLOCK
    n_heads = s0.shape[1]
    hps = next(g for g in (HGRN_HEADS_PER_STEP, 2, 1)
               if n_heads % g == 0 and all(c % (g * HEAD_DIM) == 0 for c in cols))
    gw = hps * HEAD_DIM
    final = o_prev is not None
    q0, z0, v0, g0 = (c // gw for c in cols)

    def row_blk(b, s):
        return b * nsb + (nsb - 1 - s if rev else s)

    blk = (HGRN_BLOCK, gw)
    st_blk = (1, hps, HEAD_DIM, HEAD_DIM)
    in_specs = [
        pl.BlockSpec(blk, lambda b, h, s: (row_blk(b, s), q0 + h)),
        pl.BlockSpec(blk, lambda b, h, s: (row_blk(b, s), z0 + h)),
        pl.BlockSpec(blk, lambda b, h, s: (row_blk(b, s), v0 + h)),
        pl.BlockSpec((1, gw), lambda b, h, s: (0, h)),
        pl.BlockSpec((HGRN_BLOCK, HGRN_BLOCK), lambda b, h, s: (0, 0)),
        pl.BlockSpec((HGRN_BLOCK, (HGRN_BLOCK // HGRN_CHUNK) * HEAD_DIM), lambda b, h, s: (0, 0)),
        pl.BlockSpec(st_blk, lambda b, h, s: (b, h, 0, 0)),
    ]
    args = [p, p, p, lb_row, _chunk_tri(rev), _chunk_cols(), s0]
    if final:
        in_specs += [
            pl.BlockSpec(blk, lambda b, h, s: (row_blk(b, s), h)),
            pl.BlockSpec(blk, lambda b, h, s: (row_blk(b, s), g0 + h)),
            pl.BlockSpec((1, HEAD_DIM), lambda b, h, s: (0, 0)),
        ]
        args += [o_prev, p, out_gain.reshape(1, HEAD_DIM)]
    return pl.pallas_call(
        functools.partial(_hgrn_kernel, rev=rev, final=final, n_heads=hps),
        grid=(batch, n_heads // hps, nsb),
        in_specs=in_specs,
        out_specs=[
            pl.BlockSpec(blk, lambda b, h, s: (row_blk(b, s), h)),
            pl.BlockSpec(st_blk, lambda b, h, s: (b, h, 0, 0)),
        ],
        out_shape=[
            jax.ShapeDtypeStruct((batch * t, n_heads * HEAD_DIM), BF16 if final else F32),
            jax.ShapeDtypeStruct(s0.shape, F32),
        ],
        scratch_shapes=[pltpu.VMEM((hps, HEAD_DIM, HEAD_DIM), F32)],
        compiler_params=_params("parallel", "parallel", "arbitrary"),
        name="hgrn_bwd" if rev else "hgrn_fwd",
    )(*args)


def _conv_kernel(b_ref, c_ref, u_ref, w_ref, o_ref):
    u = c_ref[...] * u_ref[...]
    t = u.shape[0]
    row = lax.broadcasted_iota(jnp.int32, u.shape, 0)
    prev = jnp.where(row == 0, 0.0, pltpu.roll(u, 1, 0))
    nxt = jnp.where(row == t - 1, 0.0, pltpu.roll(u, t - 1, 0))
    y = prev * w_ref[0:1, :] + u * w_ref[1:2, :] + nxt * w_ref[2:3, :]
    o_ref[...] = (b_ref[...] * y).astype(o_ref.dtype)


def _gated_conv(p, conv_w, batch):
    d = conv_w.shape[1]
    t = p.shape[0] // batch
    td = _pick(d, (256, 128))
    nd = d // td
    return pl.pallas_call(
        _conv_kernel,
        grid=(batch, nd),
        in_specs=[
            pl.BlockSpec((t, td), lambda b, j: (b, j)),
            pl.BlockSpec((t, td), lambda b, j: (b, nd + j)),
            pl.BlockSpec((t, td), lambda b, j: (b, 2 * nd + j)),
            pl.BlockSpec((conv_w.shape[0], td), lambda b, j: (0, j)),
        ],
        out_specs=pl.BlockSpec((t, td), lambda b, j: (b, j)),
        out_shape=jax.ShapeDtypeStruct((batch * t, d), BF16),
        compiler_params=_params("parallel", "parallel"),
        name="gated_conv3",
    )(p, p, p, conv_w)


def kernel(x, c, ctx, c_ctx, mod_w, mod_b, norm_g, ffn_w1, ffn_w2, ab_w_in, ab_w_out, attn_q_gain, attn_k_gain, hgrn_lb_logits, hgrn_out_gain, conv_w_in, conv_w, conv_w_out):
    batch, seq, d = x.shape
    ctx_len = ctx.shape[1]
    depth = mod_w.shape[0]
    attn_w = d // 2
    kv_w = attn_w // KV_GROUP
    hg_w = d - attn_w
    n_hg_heads = hg_w // HEAD_DIM
    col_q, col_k, col_v = 0, attn_w, attn_w + kv_w
    col_qb = attn_w + 2 * kv_w
    col_zf, col_zb, col_ib, col_gb = (col_qb + hg_w * i for i in range(1, 5))

    lb_table = jnp.cumsum(jax.nn.softmax(hgrn_lb_logits.astype(F32), axis=1), axis=1)
    tables = _rope_tables(seq)

    pad = (-(batch + 1)) % 8
    cc = jnp.concatenate([c, c_ctx[None], jnp.zeros((pad, d), F32)], axis=0)
    mod_all = _mod_all(cc, mod_w, mod_b).reshape(depth, batch + 1 + pad, N_MOD, d)

    xl = x.reshape(batch * seq, d)
    xc = ctx.reshape(batch * ctx_len, d)
    n_ctx = batch * ctx_len

    n_lat = batch * seq
    f_ff = ffn_w2.shape[2]
    tn_up, tn_down = _up_tn(n_lat, d, f_ff), _down_tn(seq, f_ff, d)
    needs = {}
    for layer in range(depth):
        mix_in, mix_out = (ab_w_in, ab_w_out) if layer % 2 == 0 else (conv_w_in, conv_w_out)
        needs[("w1", layer, 0)] = (ffn_w1, (layer, 0), tn_up)
        needs[("w2", layer, 0)] = (ffn_w2, (layer, 0), tn_down)
        needs[("in", layer)] = (mix_in, (layer // 2,), _proj_tn(n_lat, d, mix_in.shape[2]))
        needs[("out", layer)] = (mix_out, (layer // 2,), _down_tn(seq, mix_out.shape[1], d))
        needs[("w1", layer, 1)] = (ffn_w1, (layer, 1), tn_up)
        needs[("w2", layer, 1)] = (ffn_w2, (layer, 1), tn_down)
    ready = {}

    def weight(key):
        if key not in ready:
            ready[key] = _weight_bf16(*needs[key])
        return ready[key]

    def hosted(call, w_key, tiles=1):
        w = weight(w_key)
        steps = (n_lat // _row_tile(seq)) * (w.shape[0] // tiles)
        budget = HOST_CAST_FRACTION * steps * tiles * w.shape[1] * w.shape[2] * 2
        keys = []
        for key, (src, _, _) in needs.items():
            size = src.shape[-2] * src.shape[-1] * 4
            if key not in ready and size <= budget:
                keys.append(key)
                budget -= size
        out, casts = call([needs[k] for k in keys])
        ready.update(zip(keys, casts))
        return out

    def ffn(xs, mod, gain, layer, idx, rows3, gate_row, rpm, latent):
        h = _adaln(xs, mod, gain, rows3, rpm)
        k1, k2 = ("w1", layer, idx), ("w2", layer, idx)
        if latent:
            a = hosted(lambda jobs: _ffn_up(h, weight(k1), jobs), k1, tiles=2)
            return hosted(lambda jobs: _down([a], weight(k2), xs, mod, gate_row, 0.5, rpm, "ffn_down", jobs), k2)
        a, _ = _ffn_up(h, weight(k1))
        return _down([a], weight(k2), xs, mod, gate_row, 0.5, rpm, "ffn_down")[0]

    for layer in range(depth):
        last = layer == depth - 1
        even = layer % 2 == 0
        ctx_needed = even or not last
        gain = norm_g[layer]
        mod_l = mod_all[layer, :batch]
        mod_c = mod_all[layer, batch:batch + 1]

        xl = ffn(xl, mod_l, gain, layer, 0, (0, 0, 1), 2, seq, True)
        if ctx_needed:
            xc = ffn(xc, mod_c, gain, layer, 0, (0, 0, 1), 2, n_ctx, False)

        k_in, k_out = ("in", layer), ("out", layer)
        hl = _adaln(xl, mod_l, gain, (1, 3, 4), seq)
        if even:
            e = layer // 2
            pl_ = hosted(lambda jobs: _proj(hl, weight(k_in), "ab_proj", jobs), k_in)
            pc_, _ = _proj(_adaln(xc, mod_c, gain, (1, 3, 4), n_ctx), weight(k_in), "ab_proj")
            o_attn = _attention(pl_, [pl_, pc_], attn_q_gain[e], attn_k_gain[e], tables, batch, col_q, col_k, col_v)
            s_zero = jnp.zeros((batch, n_hg_heads, HEAD_DIM, HEAD_DIM), F32)
            lbs = [lb_table[dd, layer].reshape(1, hg_w) for dd in range(2)]
            zcols = (col_zf, col_zb)
            sc_dir = []
            o_prev = None
            for dd in range(2):
                o_prev, s_c = _hgrn_scan(pc_, lbs[dd], s_zero, batch, (col_qb, zcols[dd], col_ib, col_gb), dd == 1,
                                         o_prev, hgrn_out_gain[e] if dd == 1 else None)
                sc_dir.append(s_c)
            o_rec_c = o_prev
            o_prev = None
            for dd in range(2):
                o_prev, _ = _hgrn_scan(pl_, lbs[dd], sc_dir[dd], batch, (col_qb, zcols[dd], col_ib, col_gb), dd == 1,
                                       o_prev, hgrn_out_gain[e] if dd == 1 else None)
            mixed = [o_attn, o_prev]
            if not last:
                o_attn_c = _attention(pc_, [pc_], attn_q_gain[e], attn_k_gain[e], None, batch, col_q, col_k, col_v)
                mixed_c = [o_attn_c, o_rec_c]
        else:
            o = layer // 2
            pl_ = hosted(lambda jobs: _proj(hl, weight(k_in), "conv_proj", jobs), k_in)
            mixed = [_gated_conv(pl_, conv_w[o], batch)]
            if not last:
                pc_, _ = _proj(_adaln(xc, mod_c, gain, (1, 3, 4), n_ctx), weight(k_in), "conv_proj")
                mixed_c = [_gated_conv(pc_, conv_w[o], batch)]
        x_res = xl
        xl = hosted(lambda jobs: _down(mixed, weight(k_out), x_res, mod_l, 5, 1.0, seq, "mixer_out", jobs), k_out)
        if not last:
            xc = _down(mixed_c, weight(k_out), xc, mod_c, 5, 1.0, n_ctx, "mixer_out")[0]

        xl = ffn(xl, mod_l, gain, layer, 1, (2, 6, 7), 8, seq, True)
        if not last:
            xc = ffn(xc, mod_c, gain, layer, 1, (2, 6, 7), 8, n_ctx, False)

    return xl.reshape(batch, seq, d)
```

```python
import functools

import numpy as np
import jax
import jax.numpy as jnp
from jax import lax
from jax.experimental import pallas as pl
from jax.experimental.pallas import tpu as pltpu

GRID_W = 64
HEAD_DIM = 128
KV_GROUP = 4
ROPE_THETA = 10000.0
ROPE_AXIS_DIM = HEAD_DIM // 2
HGRN_CHUNK = 32
N_MOD = 9
EPS = 1e-6
LOG2_E = 1.4426950408889634

V7X_VMEM_BYTES = 64 * 1024 * 1024
VMEM_LIMIT_BYTES = V7X_VMEM_BYTES * 7 // 8
MATMUL_VMEM_BUDGET = 46 * 1024 * 1024
F32_SUBLANES = 8
BF16_BYTES, F32_BYTES = 2, 4
ADALN_UNROLL = 16
MATMUL_ROWS = 1024
W_RING = 3
ATTN_Q_ROWS = 256
ATTN_ONES_ROWS = 16
HGRN_BLOCK = 256
HGRN_HEADS_PER_STEP = 8
CAST_BLOCK_BYTES = 8 * 1024 * 1024
CAST_SLAB_ROWS = 16
HOST_CAST_FRACTION = 0.2

BF16 = jnp.bfloat16
F32 = jnp.float32


def _params(*sem):
    return pltpu.CompilerParams(dimension_semantics=sem, vmem_limit_bytes=VMEM_LIMIT_BYTES)


def _pick(n, candidates):
    for c in candidates:
        if n % c == 0:
            return c
    return n


def _mod_kernel(c_ref, w_ref, b_ref, o_ref):
    a = jax.nn.silu(c_ref[...]).astype(BF16)
    o_ref[0] = jnp.dot(a, w_ref[0].astype(BF16), preferred_element_type=F32) + b_ref[0]


def _mod_all(cc, mod_w, mod_b):
    depth, d, n = mod_w.shape
    rows = cc.shape[0]
    tn = _pick(n, (512, 256, 128))
    return pl.pallas_call(
        _mod_kernel,
        grid=(depth, n // tn),
        in_specs=[
            pl.BlockSpec((rows, d), lambda l, j: (0, 0)),
            pl.BlockSpec((1, d, tn), lambda l, j: (l, 0, j)),
            pl.BlockSpec((1, 1, tn), lambda l, j: (l, 0, j)),
        ],
        out_specs=pl.BlockSpec((1, rows, tn), lambda l, j: (l, 0, j)),
        out_shape=jax.ShapeDtypeStruct((depth, rows, n), F32),
        compiler_params=_params("parallel", "parallel"),
        name="mod_proj",
    )(cc, mod_w, mod_b.reshape(depth, 1, n))


def _cast_rows(src_ref, dst_ref):
    tn = dst_ref.shape[2]
    for t in range(dst_ref.shape[0]):
        dst_ref[t] = src_ref[:, t * tn:(t + 1) * tn].astype(dst_ref.dtype)


def _weight_bf16(w, lead, tn):
    r, c = w.shape[-2:]
    tr = _pick(r, tuple(t for t in (2048, 1024, 512, 256, 128, 64, 32, 16) if t * c * F32_BYTES <= CAST_BLOCK_BYTES))
    return pl.pallas_call(
        _cast_rows,
        grid=(r // tr,),
        in_specs=[pl.BlockSpec((None,) * len(lead) + (tr, c), lambda i: tuple(lead) + (i, 0))],
        out_specs=pl.BlockSpec((c // tn, tr, tn), lambda i: (0, i, 0)),
        out_shape=jax.ShapeDtypeStruct((c // tn, r, tn), BF16),
        compiler_params=_params("parallel"),
        name="weight_cast",
    )(w)


def _call_with_casts(kernel_fn, grid, in_specs, out_specs, out_shape, args, jobs, name, scratch_shapes=()):
    n_in, n_out, n_jobs = len(in_specs), len(out_specs), len(jobs)
    steps, nj = grid[0] * grid[1], grid[1]
    in_specs, out_specs, out_shape, args = list(in_specs), list(out_specs), list(out_shape), list(args)
    for w, lead, tn in jobs:
        r, c = w.shape[-2:]
        slab = next(t for t in range(CAST_SLAB_ROWS, r + 1, CAST_SLAB_ROWS) if r % t == 0 and r // t <= steps)

        def slab_idx(i, j, last=r // slab - 1):
            return jnp.minimum(i * nj + j, last)

        in_specs.append(pl.BlockSpec((None,) * len(lead) + (slab, c),
                                     lambda i, j, lead=tuple(lead), f=slab_idx: lead + (f(i, j), 0)))
        out_specs.append(pl.BlockSpec((c // tn, slab, tn), lambda i, j, f=slab_idx: (0, f(i, j), 0)))
        out_shape.append(jax.ShapeDtypeStruct((c // tn, r, tn), BF16))
        args.append(w)

    def body(*refs):
        ins, srcs = refs[:n_in], refs[n_in:n_in + n_jobs]
        outs = refs[n_in + n_jobs:n_in + n_jobs + n_out]
        dsts = refs[n_in + n_jobs + n_out:n_in + 2 * n_jobs + n_out]
        for src, dst in zip(srcs, dsts):
            _cast_rows(src, dst)
        kernel_fn(*ins, *outs, *refs[n_in + 2 * n_jobs + n_out:])

    res = pl.pallas_call(
        body,
        grid=grid,
        in_specs=in_specs,
        out_specs=out_specs,
        out_shape=out_shape,
        scratch_shapes=list(scratch_shapes),
        compiler_params=_params("arbitrary" if jobs or scratch_shapes else "parallel", "arbitrary"),
        name=name,
    )(*args)
    return res[:n_out], res[n_out:]


def _adaln_kernel(x_ref, mod_ref, gain_ref, o_ref, *, rows3):
    gain_row, shift_row, scale_row = rows3
    gs = gain_ref[gain_row:gain_row + 1, :] * (1.0 + mod_ref[0, scale_row:scale_row + 1, :])
    shift = mod_ref[0, shift_row:shift_row + 1, :]

    def body(r, carry):
        sl = pl.ds(pl.multiple_of(r * F32_SUBLANES, F32_SUBLANES), F32_SUBLANES)
        x = x_ref[sl, :]
        y = x * lax.rsqrt(jnp.mean(x * x, axis=-1, keepdims=True) + EPS)
        o_ref[sl, :] = (y * gs + shift).astype(o_ref.dtype)
        return carry

    lax.fori_loop(0, x_ref.shape[0] // F32_SUBLANES, body, 0, unroll=ADALN_UNROLL)


def _adaln(x2d, mod, gain, rows3, rows_per_mod):
    m, d = x2d.shape
    tr = _pick(rows_per_mod, (512, 256, 128, 64, 32, 16, 8))
    bpm = rows_per_mod // tr
    return pl.pallas_call(
        functools.partial(_adaln_kernel, rows3=rows3),
        grid=(m // tr,),
        in_specs=[
            pl.BlockSpec((tr, d), lambda i: (i, 0)),
            pl.BlockSpec((1, N_MOD, d), lambda i: (i // bpm, 0, 0)),
            pl.BlockSpec(gain.shape, lambda i: (0, 0)),
        ],
        out_specs=pl.BlockSpec((tr, d), lambda i: (i, 0)),
        out_shape=jax.ShapeDtypeStruct((m, d), BF16),
        compiler_params=_params("parallel"),
        name="adaln",
    )(x2d, mod, gain)


def _row_tile(rows):
    return _pick(rows, (MATMUL_ROWS, 512, 256, 128, 64, 32, 16, 8))


def _col_tile(tm, n, a_bytes_per_row, w_bytes_per_col, io_bytes_per_elem):
    for tn in (1024, 512, 256, 128):
        vmem = 2 * (tm * a_bytes_per_row + tn * w_bytes_per_col + tm * tn * io_bytes_per_elem)
        if n % tn == 0 and vmem <= MATMUL_VMEM_BUDGET:
            return tn
    return 128


def _up_tn(rows, d, f):
    return _col_tile(_row_tile(rows), f, BF16_BYTES * d, 2 * BF16_BYTES * d, BF16_BYTES)


def _proj_tn(rows, d, n):
    return _col_tile(_row_tile(rows), n, BF16_BYTES * d, BF16_BYTES * d, F32_BYTES)


def _down_tn(rows, k, d):
    return _col_tile(_row_tile(rows), d, BF16_BYTES * k, BF16_BYTES * k, 2 * F32_BYTES)


def _up_kernel(h_ref, wg_ref, wu_ref, o_ref):
    h = h_ref[...]
    g = jnp.dot(h, wg_ref[...], preferred_element_type=F32)
    u = jnp.dot(h, wu_ref[...], preferred_element_type=F32)
    o_ref[...] = (jax.nn.silu(g) * u).astype(o_ref.dtype)


def _proj_kernel(h_ref, w_ref, o_ref):
    o_ref[...] = jnp.dot(h_ref[...], w_ref[...], preferred_element_type=F32).astype(o_ref.dtype)


def _ffn_up(h, w1, jobs=()):
    m, d = h.shape
    tn = w1.shape[2]
    nj = w1.shape[0] // 2
    tm = _row_tile(m)
    (a,), casts = _call_with_casts(
        _up_kernel,
        (m // tm, nj),
        [
            pl.BlockSpec((tm, d), lambda i, j: (i, 0)),
            pl.BlockSpec((None, d, tn), lambda i, j: (j, 0, 0)),
            pl.BlockSpec((None, d, tn), lambda i, j: (j + nj, 0, 0)),
        ],
        [pl.BlockSpec((tm, tn), lambda i, j: (i, j))],
        [jax.ShapeDtypeStruct((m, nj * tn), BF16)],
        [h, w1, w1], jobs, "ffn_up")
    return a, casts


def _proj(h, w, name, jobs=()):
    m, d = h.shape
    nj, _, tn = w.shape
    tm = _row_tile(m)
    (p,), casts = _call_with_casts(
        _proj_kernel,
        (m // tm, nj),
        [
            pl.BlockSpec((tm, d), lambda i, j: (i, 0)),
            pl.BlockSpec((None, d, tn), lambda i, j: (j, 0, 0)),
        ],
        [pl.BlockSpec((tm, tn), lambda i, j: (i, j))],
        [jax.ShapeDtypeStruct((m, nj * tn), F32)],
        [h, w], jobs, name)
    return p, casts


def _down_kernel(*refs, n_a, coef, steps):
    a_refs = refs[:n_a]
    w_hbm, x_hbm, gate_ref, o_hbm, w_ring, x_ring, o_ring, w_sem, x_sem, o_sem = refs[n_a:]
    nj = pl.num_programs(1)
    step = pl.program_id(0) * nj + pl.program_id(1)
    kk = a_refs[0].shape[1]
    _, tm, tn = x_ring.shape

    def tile(s):
        return (pl.ds(pl.multiple_of((s // nj) * tm, tm), tm), pl.ds(pl.multiple_of((s % nj) * tn, tn), tn))

    def copies(s):
        slot = s % W_RING
        return (pltpu.make_async_copy(w_hbm.at[s % nj], w_ring.at[slot], w_sem.at[slot]),
                pltpu.make_async_copy(x_hbm.at[tile(s)], x_ring.at[slot], x_sem.at[slot]))

    def out_copy(s):
        slot = s % W_RING
        return pltpu.make_async_copy(o_ring.at[slot], o_hbm.at[tile(s)], o_sem.at[slot])

    @pl.when(step == 0)
    def _():
        for s in range(min(W_RING - 1, steps)):
            for c in copies(s):
                c.start()

    @pl.when(step + (W_RING - 1) < steps)
    def _():
        for c in copies(step + (W_RING - 1)):
            c.start()

    for c in copies(step):
        c.wait()
    w_tile = w_ring.at[step % W_RING]
    acc = jnp.dot(a_refs[0][...], w_tile[0:kk, :], preferred_element_type=F32)
    for r in range(1, n_a):
        acc = acc + jnp.dot(a_refs[r][...], w_tile[r * kk:(r + 1) * kk, :], preferred_element_type=F32)
    gate = gate_ref[0, pl.program_id(1)]
    if coef != 1.0:
        gate = coef * gate
    o_ring[step % W_RING] = x_ring[step % W_RING] + gate * acc
    out_copy(step).start()

    @pl.when(step >= W_RING - 1)
    def _():
        out_copy(step - (W_RING - 1)).wait()

    @pl.when(step == steps - 1)
    def _():
        for back in range(min(W_RING - 1, steps) - 1, -1, -1):
            out_copy(step - back).wait()


def _down(a_list, w, x2d, mod, gate_row, coef, rows_per_mod, name, jobs=()):
    m, d = x2d.shape
    n_a = len(a_list)
    kk = a_list[0].shape[1]
    nj, _, tn = w.shape
    tm = _row_tile(rows_per_mod)
    bpm = rows_per_mod // tm
    gate = mod[:, gate_row].reshape(mod.shape[0], nj, 1, tn)
    in_specs = [pl.BlockSpec((tm, kk), lambda i, j: (i, 0)) for _ in a_list]
    in_specs += [
        pl.BlockSpec(memory_space=pl.ANY),
        pl.BlockSpec(memory_space=pl.ANY),
        pl.BlockSpec((1, nj, 1, tn), lambda i, j: (i // bpm, 0, 0, 0)),
    ]
    grid = (m // tm, nj)
    (y,), casts = _call_with_casts(
        functools.partial(_down_kernel, n_a=n_a, coef=coef, steps=grid[0] * grid[1]),
        grid,
        in_specs,
        [pl.BlockSpec(memory_space=pl.ANY)],
        [jax.ShapeDtypeStruct((m, d), F32)],
        [*a_list, w, x2d, gate], jobs, name,
        scratch_shapes=[pltpu.VMEM((W_RING,) + w.shape[1:], BF16), pltpu.VMEM((W_RING, tm, tn), F32),
                        pltpu.VMEM((W_RING, tm, tn), F32)] + [pltpu.SemaphoreType.DMA((W_RING,))] * 3)
    return y, casts


def _head_norm(x, gain):
    return x * lax.rsqrt(jnp.mean(x * x, axis=-1, keepdims=True) + EPS) * gain


def _rope(x, c, s_hi, s_lo):
    return x * c + pltpu.roll(x, HEAD_DIM - ROPE_AXIS_DIM // 2, 1) * s_hi + pltpu.roll(x, ROPE_AXIS_DIM // 2, 1) * s_lo


def _attn_kernel(*refs, kv_rows, rope):
    n_kv = len(kv_rows)
    q_ref = refs[0]
    kv_refs = refs[1:1 + 2 * n_kv]
    qg_ref, kg_ref = refs[1 + 2 * n_kv:3 + 2 * n_kv]
    pos = 3 + 2 * n_kv
    if rope:
        cq_ref, hq_ref, lq_ref, ck_ref, hk_ref, lk_ref = refs[pos:pos + 6]
        pos += 6
    o_ref, k_scr, vt_scr = refs[pos:pos + 3]
    tq = q_ref.shape[0]

    @pl.when(pl.program_id(2) == 0)
    def _():
        off = 0
        for part, rows in enumerate(kv_rows):
            kn = _head_norm(kv_refs[2 * part][...], kg_ref[...])
            if rope and part == 0:
                kn = _rope(kn, ck_ref[...], hk_ref[...], lk_ref[...])
            k_scr[off:off + rows, :] = kn.astype(BF16)
            vt_scr[:HEAD_DIM, off:off + rows] = kv_refs[2 * part + 1][...].T.astype(BF16)
            off += rows
        vt_scr[HEAD_DIM:, :] = jnp.ones((vt_scr.shape[0] - HEAD_DIM, off), BF16)

    scale = HEAD_DIM ** -0.5 * LOG2_E
    qs = []
    for g in range(KV_GROUP):
        q = _head_norm(q_ref[:, g * HEAD_DIM:(g + 1) * HEAD_DIM], qg_ref[...])
        if rope:
            q = _rope(q, cq_ref[...], hq_ref[...], lq_ref[...])
        qs.append((q * scale).astype(BF16))
    q_all = jnp.concatenate(qs, axis=0)
    s_t = lax.dot_general(k_scr[...], q_all, (((1,), (1,)), ((), ())), preferred_element_type=F32)
    p_t = jnp.exp2(s_t - jnp.max(s_t, axis=0, keepdims=True)).astype(BF16)
    o_t = jnp.dot(vt_scr[...], p_t, preferred_element_type=F32)
    o = (o_t[:HEAD_DIM] / o_t[HEAD_DIM:HEAD_DIM + 1]).T
    for g in range(KV_GROUP):
        o_ref[:, g * HEAD_DIM:(g + 1) * HEAD_DIM] = o[g * tq:(g + 1) * tq].astype(o_ref.dtype)


def _attention(pq, kv_sources, q_gain, k_gain, tables, batch, q_col0, k_col0, v_col0):
    t = pq.shape[0] // batch
    n_kv_heads = (k_col0 - q_col0) // (KV_GROUP * HEAD_DIM)
    tq = _pick(t, (ATTN_Q_ROWS, 256, 128, 64, 32, 16, 8))
    nq = t // tq
    gw = KV_GROUP * HEAD_DIM
    kv_rows = tuple(src.shape[0] // batch for src in kv_sources)
    rope = tables is not None
    in_specs = [pl.BlockSpec((tq, gw), lambda b, h, i: (b * nq + i, q_col0 // gw + h))]
    args = [pq]
    for src, rows in zip(kv_sources, kv_rows):
        in_specs.append(pl.BlockSpec((rows, HEAD_DIM), lambda b, h, i: (b, k_col0 // HEAD_DIM + h)))
        in_specs.append(pl.BlockSpec((rows, HEAD_DIM), lambda b, h, i: (b, v_col0 // HEAD_DIM + h)))
        args += [src, src]
    in_specs += [pl.BlockSpec((1, HEAD_DIM), lambda b, h, i: (0, 0))] * 2
    args += [q_gain.reshape(1, HEAD_DIM), k_gain.reshape(1, HEAD_DIM)]
    if rope:
        in_specs += [pl.BlockSpec((tq, HEAD_DIM), lambda b, h, i: (i, 0))] * 3
        in_specs += [pl.BlockSpec((t, HEAD_DIM), lambda b, h, i: (0, 0))] * 3
        args += list(tables) * 2
    return pl.pallas_call(
        functools.partial(_attn_kernel, kv_rows=kv_rows, rope=rope),
        grid=(batch, n_kv_heads, nq),
        in_specs=in_specs,
        out_specs=pl.BlockSpec((tq, gw), lambda b, h, i: (b * nq + i, h)),
        out_shape=jax.ShapeDtypeStruct((batch * t, n_kv_heads * gw), BF16),
        scratch_shapes=[pltpu.VMEM((sum(kv_rows), HEAD_DIM), BF16),
                        pltpu.VMEM((HEAD_DIM + ATTN_ONES_ROWS, sum(kv_rows)), BF16)],
        compiler_params=_params("parallel", "parallel", "arbitrary"),
        name="gqa_rope" if rope else "gqa_ctx",
    )(*args)


def _rope_tables(t):
    rows = t // GRID_W
    row_ids = np.repeat(np.arange(rows, dtype=np.float32), GRID_W)
    col_ids = np.tile(np.arange(GRID_W, dtype=np.float32), rows)
    inv_freq = jnp.asarray(ROPE_THETA, F32) ** (-jnp.arange(0, ROPE_AXIS_DIM, 2, dtype=F32) / ROPE_AXIS_DIM)
    ang_r = jnp.asarray(row_ids)[:, None] * inv_freq
    ang_c = jnp.asarray(col_ids)[:, None] * inv_freq
    zero = jnp.zeros_like(ang_r)
    cos = jnp.concatenate([jnp.cos(ang_r), jnp.cos(ang_r), jnp.cos(ang_c), jnp.cos(ang_c)], axis=1)
    s_hi = jnp.concatenate([-jnp.sin(ang_r), zero, -jnp.sin(ang_c), zero], axis=1)
    s_lo = jnp.concatenate([zero, jnp.sin(ang_r), zero, jnp.sin(ang_c)], axis=1)
    return cos, s_hi, s_lo


def _chunk_tri(rev):
    idx = np.arange(HGRN_BLOCK)
    same = (idx[:, None] // HGRN_CHUNK) == (idx[None, :] // HGRN_CHUNK)
    tri = (idx[None, :] >= idx[:, None]) if rev else (idx[None, :] <= idx[:, None])
    return jnp.asarray((same & tri).astype(np.float32), BF16)


def _chunk_cols():
    idx = np.arange(HGRN_BLOCK)
    col = np.arange((HGRN_BLOCK // HGRN_CHUNK) * HEAD_DIM)
    return jnp.asarray(((idx[:, None] // HGRN_CHUNK) == (col[None, :] // HEAD_DIM)).astype(np.float32), BF16)


def _hgrn_kernel(*refs, rev, final, n_heads):
    q_ref, z_ref, v_ref, lb_ref, tri_ref, cmask_ref, s0_ref = refs[:7]
    pos = 7
    if final:
        oprev_ref, g_ref, gain_ref = refs[7:10]
        pos = 10
    o_ref, sfin_ref, st_scr = refs[pos:pos + 3]
    n_chunks = HGRN_BLOCK // HGRN_CHUNK
    ref_row = HGRN_CHUNK - 1 - HGRN_CHUNK // 2 if rev else HGRN_CHUNK // 2
    end_row = 0 if rev else HGRN_CHUNK - 1

    @pl.when(pl.program_id(2) == 0)
    def _():
        st_scr[...] = s0_ref[0]

    gw = n_heads * HEAD_DIM
    tri = tri_ref[...]
    in_chunk = tri > 0
    z = z_ref[...]
    lb = lb_ref[...]
    sig = jax.nn.sigmoid(z)
    log_f = jnp.log(lb + (1.0 - lb) * sig)
    k = (1.0 - lb) * (1.0 - sig)
    q = jax.nn.silu(q_ref[...])
    v = v_ref[...]
    g1 = log_f.astype(BF16)
    r1 = log_f - g1.astype(F32)
    g2 = r1.astype(BF16)
    g3 = (r1 - g2.astype(F32)).astype(BF16)
    cum = (jnp.dot(tri, g1, preferred_element_type=F32) + jnp.dot(tri, g2, preferred_element_type=F32)
           + jnp.dot(tri, g3, preferred_element_type=F32))
    shape3 = (n_chunks, HGRN_CHUNK, gw)
    cum3 = cum.reshape(shape3)
    ref3 = cum3[:, ref_row:ref_row + 1, :]
    end3 = cum3[:, end_row:end_row + 1, :]
    rel = cum3 - ref3
    qa3 = q.reshape(shape3) * jnp.exp(rel)
    ka3 = k.reshape(shape3) * jnp.exp(-rel)
    qa = qa3.reshape(cum.shape).astype(BF16)
    ka = ka3.reshape(cum.shape).astype(BF16)
    qe = (qa3 * jnp.exp(ref3)).reshape(cum.shape).astype(BF16)
    kd = (ka3 * jnp.exp(end3 - ref3)).reshape(cum.shape).astype(BF16)
    dec = jnp.exp(end3)
    vb = v.astype(BF16)
    chunk_cols = cmask_ref[...]

    heads = []
    for h in range(n_heads):
        cols = slice(h * HEAD_DIM, (h + 1) * HEAD_DIM)
        attn = lax.dot_general(qa[:, cols], ka[:, cols], (((1,), (1,)), ((), ())), preferred_element_type=F32)
        attn = jnp.where(in_chunk, attn, 0.0)
        o_intra = jnp.dot(attn.astype(BF16), vb[:, cols], preferred_element_type=F32)
        kd_wide = jnp.concatenate([kd[:, cols]] * n_chunks, axis=1) * chunk_cols
        upd = jnp.dot(v[:, cols].T.astype(BF16), kd_wide, preferred_element_type=F32)
        heads.append(dict(o_intra=o_intra, upd=upd, state=st_scr[h]))

    order = range(n_chunks - 1, -1, -1) if rev else range(n_chunks)
    for c in order:
        rows = slice(c * HGRN_CHUNK, (c + 1) * HGRN_CHUNK)
        for h, hd in enumerate(heads):
            cols = slice(h * HEAD_DIM, (h + 1) * HEAD_DIM)
            inter = lax.dot_general(qe[rows, cols], hd["state"].astype(BF16), (((1,), (1,)), ((), ())),
                                    preferred_element_type=F32)
            o = hd["o_intra"][rows] + inter
            hd["state"] = hd["state"] * dec[c][:, cols] + hd["upd"][:, c * HEAD_DIM:(c + 1) * HEAD_DIM]
            if final:
                o = o + oprev_ref[rows, cols]
                y = o * lax.rsqrt(jnp.mean(o * o, axis=-1, keepdims=True) + EPS) * gain_ref[...]
                o = y * jax.nn.silu(g_ref[rows, cols])
            o_ref[rows, cols] = o.astype(o_ref.dtype)
    for h, hd in enumerate(heads):
        st_scr[h] = hd["state"]
        sfin_ref[0, h] = hd["state"]


def _hgrn_scan(p, lb_row, s0, batch, cols, rev, o_prev=None, out_gain=None):
    t = p.shape[0] // batch
    nsb = t // HGRN_BLOCK
    n_heads = s0.shape[1]
    hps = next(g for g in (HGRN_HEADS_PER_STEP, 4, 2, 1)
               if n_heads % g == 0 and all(c % (g * HEAD_DIM) == 0 for c in cols))
    gw = hps * HEAD_DIM
    final = o_prev is not None
    q0, z0, v0, g0 = (c // gw for c in cols)

    def row_blk(b, s):
        return b * nsb + (nsb - 1 - s if rev else s)

    blk = (HGRN_BLOCK, gw)
    st_blk = (1, hps, HEAD_DIM, HEAD_DIM)
    in_specs = [
        pl.BlockSpec(blk, lambda b, h, s: (row_blk(b, s), q0 + h)),
        pl.BlockSpec(blk, lambda b, h, s: (row_blk(b, s), z0 + h)),
        pl.BlockSpec(blk, lambda b, h, s: (row_blk(b, s), v0 + h)),
        pl.BlockSpec((1, gw), lambda b, h, s: (0, h)),
        pl.BlockSpec((HGRN_BLOCK, HGRN_BLOCK), lambda b, h, s: (0, 0)),
        pl.BlockSpec((HGRN_BLOCK, (HGRN_BLOCK // HGRN_CHUNK) * HEAD_DIM), lambda b, h, s: (0, 0)),
        pl.BlockSpec(st_blk, lambda b, h, s: (b, h, 0, 0)),
    ]
    args = [p, p, p, lb_row, _chunk_tri(rev), _chunk_cols(), s0]
    if final:
        in_specs += [
            pl.BlockSpec(blk, lambda b, h, s: (row_blk(b, s), h)),
            pl.BlockSpec(blk, lambda b, h, s: (row_blk(b, s), g0 + h)),
            pl.BlockSpec((1, HEAD_DIM), lambda b, h, s: (0, 0)),
        ]
        args += [o_prev, p, out_gain.reshape(1, HEAD_DIM)]
    return pl.pallas_call(
        functools.partial(_hgrn_kernel, rev=rev, final=final, n_heads=hps),
        grid=(batch, n_heads // hps, nsb),
        in_specs=in_specs,
        out_specs=[
            pl.BlockSpec(blk, lambda b, h, s: (row_blk(b, s), h)),
            pl.BlockSpec(st_blk, lambda b, h, s: (b, h, 0, 0)),
        ],
        out_shape=[
            jax.ShapeDtypeStruct((batch * t, n_heads * HEAD_DIM), BF16 if final else F32),
            jax.ShapeDtypeStruct(s0.shape, F32),
        ],
        scratch_shapes=[pltpu.VMEM((hps, HEAD_DIM, HEAD_DIM), F32)],
        compiler_params=_params("parallel", "parallel", "arbitrary"),
        name="hgrn_bwd" if rev else "hgrn_fwd",
    )(*args)


def _conv_kernel(b_ref, c_ref, u_ref, w_ref, o_ref):
    u = c_ref[...] * u_ref[...]
    t = u.shape[0]
    row = lax.broadcasted_iota(jnp.int32, u.shape, 0)
    prev = jnp.where(row == 0, 0.0, pltpu.roll(u, 1, 0))
    nxt = jnp.where(row == t - 1, 0.0, pltpu.roll(u, t - 1, 0))
    y = prev * w_ref[0:1, :] + u * w_ref[1:2, :] + nxt * w_ref[2:3, :]
    o_ref[...] = (b_ref[...] * y).astype(o_ref.dtype)


def _gated_conv(p, conv_w, batch):
    d = conv_w.shape[1]
    t = p.shape[0] // batch
    td = _pick(d, (256, 128))
    nd = d // td
    return pl.pallas_call(
        _conv_kernel,
        grid=(batch, nd),
        in_specs=[
            pl.BlockSpec((t, td), lambda b, j: (b, j)),
            pl.BlockSpec((t, td), lambda b, j: (b, nd + j)),
            pl.BlockSpec((t, td), lambda b, j: (b, 2 * nd + j)),
            pl.BlockSpec((conv_w.shape[0], td), lambda b, j: (0, j)),
        ],
        out_specs=pl.BlockSpec((t, td), lambda b, j: (b, j)),
        out_shape=jax.ShapeDtypeStruct((batch * t, d), BF16),
        compiler_params=_params("parallel", "parallel"),
        name="gated_conv3",
    )(p, p, p, conv_w)


def kernel(x, c, ctx, c_ctx, mod_w, mod_b, norm_g, ffn_w1, ffn_w2, ab_w_in, ab_w_out, attn_q_gain, attn_k_gain, hgrn_lb_logits, hgrn_out_gain, conv_w_in, conv_w, conv_w_out):
    batch, seq, d = x.shape
    ctx_len = ctx.shape[1]
    depth = mod_w.shape[0]
    attn_w = d // 2
    kv_w = attn_w // KV_GROUP
    hg_w = d - attn_w
    n_hg_heads = hg_w // HEAD_DIM
    col_q, col_k, col_v = 0, attn_w, attn_w + kv_w
    col_qb = attn_w + 2 * kv_w
    col_zf, col_zb, col_ib, col_gb = (col_qb + hg_w * i for i in range(1, 5))

    lb_table = jnp.cumsum(jax.nn.softmax(hgrn_lb_logits.astype(F32), axis=1), axis=1)
    tables = _rope_tables(seq)

    pad = (-(batch + 1)) % 8
    cc = jnp.concatenate([c, c_ctx[None], jnp.zeros((pad, d), F32)], axis=0)
    mod_all = _mod_all(cc, mod_w, mod_b).reshape(depth, batch + 1 + pad, N_MOD, d)

    xl = x.reshape(batch * seq, d)
    xc = ctx.reshape(batch * ctx_len, d)
    n_ctx = batch * ctx_len

    n_lat = batch * seq
    f_ff = ffn_w2.shape[2]
    tn_up, tn_down = _up_tn(n_lat, d, f_ff), _down_tn(seq, f_ff, d)
    needs = {}
    for layer in range(depth):
        mix_in, mix_out = (ab_w_in, ab_w_out) if layer % 2 == 0 else (conv_w_in, conv_w_out)
        needs[("w1", layer, 0)] = (ffn_w1, (layer, 0), tn_up)
        needs[("w2", layer, 0)] = (ffn_w2, (layer, 0), tn_down)
        needs[("in", layer)] = (mix_in, (layer // 2,), _proj_tn(n_lat, d, mix_in.shape[2]))
        needs[("out", layer)] = (mix_out, (layer // 2,), _down_tn(seq, mix_out.shape[1], d))
        needs[("w1", layer, 1)] = (ffn_w1, (layer, 1), tn_up)
        needs[("w2", layer, 1)] = (ffn_w2, (layer, 1), tn_down)
    ready = {}

    def weight(key):
        if key not in ready:
            ready[key] = _weight_bf16(*needs[key])
        return ready[key]

    def hosted(call, w_key, tiles=1):
        w = weight(w_key)
        steps = (n_lat // _row_tile(seq)) * (w.shape[0] // tiles)
        budget = HOST_CAST_FRACTION * steps * tiles * w.shape[1] * w.shape[2] * BF16_BYTES
        keys = []
        for key, (src, _, _) in needs.items():
            size = src.shape[-2] * src.shape[-1] * F32_BYTES
            if key not in ready and size <= budget:
                keys.append(key)
                budget -= size
        out, casts = call([needs[k] for k in keys])
        ready.update(zip(keys, casts))
        return out

    def ffn(xs, mod, gain, layer, idx, rows3, gate_row, rpm, latent):
        h = _adaln(xs, mod, gain, rows3, rpm)
        k1, k2 = ("w1", layer, idx), ("w2", layer, idx)
        if latent:
            a = hosted(lambda jobs: _ffn_up(h, weight(k1), jobs), k1, tiles=2)
            return hosted(lambda jobs: _down([a], weight(k2), xs, mod, gate_row, 0.5, rpm, "ffn_down", jobs), k2)
        a, _ = _ffn_up(h, weight(k1))
        return _down([a], weight(k2), xs, mod, gate_row, 0.5, rpm, "ffn_down")[0]

    for layer in range(depth):
        last = layer == depth - 1
        even = layer % 2 == 0
        ctx_needed = even or not last
        gain = norm_g[layer]
        mod_l = mod_all[layer, :batch]
        mod_c = mod_all[layer, batch:batch + 1]

        xl = ffn(xl, mod_l, gain, layer, 0, (0, 0, 1), 2, seq, True)
        if ctx_needed:
            xc = ffn(xc, mod_c, gain, layer, 0, (0, 0, 1), 2, n_ctx, False)

        k_in, k_out = ("in", layer), ("out", layer)
        hl = _adaln(xl, mod_l, gain, (1, 3, 4), seq)
        if even:
            e = layer // 2
            pl_ = hosted(lambda jobs: _proj(hl, weight(k_in), "ab_proj", jobs), k_in)
            pc_, _ = _proj(_adaln(xc, mod_c, gain, (1, 3, 4), n_ctx), weight(k_in), "ab_proj")
            o_attn = _attention(pl_, [pl_, pc_], attn_q_gain[e], attn_k_gain[e], tables, batch, col_q, col_k, col_v)
            s_zero = jnp.zeros((batch, n_hg_heads, HEAD_DIM, HEAD_DIM), F32)
            lbs = [lb_table[dd, layer].reshape(1, hg_w) for dd in range(2)]
            zcols = (col_zf, col_zb)
            sc_dir = []
            o_prev = None
            for dd in range(2):
                o_prev, s_c = _hgrn_scan(pc_, lbs[dd], s_zero, batch, (col_qb, zcols[dd], col_ib, col_gb), dd == 1,
                                         o_prev, hgrn_out_gain[e] if dd == 1 else None)
                sc_dir.append(s_c)
            o_rec_c = o_prev
            o_prev = None
            for dd in range(2):
                o_prev, _ = _hgrn_scan(pl_, lbs[dd], sc_dir[dd], batch, (col_qb, zcols[dd], col_ib, col_gb), dd == 1,
                                       o_prev, hgrn_out_gain[e] if dd == 1 else None)
            mixed = [o_attn, o_prev]
            if not last:
                o_attn_c = _attention(pc_, [pc_], attn_q_gain[e], attn_k_gain[e], None, batch, col_q, col_k, col_v)
                mixed_c = [o_attn_c, o_rec_c]
        else:
            o = layer // 2
            pl_ = hosted(lambda jobs: _proj(hl, weight(k_in), "conv_proj", jobs), k_in)
            mixed = [_gated_conv(pl_, conv_w[o], batch)]
            if not last:
                pc_, _ = _proj(_adaln(xc, mod_c, gain, (1, 3, 4), n_ctx), weight(k_in), "conv_proj")
                mixed_c = [_gated_conv(pc_, conv_w[o], batch)]
        x_res = xl
        xl = hosted(lambda jobs: _down(mixed, weight(k_out), x_res, mod_l, 5, 1.0, seq, "mixer_out", jobs), k_out)
        if not last:
            xc = _down(mixed_c, weight(k_out), xc, mod_c, 5, 1.0, n_ctx, "mixer_out")[0]

        xl = ffn(xl, mod_l, gain, layer, 1, (2, 6, 7), 8, seq, True)
        if not last:
            xc = ffn(xc, mod_c, gain, layer, 1, (2, 6, 7), 8, n_ctx, False)

    return xl.reshape(batch, seq, d)
```

```python
import functools

import numpy as np
import jax
import jax.numpy as jnp
from jax import lax
from jax.experimental import pallas as pl
from jax.experimental.pallas import tpu as pltpu

GRID_W = 64
HEAD_DIM = 128
KV_GROUP = 4
ROPE_THETA = 10000.0
ROPE_AXIS_DIM = HEAD_DIM // 2
HGRN_CHUNK = 32
N_MOD = 9
EPS = 1e-6
LOG2_E = 1.4426950408889634

V7X_VMEM_BYTES = 64 * 1024 * 1024
VMEM_LIMIT_BYTES = V7X_VMEM_BYTES * 7 // 8
MATMUL_VMEM_BUDGET = 46 * 1024 * 1024
F32_SUBLANES = 8
BF16_BYTES, F32_BYTES = 2, 4
ADALN_UNROLL = 16
MATMUL_ROWS = 1024
UP_ROWS = 2048
W_RING = 3
ATTN_Q_ROWS = 256
ATTN_ONES_ROWS = 16
HGRN_BLOCK = 256
HGRN_HEADS_PER_STEP = 8
CAST_BLOCK_BYTES = 8 * 1024 * 1024
CAST_SLAB_ROWS = 16
HOST_CAST_FRACTION = 0.2

BF16 = jnp.bfloat16
F32 = jnp.float32


def _params(*sem):
    return pltpu.CompilerParams(dimension_semantics=sem, vmem_limit_bytes=VMEM_LIMIT_BYTES)


def _pick(n, candidates):
    for c in candidates:
        if n % c == 0:
            return c
    return n


def _mod_kernel(c_ref, w_ref, b_ref, o_ref):
    a = jax.nn.silu(c_ref[...]).astype(BF16)
    o_ref[0] = jnp.dot(a, w_ref[0].astype(BF16), preferred_element_type=F32) + b_ref[0]


def _mod_all(cc, mod_w, mod_b):
    depth, d, n = mod_w.shape
    rows = cc.shape[0]
    tn = _pick(n, (512, 256, 128))
    return pl.pallas_call(
        _mod_kernel,
        grid=(depth, n // tn),
        in_specs=[
            pl.BlockSpec((rows, d), lambda l, j: (0, 0)),
            pl.BlockSpec((1, d, tn), lambda l, j: (l, 0, j)),
            pl.BlockSpec((1, 1, tn), lambda l, j: (l, 0, j)),
        ],
        out_specs=pl.BlockSpec((1, rows, tn), lambda l, j: (l, 0, j)),
        out_shape=jax.ShapeDtypeStruct((depth, rows, n), F32),
        compiler_params=_params("parallel", "parallel"),
        name="mod_proj",
    )(cc, mod_w, mod_b.reshape(depth, 1, n))


def _cast_rows(src_ref, dst_ref):
    tn = dst_ref.shape[2]
    for t in range(dst_ref.shape[0]):
        dst_ref[t] = src_ref[:, t * tn:(t + 1) * tn].astype(dst_ref.dtype)


def _weight_bf16(w, lead, tn):
    r, c = w.shape[-2:]
    tr = _pick(r, tuple(t for t in (2048, 1024, 512, 256, 128, 64, 32, 16) if t * c * F32_BYTES <= CAST_BLOCK_BYTES))
    return pl.pallas_call(
        _cast_rows,
        grid=(r // tr,),
        in_specs=[pl.BlockSpec((None,) * len(lead) + (tr, c), lambda i: tuple(lead) + (i, 0))],
        out_specs=pl.BlockSpec((c // tn, tr, tn), lambda i: (0, i, 0)),
        out_shape=jax.ShapeDtypeStruct((c // tn, r, tn), BF16),
        compiler_params=_params("parallel"),
        name="weight_cast",
    )(w)


def _call_with_casts(kernel_fn, grid, in_specs, out_specs, out_shape, args, jobs, name, scratch_shapes=()):
    n_in, n_out, n_jobs = len(in_specs), len(out_specs), len(jobs)
    steps, nj = grid[0] * grid[1], grid[1]
    in_specs, out_specs, out_shape, args = list(in_specs), list(out_specs), list(out_shape), list(args)
    for w, lead, tn in jobs:
        r, c = w.shape[-2:]
        slab = next(t for t in range(CAST_SLAB_ROWS, r + 1, CAST_SLAB_ROWS) if r % t == 0 and r // t <= steps)

        def slab_idx(i, j, last=r // slab - 1):
            return jnp.minimum(i * nj + j, last)

        in_specs.append(pl.BlockSpec((None,) * len(lead) + (slab, c),
                                     lambda i, j, lead=tuple(lead), f=slab_idx: lead + (f(i, j), 0)))
        out_specs.append(pl.BlockSpec((c // tn, slab, tn), lambda i, j, f=slab_idx: (0, f(i, j), 0)))
        out_shape.append(jax.ShapeDtypeStruct((c // tn, r, tn), BF16))
        args.append(w)

    def body(*refs):
        ins, srcs = refs[:n_in], refs[n_in:n_in + n_jobs]
        outs = refs[n_in + n_jobs:n_in + n_jobs + n_out]
        dsts = refs[n_in + n_jobs + n_out:n_in + 2 * n_jobs + n_out]
        for src, dst in zip(srcs, dsts):
            _cast_rows(src, dst)
        kernel_fn(*ins, *outs, *refs[n_in + 2 * n_jobs + n_out:])

    res = pl.pallas_call(
        body,
        grid=grid,
        in_specs=in_specs,
        out_specs=out_specs,
        out_shape=out_shape,
        scratch_shapes=list(scratch_shapes),
        compiler_params=_params("arbitrary" if jobs or scratch_shapes else "parallel", "arbitrary"),
        name=name,
    )(*args)
    return res[:n_out], res[n_out:]


def _adaln_kernel(x_ref, mod_ref, gain_ref, o_ref, *, rows3):
    gain_row, shift_row, scale_row = rows3
    gs = gain_ref[gain_row:gain_row + 1, :] * (1.0 + mod_ref[0, scale_row:scale_row + 1, :])
    shift = mod_ref[0, shift_row:shift_row + 1, :]

    def body(r, carry):
        sl = pl.ds(pl.multiple_of(r * F32_SUBLANES, F32_SUBLANES), F32_SUBLANES)
        x = x_ref[sl, :]
        y = x * lax.rsqrt(jnp.mean(x * x, axis=-1, keepdims=True) + EPS)
        o_ref[sl, :] = (y * gs + shift).astype(o_ref.dtype)
        return carry

    lax.fori_loop(0, x_ref.shape[0] // F32_SUBLANES, body, 0, unroll=ADALN_UNROLL)


def _adaln(x2d, mod, gain, rows3, rows_per_mod):
    m, d = x2d.shape
    tr = _pick(rows_per_mod, (512, 256, 128, 64, 32, 16, 8))
    bpm = rows_per_mod // tr
    return pl.pallas_call(
        functools.partial(_adaln_kernel, rows3=rows3),
        grid=(m // tr,),
        in_specs=[
            pl.BlockSpec((tr, d), lambda i: (i, 0)),
            pl.BlockSpec((1, N_MOD, d), lambda i: (i // bpm, 0, 0)),
            pl.BlockSpec(gain.shape, lambda i: (0, 0)),
        ],
        out_specs=pl.BlockSpec((tr, d), lambda i: (i, 0)),
        out_shape=jax.ShapeDtypeStruct((m, d), BF16),
        compiler_params=_params("parallel"),
        name="adaln",
    )(x2d, mod, gain)


def _row_tile(rows):
    return _pick(rows, (MATMUL_ROWS, 512, 256, 128, 64, 32, 16, 8))


def _up_rows(rows):
    return _pick(rows, (UP_ROWS, MATMUL_ROWS, 512, 256, 128, 64, 32, 16, 8))


def _col_tile(tm, n, a_bytes_per_row, w_bytes_per_col, io_bytes_per_elem):
    for tn in (1024, 512, 256, 128):
        vmem = 2 * (tm * a_bytes_per_row + tn * w_bytes_per_col + tm * tn * io_bytes_per_elem)
        if n % tn == 0 and vmem <= MATMUL_VMEM_BUDGET:
            return tn
    return 128


def _up_tn(rows, d, f):
    return _col_tile(_up_rows(rows), f, BF16_BYTES * d, 2 * BF16_BYTES * d, BF16_BYTES)


def _proj_tn(rows, d, n):
    return _col_tile(_row_tile(rows), n, BF16_BYTES * d, BF16_BYTES * d, F32_BYTES)


def _down_tn(rows, k, d):
    return _col_tile(_row_tile(rows), d, BF16_BYTES * k, BF16_BYTES * k, 2 * F32_BYTES)


def _up_kernel(h_ref, wg_ref, wu_ref, o_ref):
    h = h_ref[...]
    g = jnp.dot(h, wg_ref[...], preferred_element_type=F32)
    u = jnp.dot(h, wu_ref[...], preferred_element_type=F32)
    o_ref[...] = (jax.nn.silu(g) * u).astype(o_ref.dtype)


def _proj_kernel(h_ref, w_ref, o_ref):
    o_ref[...] = jnp.dot(h_ref[...], w_ref[...], preferred_element_type=F32).astype(o_ref.dtype)


def _ffn_up(h, w1, jobs=()):
    m, d = h.shape
    tn = w1.shape[2]
    nj = w1.shape[0] // 2
    tm = _up_rows(m)
    (a,), casts = _call_with_casts(
        _up_kernel,
        (m // tm, nj),
        [
            pl.BlockSpec((tm, d), lambda i, j: (i, 0)),
            pl.BlockSpec((None, d, tn), lambda i, j: (j, 0, 0)),
            pl.BlockSpec((None, d, tn), lambda i, j: (j + nj, 0, 0)),
        ],
        [pl.BlockSpec((tm, tn), lambda i, j: (i, j))],
        [jax.ShapeDtypeStruct((m, nj * tn), BF16)],
        [h, w1, w1], jobs, "ffn_up")
    return a, casts


def _proj(h, w, name, jobs=()):
    m, d = h.shape
    nj, _, tn = w.shape
    tm = _row_tile(m)
    (p,), casts = _call_with_casts(
        _proj_kernel,
        (m // tm, nj),
        [
            pl.BlockSpec((tm, d), lambda i, j: (i, 0)),
            pl.BlockSpec((None, d, tn), lambda i, j: (j, 0, 0)),
        ],
        [pl.BlockSpec((tm, tn), lambda i, j: (i, j))],
        [jax.ShapeDtypeStruct((m, nj * tn), F32)],
        [h, w], jobs, name)
    return p, casts


def _down_kernel(*refs, n_a, coef, steps):
    a_refs = refs[:n_a]
    w_hbm, x_hbm, gate_ref, o_ref, w_ring, x_ring, w_sem, x_sem = refs[n_a:]
    nj = pl.num_programs(1)
    step = pl.program_id(0) * nj + pl.program_id(1)
    kk = a_refs[0].shape[1]
    tm, tn = o_ref.shape

    def copies(s):
        slot = s % W_RING
        rows = pl.ds(pl.multiple_of((s // nj) * tm, tm), tm)
        cols = pl.ds(pl.multiple_of((s % nj) * tn, tn), tn)
        return (pltpu.make_async_copy(w_hbm.at[s % nj], w_ring.at[slot], w_sem.at[slot]),
                pltpu.make_async_copy(x_hbm.at[rows, cols], x_ring.at[slot], x_sem.at[slot]))

    @pl.when(step == 0)
    def _():
        for s in range(min(W_RING - 1, steps)):
            for c in copies(s):
                c.start()

    @pl.when(step + (W_RING - 1) < steps)
    def _():
        for c in copies(step + (W_RING - 1)):
            c.start()

    for c in copies(step):
        c.wait()
    w_tile = w_ring.at[step % W_RING]
    acc = jnp.dot(a_refs[0][...], w_tile[0:kk, :], preferred_element_type=F32)
    for r in range(1, n_a):
        acc = acc + jnp.dot(a_refs[r][...], w_tile[r * kk:(r + 1) * kk, :], preferred_element_type=F32)
    gate = gate_ref[0, pl.program_id(1)]
    if coef != 1.0:
        gate = coef * gate
    o_ref[...] = x_ring[step % W_RING] + gate * acc


def _down(a_list, w, x2d, mod, gate_row, coef, rows_per_mod, name, jobs=()):
    m, d = x2d.shape
    n_a = len(a_list)
    kk = a_list[0].shape[1]
    nj, _, tn = w.shape
    tm = _row_tile(rows_per_mod)
    bpm = rows_per_mod // tm
    gate = mod[:, gate_row].reshape(mod.shape[0], nj, 1, tn)
    in_specs = [pl.BlockSpec((tm, kk), lambda i, j: (i, 0)) for _ in a_list]
    in_specs += [
        pl.BlockSpec(memory_space=pl.ANY),
        pl.BlockSpec(memory_space=pl.ANY),
        pl.BlockSpec((1, nj, 1, tn), lambda i, j: (i // bpm, 0, 0, 0)),
    ]
    grid = (m // tm, nj)
    (y,), casts = _call_with_casts(
        functools.partial(_down_kernel, n_a=n_a, coef=coef, steps=grid[0] * grid[1]),
        grid,
        in_specs,
        [pl.BlockSpec((tm, tn), lambda i, j: (i, j))],
        [jax.ShapeDtypeStruct((m, d), F32)],
        [*a_list, w, x2d, gate], jobs, name,
        scratch_shapes=[pltpu.VMEM((W_RING,) + w.shape[1:], BF16), pltpu.VMEM((W_RING, tm, tn), F32),
                        pltpu.SemaphoreType.DMA((W_RING,)), pltpu.SemaphoreType.DMA((W_RING,))])
    return y, casts


def _head_norm(x, gain):
    return x * lax.rsqrt(jnp.mean(x * x, axis=-1, keepdims=True) + EPS) * gain


def _rope(x, c, s_hi, s_lo):
    return x * c + pltpu.roll(x, HEAD_DIM - ROPE_AXIS_DIM // 2, 1) * s_hi + pltpu.roll(x, ROPE_AXIS_DIM // 2, 1) * s_lo


def _attn_kernel(*refs, kv_rows, rope):
    n_kv = len(kv_rows)
    q_ref = refs[0]
    kv_refs = refs[1:1 + 2 * n_kv]
    qg_ref, kg_ref = refs[1 + 2 * n_kv:3 + 2 * n_kv]
    pos = 3 + 2 * n_kv
    if rope:
        cq_ref, hq_ref, lq_ref, ck_ref, hk_ref, lk_ref = refs[pos:pos + 6]
        pos += 6
    o_ref, k_scr, vt_scr = refs[pos:pos + 3]
    tq = q_ref.shape[0]

    @pl.when(pl.program_id(2) == 0)
    def _():
        off = 0
        for part, rows in enumerate(kv_rows):
            kn = _head_norm(kv_refs[2 * part][...], kg_ref[...])
            if rope and part == 0:
                kn = _rope(kn, ck_ref[...], hk_ref[...], lk_ref[...])
            k_scr[off:off + rows, :] = kn.astype(BF16)
            vt_scr[:HEAD_DIM, off:off + rows] = kv_refs[2 * part + 1][...].T.astype(BF16)
            off += rows
        vt_scr[HEAD_DIM:, :] = jnp.ones((vt_scr.shape[0] - HEAD_DIM, off), BF16)

    scale = HEAD_DIM ** -0.5 * LOG2_E
    qs = []
    for g in range(KV_GROUP):
        q = _head_norm(q_ref[:, g * HEAD_DIM:(g + 1) * HEAD_DIM], qg_ref[...])
        if rope:
            q = _rope(q, cq_ref[...], hq_ref[...], lq_ref[...])
        qs.append((q * scale).astype(BF16))
    q_all = jnp.concatenate(qs, axis=0)
    s_t = lax.dot_general(k_scr[...], q_all, (((1,), (1,)), ((), ())), preferred_element_type=F32)
    p_t = jnp.exp2(s_t - jnp.max(s_t, axis=0, keepdims=True)).astype(BF16)
    o_t = jnp.dot(vt_scr[...], p_t, preferred_element_type=F32)
    o = (o_t[:HEAD_DIM] / o_t[HEAD_DIM:HEAD_DIM + 1]).T
    for g in range(KV_GROUP):
        o_ref[:, g * HEAD_DIM:(g + 1) * HEAD_DIM] = o[g * tq:(g + 1) * tq].astype(o_ref.dtype)


def _attention(pq, kv_sources, q_gain, k_gain, tables, batch, q_col0, k_col0, v_col0):
    t = pq.shape[0] // batch
    n_kv_heads = (k_col0 - q_col0) // (KV_GROUP * HEAD_DIM)
    tq = _pick(t, (ATTN_Q_ROWS, 256, 128, 64, 32, 16, 8))
    nq = t // tq
    gw = KV_GROUP * HEAD_DIM
    kv_rows = tuple(src.shape[0] // batch for src in kv_sources)
    rope = tables is not None
    in_specs = [pl.BlockSpec((tq, gw), lambda b, h, i: (b * nq + i, q_col0 // gw + h))]
    args = [pq]
    for src, rows in zip(kv_sources, kv_rows):
        in_specs.append(pl.BlockSpec((rows, HEAD_DIM), lambda b, h, i: (b, k_col0 // HEAD_DIM + h)))
        in_specs.append(pl.BlockSpec((rows, HEAD_DIM), lambda b, h, i: (b, v_col0 // HEAD_DIM + h)))
        args += [src, src]
    in_specs += [pl.BlockSpec((1, HEAD_DIM), lambda b, h, i: (0, 0))] * 2
    args += [q_gain.reshape(1, HEAD_DIM), k_gain.reshape(1, HEAD_DIM)]
    if rope:
        in_specs += [pl.BlockSpec((tq, HEAD_DIM), lambda b, h, i: (i, 0))] * 3
        in_specs += [pl.BlockSpec((t, HEAD_DIM), lambda b, h, i: (0, 0))] * 3
        args += list(tables) * 2
    return pl.pallas_call(
        functools.partial(_attn_kernel, kv_rows=kv_rows, rope=rope),
        grid=(batch, n_kv_heads, nq),
        in_specs=in_specs,
        out_specs=pl.BlockSpec((tq, gw), lambda b, h, i: (b * nq + i, h)),
        out_shape=jax.ShapeDtypeStruct((batch * t, n_kv_heads * gw), BF16),
        scratch_shapes=[pltpu.VMEM((sum(kv_rows), HEAD_DIM), BF16),
                        pltpu.VMEM((HEAD_DIM + ATTN_ONES_ROWS, sum(kv_rows)), BF16)],
        compiler_params=_params("parallel", "parallel", "arbitrary"),
        name="gqa_rope" if rope else "gqa_ctx",
    )(*args)


def _rope_tables(t):
    rows = t // GRID_W
    row_ids = np.repeat(np.arange(rows, dtype=np.float32), GRID_W)
    col_ids = np.tile(np.arange(GRID_W, dtype=np.float32), rows)
    inv_freq = jnp.asarray(ROPE_THETA, F32) ** (-jnp.arange(0, ROPE_AXIS_DIM, 2, dtype=F32) / ROPE_AXIS_DIM)
    ang_r = jnp.asarray(row_ids)[:, None] * inv_freq
    ang_c = jnp.asarray(col_ids)[:, None] * inv_freq
    zero = jnp.zeros_like(ang_r)
    cos = jnp.concatenate([jnp.cos(ang_r), jnp.cos(ang_r), jnp.cos(ang_c), jnp.cos(ang_c)], axis=1)
    s_hi = jnp.concatenate([-jnp.sin(ang_r), zero, -jnp.sin(ang_c), zero], axis=1)
    s_lo = jnp.concatenate([zero, jnp.sin(ang_r), zero, jnp.sin(ang_c)], axis=1)
    return cos, s_hi, s_lo


def _chunk_tri(rev):
    idx = np.arange(HGRN_BLOCK)
    same = (idx[:, None] // HGRN_CHUNK) == (idx[None, :] // HGRN_CHUNK)
    tri = (idx[None, :] >= idx[:, None]) if rev else (idx[None, :] <= idx[:, None])
    return jnp.asarray((same & tri).astype(np.float32), BF16)


def _chunk_cols():
    idx = np.arange(HGRN_BLOCK)
    col = np.arange((HGRN_BLOCK // HGRN_CHUNK) * HEAD_DIM)
    return jnp.asarray(((idx[:, None] // HGRN_CHUNK) == (col[None, :] // HEAD_DIM)).astype(np.float32), BF16)


def _hgrn_kernel(*refs, rev, final, n_heads):
    q_ref, z_ref, v_ref, lb_ref, tri_ref, cmask_ref, s0_ref = refs[:7]
    pos = 7
    if final:
        oprev_ref, g_ref, gain_ref = refs[7:10]
        pos = 10
    o_ref, sfin_ref, st_scr = refs[pos:pos + 3]
    n_chunks = HGRN_BLOCK // HGRN_CHUNK
    ref_row = HGRN_CHUNK - 1 - HGRN_CHUNK // 2 if rev else HGRN_CHUNK // 2
    end_row = 0 if rev else HGRN_CHUNK - 1

    @pl.when(pl.program_id(2) == 0)
    def _():
        st_scr[...] = s0_ref[0]

    gw = n_heads * HEAD_DIM
    tri = tri_ref[...]
    in_chunk = tri > 0
    z = z_ref[...]
    lb = lb_ref[...]
    sig = jax.nn.sigmoid(z)
    log_f = jnp.log(lb + (1.0 - lb) * sig)
    k = (1.0 - lb) * (1.0 - sig)
    q = jax.nn.silu(q_ref[...])
    v = v_ref[...]
    g1 = log_f.astype(BF16)
    r1 = log_f - g1.astype(F32)
    g2 = r1.astype(BF16)
    g3 = (r1 - g2.astype(F32)).astype(BF16)
    cum = (jnp.dot(tri, g1, preferred_element_type=F32) + jnp.dot(tri, g2, preferred_element_type=F32)
           + jnp.dot(tri, g3, preferred_element_type=F32))
    shape3 = (n_chunks, HGRN_CHUNK, gw)
    cum3 = cum.reshape(shape3)
    ref3 = cum3[:, ref_row:ref_row + 1, :]
    end3 = cum3[:, end_row:end_row + 1, :]
    rel = cum3 - ref3
    qa3 = q.reshape(shape3) * jnp.exp(rel)
    ka3 = k.reshape(shape3) * jnp.exp(-rel)
    qa = qa3.reshape(cum.shape).astype(BF16)
    ka = ka3.reshape(cum.shape).astype(BF16)
    qe = (qa3 * jnp.exp(ref3)).reshape(cum.shape).astype(BF16)
    kd = (ka3 * jnp.exp(end3 - ref3)).reshape(cum.shape).astype(BF16)
    dec = jnp.exp(end3)
    vb = v.astype(BF16)
    chunk_cols = cmask_ref[...]

    heads = []
    for h in range(n_heads):
        cols = slice(h * HEAD_DIM, (h + 1) * HEAD_DIM)
        attn = lax.dot_general(qa[:, cols], ka[:, cols], (((1,), (1,)), ((), ())), preferred_element_type=F32)
        attn = jnp.where(in_chunk, attn, 0.0)
        o_intra = jnp.dot(attn.astype(BF16), vb[:, cols], preferred_element_type=F32)
        kd_wide = jnp.concatenate([kd[:, cols]] * n_chunks, axis=1) * chunk_cols
        upd = jnp.dot(v[:, cols].T.astype(BF16), kd_wide, preferred_element_type=F32)
        heads.append(dict(o_intra=o_intra, upd=upd, state=st_scr[h]))

    order = range(n_chunks - 1, -1, -1) if rev else range(n_chunks)
    for c in order:
        rows = slice(c * HGRN_CHUNK, (c + 1) * HGRN_CHUNK)
        for h, hd in enumerate(heads):
            cols = slice(h * HEAD_DIM, (h + 1) * HEAD_DIM)
            inter = lax.dot_general(qe[rows, cols], hd["state"].astype(BF16), (((1,), (1,)), ((), ())),
                                    preferred_element_type=F32)
            o = hd["o_intra"][rows] + inter
            hd["state"] = hd["state"] * dec[c][:, cols] + hd["upd"][:, c * HEAD_DIM:(c + 1) * HEAD_DIM]
            if final:
                o = o + oprev_ref[rows, cols]
                y = o * lax.rsqrt(jnp.mean(o * o, axis=-1, keepdims=True) + EPS) * gain_ref[...]
                o = y * jax.nn.silu(g_ref[rows, cols])
            o_ref[rows, cols] = o.astype(o_ref.dtype)
    for h, hd in enumerate(heads):
        st_scr[h] = hd["state"]
        sfin_ref[0, h] = hd["state"]


def _hgrn_scan(p, lb_row, s0, batch, cols, rev, o_prev=None, out_gain=None):
    t = p.shape[0] // batch
    nsb = t // HGRN_BLOCK
    n_heads = s0.shape[1]
    hps = next(g for g in (HGRN_HEADS_PER_STEP, 4, 2, 1)
               if n_heads % g == 0 and all(c % (g * HEAD_DIM) == 0 for c in cols))
    gw = hps * HEAD_DIM
    final = o_prev is not None
    q0, z0, v0, g0 = (c // gw for c in cols)

    def row_blk(b, s):
        return b * nsb + (nsb - 1 - s if rev else s)

    blk = (HGRN_BLOCK, gw)
    st_blk = (1, hps, HEAD_DIM, HEAD_DIM)
    in_specs = [
        pl.BlockSpec(blk, lambda b, h, s: (row_blk(b, s), q0 + h)),
        pl.BlockSpec(blk, lambda b, h, s: (row_blk(b, s), z0 + h)),
        pl.BlockSpec(blk, lambda b, h, s: (row_blk(b, s), v0 + h)),
        pl.BlockSpec((1, gw), lambda b, h, s: (0, h)),
        pl.BlockSpec((HGRN_BLOCK, HGRN_BLOCK), lambda b, h, s: (0, 0)),
        pl.BlockSpec((HGRN_BLOCK, (HGRN_BLOCK // HGRN_CHUNK) * HEAD_DIM), lambda b, h, s: (0, 0)),
        pl.BlockSpec(st_blk, lambda b, h, s: (b, h, 0, 0)),
    ]
    args = [p, p, p, lb_row, _chunk_tri(rev), _chunk_cols(), s0]
    if final:
        in_specs += [
            pl.BlockSpec(blk, lambda b, h, s: (row_blk(b, s), h)),
            pl.BlockSpec(blk, lambda b, h, s: (row_blk(b, s), g0 + h)),
            pl.BlockSpec((1, HEAD_DIM), lambda b, h, s: (0, 0)),
        ]
        args += [o_prev, p, out_gain.reshape(1, HEAD_DIM)]
    return pl.pallas_call(
        functools.partial(_hgrn_kernel, rev=rev, final=final, n_heads=hps),
        grid=(batch, n_heads // hps, nsb),
        in_specs=in_specs,
        out_specs=[
            pl.BlockSpec(blk, lambda b, h, s: (row_blk(b, s), h)),
            pl.BlockSpec(st_blk, lambda b, h, s: (b, h, 0, 0)),
        ],
        out_shape=[
            jax.ShapeDtypeStruct((batch * t, n_heads * HEAD_DIM), BF16 if final else F32),
            jax.ShapeDtypeStruct(s0.shape, F32),
        ],
        scratch_shapes=[pltpu.VMEM((hps, HEAD_DIM, HEAD_DIM), F32)],
        compiler_params=_params("parallel", "parallel", "arbitrary"),
        name="hgrn_bwd" if rev else "hgrn_fwd",
    )(*args)


def _conv_kernel(b_ref, c_ref, u_ref, w_ref, o_ref):
    u = c_ref[...] * u_ref[...]
    t = u.shape[0]
    row = lax.broadcasted_iota(jnp.int32, u.shape, 0)
    prev = jnp.where(row == 0, 0.0, pltpu.roll(u, 1, 0))
    nxt = jnp.where(row == t - 1, 0.0, pltpu.roll(u, t - 1, 0))
    y = prev * w_ref[0:1, :] + u * w_ref[1:2, :] + nxt * w_ref[2:3, :]
    o_ref[...] = (b_ref[...] * y).astype(o_ref.dtype)


def _gated_conv(p, conv_w, batch):
    d = conv_w.shape[1]
    t = p.shape[0] // batch
    td = _pick(d, (256, 128))
    nd = d // td
    return pl.pallas_call(
        _conv_kernel,
        grid=(batch, nd),
        in_specs=[
            pl.BlockSpec((t, td), lambda b, j: (b, j)),
            pl.BlockSpec((t, td), lambda b, j: (b, nd + j)),
            pl.BlockSpec((t, td), lambda b, j: (b, 2 * nd + j)),
            pl.BlockSpec((conv_w.shape[0], td), lambda b, j: (0, j)),
        ],
        out_specs=pl.BlockSpec((t, td), lambda b, j: (b, j)),
        out_shape=jax.ShapeDtypeStruct((batch * t, d), BF16),
        compiler_params=_params("parallel", "parallel"),
        name="gated_conv3",
    )(p, p, p, conv_w)


def kernel(x, c, ctx, c_ctx, mod_w, mod_b, norm_g, ffn_w1, ffn_w2, ab_w_in, ab_w_out, attn_q_gain, attn_k_gain, hgrn_lb_logits, hgrn_out_gain, conv_w_in, conv_w, conv_w_out):
    batch, seq, d = x.shape
    ctx_len = ctx.shape[1]
    depth = mod_w.shape[0]
    attn_w = d // 2
    kv_w = attn_w // KV_GROUP
    hg_w = d - attn_w
    n_hg_heads = hg_w // HEAD_DIM
    col_q, col_k, col_v = 0, attn_w, attn_w + kv_w
    col_qb = attn_w + 2 * kv_w
    col_zf, col_zb, col_ib, col_gb = (col_qb + hg_w * i for i in range(1, 5))

    lb_table = jnp.cumsum(jax.nn.softmax(hgrn_lb_logits.astype(F32), axis=1), axis=1)
    tables = _rope_tables(seq)

    pad = (-(batch + 1)) % 8
    cc = jnp.concatenate([c, c_ctx[None], jnp.zeros((pad, d), F32)], axis=0)
    mod_all = _mod_all(cc, mod_w, mod_b).reshape(depth, batch + 1 + pad, N_MOD, d)

    xl = x.reshape(batch * seq, d)
    xc = ctx.reshape(batch * ctx_len, d)
    n_ctx = batch * ctx_len

    n_lat = batch * seq
    f_ff = ffn_w2.shape[2]
    tn_up, tn_down = _up_tn(n_lat, d, f_ff), _down_tn(seq, f_ff, d)
    needs = {}
    for layer in range(depth):
        mix_in, mix_out = (ab_w_in, ab_w_out) if layer % 2 == 0 else (conv_w_in, conv_w_out)
        needs[("w1", layer, 0)] = (ffn_w1, (layer, 0), tn_up)
        needs[("w2", layer, 0)] = (ffn_w2, (layer, 0), tn_down)
        needs[("in", layer)] = (mix_in, (layer // 2,), _proj_tn(n_lat, d, mix_in.shape[2]))
        needs[("out", layer)] = (mix_out, (layer // 2,), _down_tn(seq, mix_out.shape[1], d))
        needs[("w1", layer, 1)] = (ffn_w1, (layer, 1), tn_up)
        needs[("w2", layer, 1)] = (ffn_w2, (layer, 1), tn_down)
    ready = {}

    def weight(key):
        if key not in ready:
            ready[key] = _weight_bf16(*needs[key])
        return ready[key]

    def hosted(call, w_key, tiles=1, tm=_row_tile(seq)):
        w = weight(w_key)
        steps = (n_lat // tm) * (w.shape[0] // tiles)
        budget = HOST_CAST_FRACTION * steps * tiles * w.shape[1] * w.shape[2] * BF16_BYTES
        keys = []
        for key, (src, _, _) in needs.items():
            size = src.shape[-2] * src.shape[-1] * F32_BYTES
            if key not in ready and size <= budget:
                keys.append(key)
                budget -= size
        out, casts = call([needs[k] for k in keys])
        ready.update(zip(keys, casts))
        return out

    def ffn(xs, mod, gain, layer, idx, rows3, gate_row, rpm, latent):
        h = _adaln(xs, mod, gain, rows3, rpm)
        k1, k2 = ("w1", layer, idx), ("w2", layer, idx)
        if latent:
            a = hosted(lambda jobs: _ffn_up(h, weight(k1), jobs), k1, tiles=2, tm=_up_rows(n_lat))
            return hosted(lambda jobs: _down([a], weight(k2), xs, mod, gate_row, 0.5, rpm, "ffn_down", jobs), k2)
        a, _ = _ffn_up(h, weight(k1))
        return _down([a], weight(k2), xs, mod, gate_row, 0.5, rpm, "ffn_down")[0]

    for layer in range(depth):
        last = layer == depth - 1
        even = layer % 2 == 0
        ctx_needed = even or not last
        gain = norm_g[layer]
        mod_l = mod_all[layer, :batch]
        mod_c = mod_all[layer, batch:batch + 1]

        xl = ffn(xl, mod_l, gain, layer, 0, (0, 0, 1), 2, seq, True)
        if ctx_needed:
            xc = ffn(xc, mod_c, gain, layer, 0, (0, 0, 1), 2, n_ctx, False)

        k_in, k_out = ("in", layer), ("out", layer)
        hl = _adaln(xl, mod_l, gain, (1, 3, 4), seq)
        if even:
            e = layer // 2
            pl_ = hosted(lambda jobs: _proj(hl, weight(k_in), "ab_proj", jobs), k_in)
            pc_, _ = _proj(_adaln(xc, mod_c, gain, (1, 3, 4), n_ctx), weight(k_in), "ab_proj")
            o_attn = _attention(pl_, [pl_, pc_], attn_q_gain[e], attn_k_gain[e], tables, batch, col_q, col_k, col_v)
            s_zero = jnp.zeros((batch, n_hg_heads, HEAD_DIM, HEAD_DIM), F32)
            lbs = [lb_table[dd, layer].reshape(1, hg_w) for dd in range(2)]
            zcols = (col_zf, col_zb)
            sc_dir = []
            o_prev = None
            for dd in range(2):
                o_prev, s_c = _hgrn_scan(pc_, lbs[dd], s_zero, batch, (col_qb, zcols[dd], col_ib, col_gb), dd == 1,
                                         o_prev, hgrn_out_gain[e] if dd == 1 else None)
                sc_dir.append(s_c)
            o_rec_c = o_prev
            o_prev = None
            for dd in range(2):
                o_prev, _ = _hgrn_scan(pl_, lbs[dd], sc_dir[dd], batch, (col_qb, zcols[dd], col_ib, col_gb), dd == 1,
                                       o_prev, hgrn_out_gain[e] if dd == 1 else None)
            mixed = [o_attn, o_prev]
            if not last:
                o_attn_c = _attention(pc_, [pc_], attn_q_gain[e], attn_k_gain[e], None, batch, col_q, col_k, col_v)
                mixed_c = [o_attn_c, o_rec_c]
        else:
            o = layer // 2
            pl_ = hosted(lambda jobs: _proj(hl, weight(k_in), "conv_proj", jobs), k_in)
            mixed = [_gated_conv(pl_, conv_w[o], batch)]
            if not last:
                pc_, _ = _proj(_adaln(xc, mod_c, gain, (1, 3, 4), n_ctx), weight(k_in), "conv_proj")
                mixed_c = [_gated_conv(pc_, conv_w[o], batch)]
        x_res = xl
        xl = hosted(lambda jobs: _down(mixed, weight(k_out), x_res, mod_l, 5, 1.0, seq, "mixer_out", jobs), k_out)
        if not last:
            xc = _down(mixed_c, weight(k_out), xc, mod_c, 5, 1.0, n_ctx, "mixer_out")[0]

        xl = ffn(xl, mod_l, gain, layer, 1, (2, 6, 7), 8, seq, True)
        if not last:
            xc = ffn(xc, mod_c, gain, layer, 1, (2, 6, 7), 8, n_ctx, False)

    return xl.reshape(batch, seq, d)
```

```python
import functools

import numpy as np
import jax
import jax.numpy as jnp
from jax import lax
from jax.experimental import pallas as pl
from jax.experimental.pallas import tpu as pltpu

GRID_W = 64
HEAD_DIM = 128
KV_GROUP = 4
ROPE_THETA = 10000.0
ROPE_AXIS_DIM = HEAD_DIM // 2
HGRN_CHUNK = 32
N_MOD = 9
EPS = 1e-6
LOG2_E = 1.4426950408889634

V7X_VMEM_BYTES = 64 * 1024 * 1024
VMEM_LIMIT_BYTES = V7X_VMEM_BYTES * 7 // 8
MATMUL_VMEM_BUDGET = 46 * 1024 * 1024
F32_SUBLANES = 8
BF16_BYTES, F32_BYTES = 2, 4
ADALN_UNROLL = 16
MATMUL_ROWS = 1024
A_LEAD = 3
W_RING = 3
ATTN_Q_ROWS = 256
ATTN_ONES_ROWS = 16
HGRN_BLOCK = 256
HGRN_HEADS_PER_STEP = 8
CAST_BLOCK_BYTES = 8 * 1024 * 1024
CAST_SLAB_ROWS = 16
HOST_CAST_FRACTION = 0.2

BF16 = jnp.bfloat16
F32 = jnp.float32


def _params(*sem):
    return pltpu.CompilerParams(dimension_semantics=sem, vmem_limit_bytes=VMEM_LIMIT_BYTES)


def _pick(n, candidates):
    for c in candidates:
        if n % c == 0:
            return c
    return n


def _mod_kernel(c_ref, w_ref, b_ref, o_ref):
    a = jax.nn.silu(c_ref[...]).astype(BF16)
    o_ref[0] = jnp.dot(a, w_ref[0].astype(BF16), preferred_element_type=F32) + b_ref[0]


def _mod_all(cc, mod_w, mod_b):
    depth, d, n = mod_w.shape
    rows = cc.shape[0]
    tn = _pick(n, (512, 256, 128))
    return pl.pallas_call(
        _mod_kernel,
        grid=(depth, n // tn),
        in_specs=[
            pl.BlockSpec((rows, d), lambda l, j: (0, 0)),
            pl.BlockSpec((1, d, tn), lambda l, j: (l, 0, j)),
            pl.BlockSpec((1, 1, tn), lambda l, j: (l, 0, j)),
        ],
        out_specs=pl.BlockSpec((1, rows, tn), lambda l, j: (l, 0, j)),
        out_shape=jax.ShapeDtypeStruct((depth, rows, n), F32),
        compiler_params=_params("parallel", "parallel"),
        name="mod_proj",
    )(cc, mod_w, mod_b.reshape(depth, 1, n))


def _cast_rows(src_ref, dst_ref):
    tn = dst_ref.shape[2]
    for t in range(dst_ref.shape[0]):
        dst_ref[t] = src_ref[:, t * tn:(t + 1) * tn].astype(dst_ref.dtype)


def _weight_bf16(w, lead, tn):
    r, c = w.shape[-2:]
    tr = _pick(r, tuple(t for t in (2048, 1024, 512, 256, 128, 64, 32, 16) if t * c * F32_BYTES <= CAST_BLOCK_BYTES))
    return pl.pallas_call(
        _cast_rows,
        grid=(r // tr,),
        in_specs=[pl.BlockSpec((None,) * len(lead) + (tr, c), lambda i: tuple(lead) + (i, 0))],
        out_specs=pl.BlockSpec((c // tn, tr, tn), lambda i: (0, i, 0)),
        out_shape=jax.ShapeDtypeStruct((c // tn, r, tn), BF16),
        compiler_params=_params("parallel"),
        name="weight_cast",
    )(w)


def _call_with_casts(kernel_fn, grid, in_specs, out_specs, out_shape, args, jobs, name, scratch_shapes=()):
    n_in, n_out, n_jobs = len(in_specs), len(out_specs), len(jobs)
    steps, nj = grid[0] * grid[1], grid[1]
    in_specs, out_specs, out_shape, args = list(in_specs), list(out_specs), list(out_shape), list(args)
    for w, lead, tn in jobs:
        r, c = w.shape[-2:]
        slab = next(t for t in range(CAST_SLAB_ROWS, r + 1, CAST_SLAB_ROWS) if r % t == 0 and r // t <= steps)

        def slab_idx(i, j, last=r // slab - 1):
            return jnp.minimum(i * nj + j, last)

        in_specs.append(pl.BlockSpec((None,) * len(lead) + (slab, c),
                                     lambda i, j, lead=tuple(lead), f=slab_idx: lead + (f(i, j), 0)))
        out_specs.append(pl.BlockSpec((c // tn, slab, tn), lambda i, j, f=slab_idx: (0, f(i, j), 0)))
        out_shape.append(jax.ShapeDtypeStruct((c // tn, r, tn), BF16))
        args.append(w)

    def body(*refs):
        ins, srcs = refs[:n_in], refs[n_in:n_in + n_jobs]
        outs = refs[n_in + n_jobs:n_in + n_jobs + n_out]
        dsts = refs[n_in + n_jobs + n_out:n_in + 2 * n_jobs + n_out]
        for src, dst in zip(srcs, dsts):
            _cast_rows(src, dst)
        kernel_fn(*ins, *outs, *refs[n_in + 2 * n_jobs + n_out:])

    res = pl.pallas_call(
        body,
        grid=grid,
        in_specs=in_specs,
        out_specs=out_specs,
        out_shape=out_shape,
        scratch_shapes=list(scratch_shapes),
        compiler_params=_params("arbitrary" if jobs or scratch_shapes else "parallel", "arbitrary"),
        name=name,
    )(*args)
    return res[:n_out], res[n_out:]


def _adaln_kernel(x_ref, mod_ref, gain_ref, o_ref, *, rows3):
    gain_row, shift_row, scale_row = rows3
    gs = gain_ref[gain_row:gain_row + 1, :] * (1.0 + mod_ref[0, scale_row:scale_row + 1, :])
    shift = mod_ref[0, shift_row:shift_row + 1, :]

    def body(r, carry):
        sl = pl.ds(pl.multiple_of(r * F32_SUBLANES, F32_SUBLANES), F32_SUBLANES)
        x = x_ref[sl, :]
        y = x * lax.rsqrt(jnp.mean(x * x, axis=-1, keepdims=True) + EPS)
        o_ref[sl, :] = (y * gs + shift).astype(o_ref.dtype)
        return carry

    lax.fori_loop(0, x_ref.shape[0] // F32_SUBLANES, body, 0, unroll=ADALN_UNROLL)


def _adaln(x2d, mod, gain, rows3, rows_per_mod):
    m, d = x2d.shape
    tr = _pick(rows_per_mod, (512, 256, 128, 64, 32, 16, 8))
    bpm = rows_per_mod // tr
    return pl.pallas_call(
        functools.partial(_adaln_kernel, rows3=rows3),
        grid=(m // tr,),
        in_specs=[
            pl.BlockSpec((tr, d), lambda i: (i, 0)),
            pl.BlockSpec((1, N_MOD, d), lambda i: (i // bpm, 0, 0)),
            pl.BlockSpec(gain.shape, lambda i: (0, 0)),
        ],
        out_specs=pl.BlockSpec((tr, d), lambda i: (i, 0)),
        out_shape=jax.ShapeDtypeStruct((m, d), BF16),
        compiler_params=_params("parallel"),
        name="adaln",
    )(x2d, mod, gain)


def _row_tile(rows):
    return _pick(rows, (MATMUL_ROWS, 512, 256, 128, 64, 32, 16, 8))


def _col_tile(tm, n, a_bytes_per_row, w_bytes_per_col, io_bytes_per_elem):
    for tn in (1024, 512, 256, 128):
        vmem = 2 * (tm * a_bytes_per_row + tn * w_bytes_per_col + tm * tn * io_bytes_per_elem)
        if n % tn == 0 and vmem <= MATMUL_VMEM_BUDGET:
            return tn
    return 128


def _up_tn(rows, d, f):
    return _col_tile(_row_tile(rows), f, BF16_BYTES * d, 2 * BF16_BYTES * d, BF16_BYTES)


def _proj_tn(rows, d, n):
    return _col_tile(_row_tile(rows), n, BF16_BYTES * d, BF16_BYTES * d, F32_BYTES)


def _down_tn(rows, k, d):
    return _col_tile(_row_tile(rows), d, BF16_BYTES * k, BF16_BYTES * k, 2 * F32_BYTES)


def _up_kernel(h_ref, wg_ref, wu_ref, o_ref):
    h = h_ref[...]
    g = jnp.dot(h, wg_ref[...], preferred_element_type=F32)
    u = jnp.dot(h, wu_ref[...], preferred_element_type=F32)
    o_ref[...] = (jax.nn.silu(g) * u).astype(o_ref.dtype)


def _proj_kernel(h_ref, w_ref, o_ref):
    o_ref[...] = jnp.dot(h_ref[...], w_ref[...], preferred_element_type=F32).astype(o_ref.dtype)


def _ffn_up(h, w1, jobs=()):
    m, d = h.shape
    tn = w1.shape[2]
    nj = w1.shape[0] // 2
    tm = _row_tile(m)
    (a,), casts = _call_with_casts(
        _up_kernel,
        (m // tm, nj),
        [
            pl.BlockSpec((tm, d), lambda i, j: (i, 0)),
            pl.BlockSpec((None, d, tn), lambda i, j: (j, 0, 0)),
            pl.BlockSpec((None, d, tn), lambda i, j: (j + nj, 0, 0)),
        ],
        [pl.BlockSpec((tm, tn), lambda i, j: (i, j))],
        [jax.ShapeDtypeStruct((m, nj * tn), BF16)],
        [h, w1, w1], jobs, "ffn_up")
    return a, casts


def _proj(h, w, name, jobs=()):
    m, d = h.shape
    nj, _, tn = w.shape
    tm = _row_tile(m)
    (p,), casts = _call_with_casts(
        _proj_kernel,
        (m // tm, nj),
        [
            pl.BlockSpec((tm, d), lambda i, j: (i, 0)),
            pl.BlockSpec((None, d, tn), lambda i, j: (j, 0, 0)),
        ],
        [pl.BlockSpec((tm, tn), lambda i, j: (i, j))],
        [jax.ShapeDtypeStruct((m, nj * tn), F32)],
        [h, w], jobs, name)
    return p, casts


def _down_kernel(*refs, n_a, coef, steps):
    a_hbms = refs[:n_a]
    w_hbm, x_hbm, gate_ref, o_ref = refs[n_a:n_a + 4]
    a_rings = refs[n_a + 4:2 * n_a + 4]
    w_ring, x_ring, a_sem, w_sem, x_sem = refs[2 * n_a + 4:]
    blk, nblk, nj = pl.program_id(0), pl.num_programs(0), pl.num_programs(1)
    step = blk * nj + pl.program_id(1)
    kk = a_hbms[0].shape[1]
    tm, tn = o_ref.shape

    def a_copies(b):
        rows = pl.ds(pl.multiple_of(b * tm, tm), tm)
        return [pltpu.make_async_copy(a_hbms[r].at[rows, :], a_rings[r].at[b % 2], a_sem.at[b % 2, r]) for r in range(n_a)]

    @pl.when(step == 0)
    def _():
        for c in a_copies(0):
            c.start()

    @pl.when((pl.program_id(1) == jnp.maximum(nj - A_LEAD, 0)) & (blk + 1 < nblk))
    def _():
        for c in a_copies(blk + 1):
            c.start()

    @pl.when(pl.program_id(1) == 0)
    def _():
        for c in a_copies(blk):
            c.wait()

    def copies(s):
        slot = s % W_RING
        rows = pl.ds(pl.multiple_of((s // nj) * tm, tm), tm)
        cols = pl.ds(pl.multiple_of((s % nj) * tn, tn), tn)
        return (pltpu.make_async_copy(w_hbm.at[s % nj], w_ring.at[slot], w_sem.at[slot]),
                pltpu.make_async_copy(x_hbm.at[rows, cols], x_ring.at[slot], x_sem.at[slot]))

    @pl.when(step == 0)
    def _():
        for s in range(min(W_RING - 1, steps)):
            for c in copies(s):
                c.start()

    @pl.when(step + (W_RING - 1) < steps)
    def _():
        for c in copies(step + (W_RING - 1)):
            c.start()

    for c in copies(step):
        c.wait()
    w_tile = w_ring.at[step % W_RING]
    acc = jnp.dot(a_rings[0][blk % 2], w_tile[0:kk, :], preferred_element_type=F32)
    for r in range(1, n_a):
        acc = acc + jnp.dot(a_rings[r][blk % 2], w_tile[r * kk:(r + 1) * kk, :], preferred_element_type=F32)
    gate = gate_ref[0, pl.program_id(1)]
    if coef != 1.0:
        gate = coef * gate
    o_ref[...] = x_ring[step % W_RING] + gate * acc


def _down(a_list, w, x2d, mod, gate_row, coef, rows_per_mod, name, jobs=()):
    m, d = x2d.shape
    n_a = len(a_list)
    kk = a_list[0].shape[1]
    nj, _, tn = w.shape
    tm = _row_tile(rows_per_mod)
    bpm = rows_per_mod // tm
    gate = mod[:, gate_row].reshape(mod.shape[0], nj, 1, tn)
    in_specs = [pl.BlockSpec(memory_space=pl.ANY)] * (n_a + 2)
    in_specs.append(pl.BlockSpec((1, nj, 1, tn), lambda i, j: (i // bpm, 0, 0, 0)))
    grid = (m // tm, nj)
    (y,), casts = _call_with_casts(
        functools.partial(_down_kernel, n_a=n_a, coef=coef, steps=grid[0] * grid[1]),
        grid,
        in_specs,
        [pl.BlockSpec((tm, tn), lambda i, j: (i, j))],
        [jax.ShapeDtypeStruct((m, d), F32)],
        [*a_list, w, x2d, gate], jobs, name,
        scratch_shapes=[pltpu.VMEM((2, tm, kk), BF16)] * n_a
        + [pltpu.VMEM((W_RING,) + w.shape[1:], BF16), pltpu.VMEM((W_RING, tm, tn), F32),
           pltpu.SemaphoreType.DMA((2, n_a)), pltpu.SemaphoreType.DMA((W_RING,)), pltpu.SemaphoreType.DMA((W_RING,))])
    return y, casts


def _head_norm(x, gain):
    return x * lax.rsqrt(jnp.mean(x * x, axis=-1, keepdims=True) + EPS) * gain


def _rope(x, c, s_hi, s_lo):
    return x * c + pltpu.roll(x, HEAD_DIM - ROPE_AXIS_DIM // 2, 1) * s_hi + pltpu.roll(x, ROPE_AXIS_DIM // 2, 1) * s_lo


def _attn_kernel(*refs, kv_rows, rope):
    n_kv = len(kv_rows)
    q_ref = refs[0]
    kv_refs = refs[1:1 + 2 * n_kv]
    qg_ref, kg_ref = refs[1 + 2 * n_kv:3 + 2 * n_kv]
    pos = 3 + 2 * n_kv
    if rope:
        cq_ref, hq_ref, lq_ref, ck_ref, hk_ref, lk_ref = refs[pos:pos + 6]
        pos += 6
    o_ref, k_scr, vt_scr = refs[pos:pos + 3]
    tq = q_ref.shape[0]

    @pl.when(pl.program_id(2) == 0)
    def _():
        off = 0
        for part, rows in enumerate(kv_rows):
            kn = _head_norm(kv_refs[2 * part][...], kg_ref[...])
            if rope and part == 0:
                kn = _rope(kn, ck_ref[...], hk_ref[...], lk_ref[...])
            k_scr[off:off + rows, :] = kn.astype(BF16)
            vt_scr[:HEAD_DIM, off:off + rows] = kv_refs[2 * part + 1][...].T.astype(BF16)
            off += rows
        vt_scr[HEAD_DIM:, :] = jnp.ones((vt_scr.shape[0] - HEAD_DIM, off), BF16)

    scale = HEAD_DIM ** -0.5 * LOG2_E
    qs = []
    for g in range(KV_GROUP):
        q = _head_norm(q_ref[:, g * HEAD_DIM:(g + 1) * HEAD_DIM], qg_ref[...])
        if rope:
            q = _rope(q, cq_ref[...], hq_ref[...], lq_ref[...])
        qs.append((q * scale).astype(BF16))
    q_all = jnp.concatenate(qs, axis=0)
    s_t = lax.dot_general(k_scr[...], q_all, (((1,), (1,)), ((), ())), preferred_element_type=F32)
    p_t = jnp.exp2(s_t - jnp.max(s_t, axis=0, keepdims=True)).astype(BF16)
    o_t = jnp.dot(vt_scr[...], p_t, preferred_element_type=F32)
    o = (o_t[:HEAD_DIM] / o_t[HEAD_DIM:HEAD_DIM + 1]).T
    for g in range(KV_GROUP):
        o_ref[:, g * HEAD_DIM:(g + 1) * HEAD_DIM] = o[g * tq:(g + 1) * tq].astype(o_ref.dtype)


def _attention(pq, kv_sources, q_gain, k_gain, tables, batch, q_col0, k_col0, v_col0):
    t = pq.shape[0] // batch
    n_kv_heads = (k_col0 - q_col0) // (KV_GROUP * HEAD_DIM)
    tq = _pick(t, (ATTN_Q_ROWS, 256, 128, 64, 32, 16, 8))
    nq = t // tq
    gw = KV_GROUP * HEAD_DIM
    kv_rows = tuple(src.shape[0] // batch for src in kv_sources)
    rope = tables is not None
    in_specs = [pl.BlockSpec((tq, gw), lambda b, h, i: (b * nq + i, q_col0 // gw + h))]
    args = [pq]
    for src, rows in zip(kv_sources, kv_rows):
        in_specs.append(pl.BlockSpec((rows, HEAD_DIM), lambda b, h, i: (b, k_col0 // HEAD_DIM + h)))
        in_specs.append(pl.BlockSpec((rows, HEAD_DIM), lambda b, h, i: (b, v_col0 // HEAD_DIM + h)))
        args += [src, src]
    in_specs += [pl.BlockSpec((1, HEAD_DIM), lambda b, h, i: (0, 0))] * 2
    args += [q_gain.reshape(1, HEAD_DIM), k_gain.reshape(1, HEAD_DIM)]
    if rope:
        in_specs += [pl.BlockSpec((tq, HEAD_DIM), lambda b, h, i: (i, 0))] * 3
        in_specs += [pl.BlockSpec((t, HEAD_DIM), lambda b, h, i: (0, 0))] * 3
        args += list(tables) * 2
    return pl.pallas_call(
        functools.partial(_attn_kernel, kv_rows=kv_rows, rope=rope),
        grid=(batch, n_kv_heads, nq),
        in_specs=in_specs,
        out_specs=pl.BlockSpec((tq, gw), lambda b, h, i: (b * nq + i, h)),
        out_shape=jax.ShapeDtypeStruct((batch * t, n_kv_heads * gw), BF16),
        scratch_shapes=[pltpu.VMEM((sum(kv_rows), HEAD_DIM), BF16),
                        pltpu.VMEM((HEAD_DIM + ATTN_ONES_ROWS, sum(kv_rows)), BF16)],
        compiler_params=_params("parallel", "parallel", "arbitrary"),
        name="gqa_rope" if rope else "gqa_ctx",
    )(*args)


def _rope_tables(t):
    rows = t // GRID_W
    row_ids = np.repeat(np.arange(rows, dtype=np.float32), GRID_W)
    col_ids = np.tile(np.arange(GRID_W, dtype=np.float32), rows)
    inv_freq = jnp.asarray(ROPE_THETA, F32) ** (-jnp.arange(0, ROPE_AXIS_DIM, 2, dtype=F32) / ROPE_AXIS_DIM)
    ang_r = jnp.asarray(row_ids)[:, None] * inv_freq
    ang_c = jnp.asarray(col_ids)[:, None] * inv_freq
    zero = jnp.zeros_like(ang_r)
    cos = jnp.concatenate([jnp.cos(ang_r), jnp.cos(ang_r), jnp.cos(ang_c), jnp.cos(ang_c)], axis=1)
    s_hi = jnp.concatenate([-jnp.sin(ang_r), zero, -jnp.sin(ang_c), zero], axis=1)
    s_lo = jnp.concatenate([zero, jnp.sin(ang_r), zero, jnp.sin(ang_c)], axis=1)
    return cos, s_hi, s_lo


def _chunk_tri(rev):
    idx = np.arange(HGRN_BLOCK)
    same = (idx[:, None] // HGRN_CHUNK) == (idx[None, :] // HGRN_CHUNK)
    tri = (idx[None, :] >= idx[:, None]) if rev else (idx[None, :] <= idx[:, None])
    return jnp.asarray((same & tri).astype(np.float32), BF16)


def _chunk_cols():
    idx = np.arange(HGRN_BLOCK)
    col = np.arange((HGRN_BLOCK // HGRN_CHUNK) * HEAD_DIM)
    return jnp.asarray(((idx[:, None] // HGRN_CHUNK) == (col[None, :] // HEAD_DIM)).astype(np.float32), BF16)


def _hgrn_kernel(*refs, rev, final, n_heads):
    q_ref, z_ref, v_ref, lb_ref, tri_ref, cmask_ref, s0_ref = refs[:7]
    pos = 7
    if final:
        oprev_ref, g_ref, gain_ref = refs[7:10]
        pos = 10
    o_ref, sfin_ref, st_scr = refs[pos:pos + 3]
    n_chunks = HGRN_BLOCK // HGRN_CHUNK
    ref_row = HGRN_CHUNK - 1 - HGRN_CHUNK // 2 if rev else HGRN_CHUNK // 2
    end_row = 0 if rev else HGRN_CHUNK - 1

    @pl.when(pl.program_id(2) == 0)
    def _():
        st_scr[...] = s0_ref[0]

    gw = n_heads * HEAD_DIM
    tri = tri_ref[...]
    in_chunk = tri > 0
    z = z_ref[...]
    lb = lb_ref[...]
    sig = jax.nn.sigmoid(z)
    log_f = jnp.log(lb + (1.0 - lb) * sig)
    k = (1.0 - lb) * (1.0 - sig)
    q = jax.nn.silu(q_ref[...])
    v = v_ref[...]
    g1 = log_f.astype(BF16)
    r1 = log_f - g1.astype(F32)
    g2 = r1.astype(BF16)
    g3 = (r1 - g2.astype(F32)).astype(BF16)
    cum = (jnp.dot(tri, g1, preferred_element_type=F32) + jnp.dot(tri, g2, preferred_element_type=F32)
           + jnp.dot(tri, g3, preferred_element_type=F32))
    shape3 = (n_chunks, HGRN_CHUNK, gw)
    cum3 = cum.reshape(shape3)
    ref3 = cum3[:, ref_row:ref_row + 1, :]
    end3 = cum3[:, end_row:end_row + 1, :]
    rel = cum3 - ref3
    qa3 = q.reshape(shape3) * jnp.exp(rel)
    ka3 = k.reshape(shape3) * jnp.exp(-rel)
    qa = qa3.reshape(cum.shape).astype(BF16)
    ka = ka3.reshape(cum.shape).astype(BF16)
    qe = (qa3 * jnp.exp(ref3)).reshape(cum.shape).astype(BF16)
    kd = (ka3 * jnp.exp(end3 - ref3)).reshape(cum.shape).astype(BF16)
    dec = jnp.exp(end3)
    vb = v.astype(BF16)
    chunk_cols = cmask_ref[...]

    heads = []
    for h in range(n_heads):
        cols = slice(h * HEAD_DIM, (h + 1) * HEAD_DIM)
        attn = lax.dot_general(qa[:, cols], ka[:, cols], (((1,), (1,)), ((), ())), preferred_element_type=F32)
        attn = jnp.where(in_chunk, attn, 0.0)
        o_intra = jnp.dot(attn.astype(BF16), vb[:, cols], preferred_element_type=F32)
        kd_wide = jnp.concatenate([kd[:, cols]] * n_chunks, axis=1) * chunk_cols
        upd = jnp.dot(v[:, cols].T.astype(BF16), kd_wide, preferred_element_type=F32)
        heads.append(dict(o_intra=o_intra, upd=upd, state=st_scr[h]))

    order = range(n_chunks - 1, -1, -1) if rev else range(n_chunks)
    for c in order:
        rows = slice(c * HGRN_CHUNK, (c + 1) * HGRN_CHUNK)
        for h, hd in enumerate(heads):
            cols = slice(h * HEAD_DIM, (h + 1) * HEAD_DIM)
            inter = lax.dot_general(qe[rows, cols], hd["state"].astype(BF16), (((1,), (1,)), ((), ())),
                                    preferred_element_type=F32)
            o = hd["o_intra"][rows] + inter
            hd["state"] = hd["state"] * dec[c][:, cols] + hd["upd"][:, c * HEAD_DIM:(c + 1) * HEAD_DIM]
            if final:
                o = o + oprev_ref[rows, cols]
                y = o * lax.rsqrt(jnp.mean(o * o, axis=-1, keepdims=True) + EPS) * gain_ref[...]
                o = y * jax.nn.silu(g_ref[rows, cols])
            o_ref[rows, cols] = o.astype(o_ref.dtype)
    for h, hd in enumerate(heads):
        st_scr[h] = hd["state"]
        sfin_ref[0, h] = hd["state"]


def _hgrn_scan(p, lb_row, s0, batch, cols, rev, o_prev=None, out_gain=None):
    t = p.shape[0] // batch
    nsb = t // HGRN_BLOCK
    n_heads = s0.shape[1]
    hps = next(g for g in (HGRN_HEADS_PER_STEP, 4, 2, 1)
               if n_heads % g == 0 and all(c % (g * HEAD_DIM) == 0 for c in cols))
    gw = hps * HEAD_DIM
    final = o_prev is not None
    q0, z0, v0, g0 = (c // gw for c in cols)

    def row_blk(b, s):
        return b * nsb + (nsb - 1 - s if rev else s)

    blk = (HGRN_BLOCK, gw)
    st_blk = (1, hps, HEAD_DIM, HEAD_DIM)
    in_specs = [
        pl.BlockSpec(blk, lambda b, h, s: (row_blk(b, s), q0 + h)),
        pl.BlockSpec(blk, lambda b, h, s: (row_blk(b, s), z0 + h)),
        pl.BlockSpec(blk, lambda b, h, s: (row_blk(b, s), v0 + h)),
        pl.BlockSpec((1, gw), lambda b, h, s: (0, h)),
        pl.BlockSpec((HGRN_BLOCK, HGRN_BLOCK), lambda b, h, s: (0, 0)),
        pl.BlockSpec((HGRN_BLOCK, (HGRN_BLOCK // HGRN_CHUNK) * HEAD_DIM), lambda b, h, s: (0, 0)),
        pl.BlockSpec(st_blk, lambda b, h, s: (b, h, 0, 0)),
    ]
    args = [p, p, p, lb_row, _chunk_tri(rev), _chunk_cols(), s0]
    if final:
        in_specs += [
            pl.BlockSpec(blk, lambda b, h, s: (row_blk(b, s), h)),
            pl.BlockSpec(blk, lambda b, h, s: (row_blk(b, s), g0 + h)),
            pl.BlockSpec((1, HEAD_DIM), lambda b, h, s: (0, 0)),
        ]
        args += [o_prev, p, out_gain.reshape(1, HEAD_DIM)]
    return pl.pallas_call(
        functools.partial(_hgrn_kernel, rev=rev, final=final, n_heads=hps),
        grid=(batch, n_heads // hps, nsb),
        in_specs=in_specs,
        out_specs=[
            pl.BlockSpec(blk, lambda b, h, s: (row_blk(b, s), h)),
            pl.BlockSpec(st_blk, lambda b, h, s: (b, h, 0, 0)),
        ],
        out_shape=[
            jax.ShapeDtypeStruct((batch * t, n_heads * HEAD_DIM), BF16 if final else F32),
            jax.ShapeDtypeStruct(s0.shape, F32),
        ],
        scratch_shapes=[pltpu.VMEM((hps, HEAD_DIM, HEAD_DIM), F32)],
        compiler_params=_params("parallel", "parallel", "arbitrary"),
        name="hgrn_bwd" if rev else "hgrn_fwd",
    )(*args)


def _conv_kernel(b_ref, c_ref, u_ref, w_ref, o_ref):
    u = c_ref[...] * u_ref[...]
    t = u.shape[0]
    row = lax.broadcasted_iota(jnp.int32, u.shape, 0)
    prev = jnp.where(row == 0, 0.0, pltpu.roll(u, 1, 0))
    nxt = jnp.where(row == t - 1, 0.0, pltpu.roll(u, t - 1, 0))
    y = prev * w_ref[0:1, :] + u * w_ref[1:2, :] + nxt * w_ref[2:3, :]
    o_ref[...] = (b_ref[...] * y).astype(o_ref.dtype)


def _gated_conv(p, conv_w, batch):
    d = conv_w.shape[1]
    t = p.shape[0] // batch
    td = _pick(d, (256, 128))
    nd = d // td
    return pl.pallas_call(
        _conv_kernel,
        grid=(batch, nd),
        in_specs=[
            pl.BlockSpec((t, td), lambda b, j: (b, j)),
            pl.BlockSpec((t, td), lambda b, j: (b, nd + j)),
            pl.BlockSpec((t, td), lambda b, j: (b, 2 * nd + j)),
            pl.BlockSpec((conv_w.shape[0], td), lambda b, j: (0, j)),
        ],
        out_specs=pl.BlockSpec((t, td), lambda b, j: (b, j)),
        out_shape=jax.ShapeDtypeStruct((batch * t, d), BF16),
        compiler_params=_params("parallel", "parallel"),
        name="gated_conv3",
    )(p, p, p, conv_w)


def kernel(x, c, ctx, c_ctx, mod_w, mod_b, norm_g, ffn_w1, ffn_w2, ab_w_in, ab_w_out, attn_q_gain, attn_k_gain, hgrn_lb_logits, hgrn_out_gain, conv_w_in, conv_w, conv_w_out):
    batch, seq, d = x.shape
    ctx_len = ctx.shape[1]
    depth = mod_w.shape[0]
    attn_w = d // 2
    kv_w = attn_w // KV_GROUP
    hg_w = d - attn_w
    n_hg_heads = hg_w // HEAD_DIM
    col_q, col_k, col_v = 0, attn_w, attn_w + kv_w
    col_qb = attn_w + 2 * kv_w
    col_zf, col_zb, col_ib, col_gb = (col_qb + hg_w * i for i in range(1, 5))

    lb_table = jnp.cumsum(jax.nn.softmax(hgrn_lb_logits.astype(F32), axis=1), axis=1)
    tables = _rope_tables(seq)

    pad = (-(batch + 1)) % 8
    cc = jnp.concatenate([c, c_ctx[None], jnp.zeros((pad, d), F32)], axis=0)
    mod_all = _mod_all(cc, mod_w, mod_b).reshape(depth, batch + 1 + pad, N_MOD, d)

    xl = x.reshape(batch * seq, d)
    xc = ctx.reshape(batch * ctx_len, d)
    n_ctx = batch * ctx_len

    n_lat = batch * seq
    f_ff = ffn_w2.shape[2]
    tn_up, tn_down = _up_tn(n_lat, d, f_ff), _down_tn(seq, f_ff, d)
    needs = {}
    for layer in range(depth):
        mix_in, mix_out = (ab_w_in, ab_w_out) if layer % 2 == 0 else (conv_w_in, conv_w_out)
        needs[("w1", layer, 0)] = (ffn_w1, (layer, 0), tn_up)
        needs[("w2", layer, 0)] = (ffn_w2, (layer, 0), tn_down)
        needs[("in", layer)] = (mix_in, (layer // 2,), _proj_tn(n_lat, d, mix_in.shape[2]))
        needs[("out", layer)] = (mix_out, (layer // 2,), _down_tn(seq, mix_out.shape[1], d))
        needs[("w1", layer, 1)] = (ffn_w1, (layer, 1), tn_up)
        needs[("w2", layer, 1)] = (ffn_w2, (layer, 1), tn_down)
    ready = {}

    def weight(key):
        if key not in ready:
            ready[key] = _weight_bf16(*needs[key])
        return ready[key]

    def hosted(call, w_key, tiles=1):
        w = weight(w_key)
        steps = (n_lat // _row_tile(seq)) * (w.shape[0] // tiles)
        budget = HOST_CAST_FRACTION * steps * tiles * w.shape[1] * w.shape[2] * BF16_BYTES
        keys = []
        for key, (src, _, _) in needs.items():
            size = src.shape[-2] * src.shape[-1] * F32_BYTES
            if key not in ready and size <= budget:
                keys.append(key)
                budget -= size
        out, casts = call([needs[k] for k in keys])
        ready.update(zip(keys, casts))
        return out

    def ffn(xs, mod, gain, layer, idx, rows3, gate_row, rpm, latent):
        h = _adaln(xs, mod, gain, rows3, rpm)
        k1, k2 = ("w1", layer, idx), ("w2", layer, idx)
        if latent:
            a = hosted(lambda jobs: _ffn_up(h, weight(k1), jobs), k1, tiles=2)
            return hosted(lambda jobs: _down([a], weight(k2), xs, mod, gate_row, 0.5, rpm, "ffn_down", jobs), k2)
        a, _ = _ffn_up(h, weight(k1))
        return _down([a], weight(k2), xs, mod, gate_row, 0.5, rpm, "ffn_down")[0]

    for layer in range(depth):
        last = layer == depth - 1
        even = layer % 2 == 0
        ctx_needed = even or not last
        gain = norm_g[layer]
        mod_l = mod_all[layer, :batch]
        mod_c = mod_all[layer, batch:batch + 1]

        xl = ffn(xl, mod_l, gain, layer, 0, (0, 0, 1), 2, seq, True)
        if ctx_needed:
            xc = ffn(xc, mod_c, gain, layer, 0, (0, 0, 1), 2, n_ctx, False)

        k_in, k_out = ("in", layer), ("out", layer)
        hl = _adaln(xl, mod_l, gain, (1, 3, 4), seq)
        if even:
            e = layer // 2
            pl_ = hosted(lambda jobs: _proj(hl, weight(k_in), "ab_proj", jobs), k_in)
            pc_, _ = _proj(_adaln(xc, mod_c, gain, (1, 3, 4), n_ctx), weight(k_in), "ab_proj")
            o_attn = _attention(pl_, [pl_, pc_], attn_q_gain[e], attn_k_gain[e], tables, batch, col_q, col_k, col_v)
            s_zero = jnp.zeros((batch, n_hg_heads, HEAD_DIM, HEAD_DIM), F32)
            lbs = [lb_table[dd, layer].reshape(1, hg_w) for dd in range(2)]
            zcols = (col_zf, col_zb)
            sc_dir = []
            o_prev = None
            for dd in range(2):
                o_prev, s_c = _hgrn_scan(pc_, lbs[dd], s_zero, batch, (col_qb, zcols[dd], col_ib, col_gb), dd == 1,
                                         o_prev, hgrn_out_gain[e] if dd == 1 else None)
                sc_dir.append(s_c)
            o_rec_c = o_prev
            o_prev = None
            for dd in range(2):
                o_prev, _ = _hgrn_scan(pl_, lbs[dd], sc_dir[dd], batch, (col_qb, zcols[dd], col_ib, col_gb), dd == 1,
                                       o_prev, hgrn_out_gain[e] if dd == 1 else None)
            mixed = [o_attn, o_prev]
            if not last:
                o_attn_c = _attention(pc_, [pc_], attn_q_gain[e], attn_k_gain[e], None, batch, col_q, col_k, col_v)
                mixed_c = [o_attn_c, o_rec_c]
        else:
            o = layer // 2
            pl_ = hosted(lambda jobs: _proj(hl, weight(k_in), "conv_proj", jobs), k_in)
            mixed = [_gated_conv(pl_, conv_w[o], batch)]
            if not last:
                pc_, _ = _proj(_adaln(xc, mod_c, gain, (1, 3, 4), n_ctx), weight(k_in), "conv_proj")
                mixed_c = [_gated_conv(pc_, conv_w[o], batch)]
        x_res = xl
        xl = hosted(lambda jobs: _down(mixed, weight(k_out), x_res, mod_l, 5, 1.0, seq, "mixer_out", jobs), k_out)
        if not last:
            xc = _down(mixed_c, weight(k_out), xc, mod_c, 5, 1.0, n_ctx, "mixer_out")[0]

        xl = ffn(xl, mod_l, gain, layer, 1, (2, 6, 7), 8, seq, True)
        if not last:
            xc = ffn(xc, mod_c, gain, layer, 1, (2, 6, 7), 8, n_ctx, False)

    return xl.reshape(batch, seq, d)
```

```python
import functools

import numpy as np
import jax
import jax.numpy as jnp
from jax import lax
from jax.experimental import pallas as pl
from jax.experimental.pallas import tpu as pltpu

GRID_W = 64
HEAD_DIM = 128
KV_GROUP = 4
ROPE_THETA = 10000.0
ROPE_AXIS_DIM = HEAD_DIM // 2
HGRN_CHUNK = 32
N_MOD = 9
EPS = 1e-6
LOG2_E = 1.4426950408889634

V7X_VMEM_BYTES = 64 * 1024 * 1024
VMEM_LIMIT_BYTES = V7X_VMEM_BYTES * 7 // 8
MATMUL_VMEM_BUDGET = 46 * 1024 * 1024
F32_SUBLANES = 8
BF16_BYTES, F32_BYTES = 2, 4
ADALN_UNROLL = 16
MATMUL_ROWS = 1024
A_LEAD = 3
W_RING = 3
ATTN_Q_ROWS = 256
ATTN_ONES_ROWS = 16
HGRN_BLOCK = 256
HGRN_HEADS_PER_STEP = 8
CAST_BLOCK_BYTES = 8 * 1024 * 1024
CAST_SLAB_ROWS = 16
HOST_CAST_FRACTION = 0.2

BF16 = jnp.bfloat16
F32 = jnp.float32


def _params(*sem):
    return pltpu.CompilerParams(dimension_semantics=sem, vmem_limit_bytes=VMEM_LIMIT_BYTES)


def _pick(n, candidates):
    for c in candidates:
        if n % c == 0:
            return c
    return n


def _mod_kernel(c_ref, w_ref, b_ref, o_ref):
    a = jax.nn.silu(c_ref[...]).astype(BF16)
    o_ref[0] = jnp.dot(a, w_ref[0].astype(BF16), preferred_element_type=F32) + b_ref[0]


def _mod_all(cc, mod_w, mod_b):
    depth, d, n = mod_w.shape
    rows = cc.shape[0]
    tn = _pick(n, (512, 256, 128))
    return pl.pallas_call(
        _mod_kernel,
        grid=(depth, n // tn),
        in_specs=[
            pl.BlockSpec((rows, d), lambda l, j: (0, 0)),
            pl.BlockSpec((1, d, tn), lambda l, j: (l, 0, j)),
            pl.BlockSpec((1, 1, tn), lambda l, j: (l, 0, j)),
        ],
        out_specs=pl.BlockSpec((1, rows, tn), lambda l, j: (l, 0, j)),
        out_shape=jax.ShapeDtypeStruct((depth, rows, n), F32),
        compiler_params=_params("parallel", "parallel"),
        name="mod_proj",
    )(cc, mod_w, mod_b.reshape(depth, 1, n))


def _cast_rows(src_ref, dst_ref):
    tn = dst_ref.shape[2]
    for t in range(dst_ref.shape[0]):
        dst_ref[t] = src_ref[:, t * tn:(t + 1) * tn].astype(dst_ref.dtype)


def _weight_bf16(w, lead, tn):
    r, c = w.shape[-2:]
    tr = _pick(r, tuple(t for t in (2048, 1024, 512, 256, 128, 64, 32, 16) if t * c * F32_BYTES <= CAST_BLOCK_BYTES))
    return pl.pallas_call(
        _cast_rows,
        grid=(r // tr,),
        in_specs=[pl.BlockSpec((None,) * len(lead) + (tr, c), lambda i: tuple(lead) + (i, 0))],
        out_specs=pl.BlockSpec((c // tn, tr, tn), lambda i: (0, i, 0)),
        out_shape=jax.ShapeDtypeStruct((c // tn, r, tn), BF16),
        compiler_params=_params("parallel"),
        name="weight_cast",
    )(w)


def _call_with_casts(kernel_fn, grid, in_specs, out_specs, out_shape, args, jobs, name, scratch_shapes=()):
    n_in, n_out, n_jobs = len(in_specs), len(out_specs), len(jobs)
    steps, nj = grid[0] * grid[1], grid[1]
    in_specs, out_specs, out_shape, args = list(in_specs), list(out_specs), list(out_shape), list(args)
    for w, lead, tn in jobs:
        r, c = w.shape[-2:]
        slab = next(t for t in range(CAST_SLAB_ROWS, r + 1, CAST_SLAB_ROWS) if r % t == 0 and r // t <= steps)

        def slab_idx(i, j, last=r // slab - 1):
            return jnp.minimum(i * nj + j, last)

        in_specs.append(pl.BlockSpec((None,) * len(lead) + (slab, c),
                                     lambda i, j, lead=tuple(lead), f=slab_idx: lead + (f(i, j), 0)))
        out_specs.append(pl.BlockSpec((c // tn, slab, tn), lambda i, j, f=slab_idx: (0, f(i, j), 0)))
        out_shape.append(jax.ShapeDtypeStruct((c // tn, r, tn), BF16))
        args.append(w)

    def body(*refs):
        ins, srcs = refs[:n_in], refs[n_in:n_in + n_jobs]
        outs = refs[n_in + n_jobs:n_in + n_jobs + n_out]
        dsts = refs[n_in + n_jobs + n_out:n_in + 2 * n_jobs + n_out]
        for src, dst in zip(srcs, dsts):
            _cast_rows(src, dst)
        kernel_fn(*ins, *outs, *refs[n_in + 2 * n_jobs + n_out:])

    res = pl.pallas_call(
        body,
        grid=grid,
        in_specs=in_specs,
        out_specs=out_specs,
        out_shape=out_shape,
        scratch_shapes=list(scratch_shapes),
        compiler_params=_params("arbitrary" if jobs or scratch_shapes else "parallel", "arbitrary"),
        name=name,
    )(*args)
    return res[:n_out], res[n_out:]


def _adaln_kernel(x_ref, mod_ref, gain_ref, o_ref, *, rows3):
    gain_row, shift_row, scale_row = rows3
    gs = gain_ref[gain_row:gain_row + 1, :] * (1.0 + mod_ref[0, scale_row:scale_row + 1, :])
    shift = mod_ref[0, shift_row:shift_row + 1, :]

    def body(r, carry):
        sl = pl.ds(pl.multiple_of(r * F32_SUBLANES, F32_SUBLANES), F32_SUBLANES)
        x = x_ref[sl, :]
        y = x * lax.rsqrt(jnp.mean(x * x, axis=-1, keepdims=True) + EPS)
        o_ref[sl, :] = (y * gs + shift).astype(o_ref.dtype)
        return carry

    lax.fori_loop(0, x_ref.shape[0] // F32_SUBLANES, body, 0, unroll=ADALN_UNROLL)


def _adaln(x2d, mod, gain, rows3, rows_per_mod):
    m, d = x2d.shape
    tr = _pick(rows_per_mod, (512, 256, 128, 64, 32, 16, 8))
    bpm = rows_per_mod // tr
    return pl.pallas_call(
        functools.partial(_adaln_kernel, rows3=rows3),
        grid=(m // tr,),
        in_specs=[
            pl.BlockSpec((tr, d), lambda i: (i, 0)),
            pl.BlockSpec((1, N_MOD, d), lambda i: (i // bpm, 0, 0)),
            pl.BlockSpec(gain.shape, lambda i: (0, 0)),
        ],
        out_specs=pl.BlockSpec((tr, d), lambda i: (i, 0)),
        out_shape=jax.ShapeDtypeStruct((m, d), BF16),
        compiler_params=_params("parallel"),
        name="adaln",
    )(x2d, mod, gain)


def _row_tile(rows):
    return _pick(rows, (MATMUL_ROWS, 512, 256, 128, 64, 32, 16, 8))


def _col_tile(tm, n, a_bytes_per_row, w_bytes_per_col, io_bytes_per_elem):
    for tn in (1024, 512, 256, 128):
        vmem = 2 * (tm * a_bytes_per_row + tn * w_bytes_per_col + tm * tn * io_bytes_per_elem)
        if n % tn == 0 and vmem <= MATMUL_VMEM_BUDGET:
            return tn
    return 128


def _up_tn(rows, d, f):
    return _col_tile(_row_tile(rows), f, BF16_BYTES * d, 2 * BF16_BYTES * d, BF16_BYTES)


def _proj_tn(rows, d, n):
    return _col_tile(_row_tile(rows), n, BF16_BYTES * d, BF16_BYTES * d, F32_BYTES)


def _down_tn(rows, k, d):
    return _col_tile(_row_tile(rows), d, BF16_BYTES * k, BF16_BYTES * k, 2 * F32_BYTES)


def _row_block_ring(a_hbms, a_rings, a_sem):
    blk, nblk = pl.program_id(0), pl.num_programs(0)
    col, ncol = pl.program_id(1), pl.num_programs(1)
    tm = a_rings[0].shape[1]

    def copies(b):
        rows = pl.ds(pl.multiple_of(b * tm, tm), tm)
        return [pltpu.make_async_copy(a.at[rows, :], ring.at[b % 2], a_sem.at[b % 2, r])
                for r, (a, ring) in enumerate(zip(a_hbms, a_rings))]

    @pl.when((blk == 0) & (col == 0))
    def _():
        for c in copies(0):
            c.start()

    @pl.when((col == jnp.maximum(ncol - A_LEAD, 0)) & (blk + 1 < nblk))
    def _():
        for c in copies(blk + 1):
            c.start()

    @pl.when(col == 0)
    def _():
        for c in copies(blk):
            c.wait()

    return [ring.at[blk % 2] for ring in a_rings]


def _up_kernel(h_hbm, wg_ref, wu_ref, o_ref, h_ring, h_sem):
    h = _row_block_ring([h_hbm], [h_ring], h_sem)[0][...]
    g = jnp.dot(h, wg_ref[...], preferred_element_type=F32)
    u = jnp.dot(h, wu_ref[...], preferred_element_type=F32)
    o_ref[...] = (jax.nn.silu(g) * u).astype(o_ref.dtype)


def _proj_kernel(h_ref, w_ref, o_ref):
    o_ref[...] = jnp.dot(h_ref[...], w_ref[...], preferred_element_type=F32).astype(o_ref.dtype)


def _ffn_up(h, w1, jobs=()):
    m, d = h.shape
    tn = w1.shape[2]
    nj = w1.shape[0] // 2
    tm = _row_tile(m)
    (a,), casts = _call_with_casts(
        _up_kernel,
        (m // tm, nj),
        [
            pl.BlockSpec(memory_space=pl.ANY),
            pl.BlockSpec((None, d, tn), lambda i, j: (j, 0, 0)),
            pl.BlockSpec((None, d, tn), lambda i, j: (j + nj, 0, 0)),
        ],
        [pl.BlockSpec((tm, tn), lambda i, j: (i, j))],
        [jax.ShapeDtypeStruct((m, nj * tn), BF16)],
        [h, w1, w1], jobs, "ffn_up",
        scratch_shapes=[pltpu.VMEM((2, tm, d), BF16), pltpu.SemaphoreType.DMA((2, 1))])
    return a, casts


def _proj(h, w, name, jobs=()):
    m, d = h.shape
    nj, _, tn = w.shape
    tm = _row_tile(m)
    (p,), casts = _call_with_casts(
        _proj_kernel,
        (m // tm, nj),
        [
            pl.BlockSpec((tm, d), lambda i, j: (i, 0)),
            pl.BlockSpec((None, d, tn), lambda i, j: (j, 0, 0)),
        ],
        [pl.BlockSpec((tm, tn), lambda i, j: (i, j))],
        [jax.ShapeDtypeStruct((m, nj * tn), F32)],
        [h, w], jobs, name)
    return p, casts


def _down_kernel(*refs, n_a, coef, steps):
    a_hbms = refs[:n_a]
    w_hbm, x_hbm, gate_ref, o_ref = refs[n_a:n_a + 4]
    a_rings = refs[n_a + 4:2 * n_a + 4]
    w_ring, x_ring, a_sem, w_sem, x_sem = refs[2 * n_a + 4:]
    blk, nj = pl.program_id(0), pl.num_programs(1)
    step = blk * nj + pl.program_id(1)
    kk = a_hbms[0].shape[1]
    tm, tn = o_ref.shape
    a_blocks = _row_block_ring(a_hbms, a_rings, a_sem)

    def copies(s):
        slot = s % W_RING
        rows = pl.ds(pl.multiple_of((s // nj) * tm, tm), tm)
        cols = pl.ds(pl.multiple_of((s % nj) * tn, tn), tn)
        return (pltpu.make_async_copy(w_hbm.at[s % nj], w_ring.at[slot], w_sem.at[slot]),
                pltpu.make_async_copy(x_hbm.at[rows, cols], x_ring.at[slot], x_sem.at[slot]))

    @pl.when(step == 0)
    def _():
        for s in range(min(W_RING - 1, steps)):
            for c in copies(s):
                c.start()

    @pl.when(step + (W_RING - 1) < steps)
    def _():
        for c in copies(step + (W_RING - 1)):
            c.start()

    for c in copies(step):
        c.wait()
    w_tile = w_ring.at[step % W_RING]
    acc = jnp.dot(a_blocks[0][...], w_tile[0:kk, :], preferred_element_type=F32)
    for r in range(1, n_a):
        acc = acc + jnp.dot(a_blocks[r][...], w_tile[r * kk:(r + 1) * kk, :], preferred_element_type=F32)
    gate = gate_ref[0, pl.program_id(1)]
    if coef != 1.0:
        gate = coef * gate
    o_ref[...] = x_ring[step % W_RING] + gate * acc


def _down(a_list, w, x2d, mod, gate_row, coef, rows_per_mod, name, jobs=()):
    m, d = x2d.shape
    n_a = len(a_list)
    kk = a_list[0].shape[1]
    nj, _, tn = w.shape
    tm = _row_tile(rows_per_mod)
    bpm = rows_per_mod // tm
    gate = mod[:, gate_row].reshape(mod.shape[0], nj, 1, tn)
    in_specs = [pl.BlockSpec(memory_space=pl.ANY)] * (n_a + 2)
    in_specs.append(pl.BlockSpec((1, nj, 1, tn), lambda i, j: (i // bpm, 0, 0, 0)))
    grid = (m // tm, nj)
    (y,), casts = _call_with_casts(
        functools.partial(_down_kernel, n_a=n_a, coef=coef, steps=grid[0] * grid[1]),
        grid,
        in_specs,
        [pl.BlockSpec((tm, tn), lambda i, j: (i, j))],
        [jax.ShapeDtypeStruct((m, d), F32)],
        [*a_list, w, x2d, gate], jobs, name,
        scratch_shapes=[pltpu.VMEM((2, tm, kk), BF16)] * n_a
        + [pltpu.VMEM((W_RING,) + w.shape[1:], BF16), pltpu.VMEM((W_RING, tm, tn), F32),
           pltpu.SemaphoreType.DMA((2, n_a)), pltpu.SemaphoreType.DMA((W_RING,)), pltpu.SemaphoreType.DMA((W_RING,))])
    return y, casts


def _head_norm(x, gain):
    return x * lax.rsqrt(jnp.mean(x * x, axis=-1, keepdims=True) + EPS) * gain


def _rope(x, c, s_hi, s_lo):
    return x * c + pltpu.roll(x, HEAD_DIM - ROPE_AXIS_DIM // 2, 1) * s_hi + pltpu.roll(x, ROPE_AXIS_DIM // 2, 1) * s_lo


def _attn_kernel(*refs, kv_rows, rope):
    n_kv = len(kv_rows)
    q_ref = refs[0]
    kv_refs = refs[1:1 + 2 * n_kv]
    qg_ref, kg_ref = refs[1 + 2 * n_kv:3 + 2 * n_kv]
    pos = 3 + 2 * n_kv
    if rope:
        cq_ref, hq_ref, lq_ref, ck_ref, hk_ref, lk_ref = refs[pos:pos + 6]
        pos += 6
    o_ref, k_scr, vt_scr = refs[pos:pos + 3]
    tq = q_ref.shape[0]

    @pl.when(pl.program_id(2) == 0)
    def _():
        off = 0
        for part, rows in enumerate(kv_rows):
            kn = _head_norm(kv_refs[2 * part][...], kg_ref[...])
            if rope and part == 0:
                kn = _rope(kn, ck_ref[...], hk_ref[...], lk_ref[...])
            k_scr[off:off + rows, :] = kn.astype(BF16)
            vt_scr[:HEAD_DIM, off:off + rows] = kv_refs[2 * part + 1][...].T.astype(BF16)
            off += rows
        vt_scr[HEAD_DIM:, :] = jnp.ones((vt_scr.shape[0] - HEAD_DIM, off), BF16)

    scale = HEAD_DIM ** -0.5 * LOG2_E
    qs = []
    for g in range(KV_GROUP):
        q = _head_norm(q_ref[:, g * HEAD_DIM:(g + 1) * HEAD_DIM], qg_ref[...])
        if rope:
            q = _rope(q, cq_ref[...], hq_ref[...], lq_ref[...])
        qs.append((q * scale).astype(BF16))
    q_all = jnp.concatenate(qs, axis=0)
    s_t = lax.dot_general(k_scr[...], q_all, (((1,), (1,)), ((), ())), preferred_element_type=F32)
    p_t = jnp.exp2(s_t - jnp.max(s_t, axis=0, keepdims=True)).astype(BF16)
    o_t = jnp.dot(vt_scr[...], p_t, preferred_element_type=F32)
    o = (o_t[:HEAD_DIM] / o_t[HEAD_DIM:HEAD_DIM + 1]).T
    for g in range(KV_GROUP):
        o_ref[:, g * HEAD_DIM:(g + 1) * HEAD_DIM] = o[g * tq:(g + 1) * tq].astype(o_ref.dtype)


def _attention(pq, kv_sources, q_gain, k_gain, tables, batch, q_col0, k_col0, v_col0):
    t = pq.shape[0] // batch
    n_kv_heads = (k_col0 - q_col0) // (KV_GROUP * HEAD_DIM)
    tq = _pick(t, (ATTN_Q_ROWS, 256, 128, 64, 32, 16, 8))
    nq = t // tq
    gw = KV_GROUP * HEAD_DIM
    kv_rows = tuple(src.shape[0] // batch for src in kv_sources)
    rope = tables is not None
    in_specs = [pl.BlockSpec((tq, gw), lambda b, h, i: (b * nq + i, q_col0 // gw + h))]
    args = [pq]
    for src, rows in zip(kv_sources, kv_rows):
        in_specs.append(pl.BlockSpec((rows, HEAD_DIM), lambda b, h, i: (b, k_col0 // HEAD_DIM + h)))
        in_specs.append(pl.BlockSpec((rows, HEAD_DIM), lambda b, h, i: (b, v_col0 // HEAD_DIM + h)))
        args += [src, src]
    in_specs += [pl.BlockSpec((1, HEAD_DIM), lambda b, h, i: (0, 0))] * 2
    args += [q_gain.reshape(1, HEAD_DIM), k_gain.reshape(1, HEAD_DIM)]
    if rope:
        in_specs += [pl.BlockSpec((tq, HEAD_DIM), lambda b, h, i: (i, 0))] * 3
        in_specs += [pl.BlockSpec((t, HEAD_DIM), lambda b, h, i: (0, 0))] * 3
        args += list(tables) * 2
    return pl.pallas_call(
        functools.partial(_attn_kernel, kv_rows=kv_rows, rope=rope),
        grid=(batch, n_kv_heads, nq),
        in_specs=in_specs,
        out_specs=pl.BlockSpec((tq, gw), lambda b, h, i: (b * nq + i, h)),
        out_shape=jax.ShapeDtypeStruct((batch * t, n_kv_heads * gw), BF16),
        scratch_shapes=[pltpu.VMEM((sum(kv_rows), HEAD_DIM), BF16),
                        pltpu.VMEM((HEAD_DIM + ATTN_ONES_ROWS, sum(kv_rows)), BF16)],
        compiler_params=_params("parallel", "parallel", "arbitrary"),
        name="gqa_rope" if rope else "gqa_ctx",
    )(*args)


def _rope_tables(t):
    rows = t // GRID_W
    row_ids = np.repeat(np.arange(rows, dtype=np.float32), GRID_W)
    col_ids = np.tile(np.arange(GRID_W, dtype=np.float32), rows)
    inv_freq = jnp.asarray(ROPE_THETA, F32) ** (-jnp.arange(0, ROPE_AXIS_DIM, 2, dtype=F32) / ROPE_AXIS_DIM)
    ang_r = jnp.asarray(row_ids)[:, None] * inv_freq
    ang_c = jnp.asarray(col_ids)[:, None] * inv_freq
    zero = jnp.zeros_like(ang_r)
    cos = jnp.concatenate([jnp.cos(ang_r), jnp.cos(ang_r), jnp.cos(ang_c), jnp.cos(ang_c)], axis=1)
    s_hi = jnp.concatenate([-jnp.sin(ang_r), zero, -jnp.sin(ang_c), zero], axis=1)
    s_lo = jnp.concatenate([zero, jnp.sin(ang_r), zero, jnp.sin(ang_c)], axis=1)
    return cos, s_hi, s_lo


def _chunk_tri(rev):
    idx = np.arange(HGRN_BLOCK)
    same = (idx[:, None] // HGRN_CHUNK) == (idx[None, :] // HGRN_CHUNK)
    tri = (idx[None, :] >= idx[:, None]) if rev else (idx[None, :] <= idx[:, None])
    return jnp.asarray((same & tri).astype(np.float32), BF16)


def _chunk_cols():
    idx = np.arange(HGRN_BLOCK)
    col = np.arange((HGRN_BLOCK // HGRN_CHUNK) * HEAD_DIM)
    return jnp.asarray(((idx[:, None] // HGRN_CHUNK) == (col[None, :] // HEAD_DIM)).astype(np.float32), BF16)


def _hgrn_kernel(*refs, rev, final, n_heads):
    q_ref, z_ref, v_ref, lb_ref, tri_ref, cmask_ref, s0_ref = refs[:7]
    pos = 7
    if final:
        oprev_ref, g_ref, gain_ref = refs[7:10]
        pos = 10
    o_ref, sfin_ref, st_scr = refs[pos:pos + 3]
    n_chunks = HGRN_BLOCK // HGRN_CHUNK
    ref_row = HGRN_CHUNK - 1 - HGRN_CHUNK // 2 if rev else HGRN_CHUNK // 2
    end_row = 0 if rev else HGRN_CHUNK - 1

    @pl.when(pl.program_id(2) == 0)
    def _():
        st_scr[...] = s0_ref[0]

    gw = n_heads * HEAD_DIM
    tri = tri_ref[...]
    in_chunk = tri > 0
    z = z_ref[...]
    lb = lb_ref[...]
    sig = jax.nn.sigmoid(z)
    log_f = jnp.log(lb + (1.0 - lb) * sig)
    k = (1.0 - lb) * (1.0 - sig)
    q = jax.nn.silu(q_ref[...])
    v = v_ref[...]
    g1 = log_f.astype(BF16)
    r1 = log_f - g1.astype(F32)
    g2 = r1.astype(BF16)
    g3 = (r1 - g2.astype(F32)).astype(BF16)
    cum = (jnp.dot(tri, g1, preferred_element_type=F32) + jnp.dot(tri, g2, preferred_element_type=F32)
           + jnp.dot(tri, g3, preferred_element_type=F32))
    shape3 = (n_chunks, HGRN_CHUNK, gw)
    cum3 = cum.reshape(shape3)
    ref3 = cum3[:, ref_row:ref_row + 1, :]
    end3 = cum3[:, end_row:end_row + 1, :]
    rel = cum3 - ref3
    qa3 = q.reshape(shape3) * jnp.exp(rel)
    ka3 = k.reshape(shape3) * jnp.exp(-rel)
    qa = qa3.reshape(cum.shape).astype(BF16)
    ka = ka3.reshape(cum.shape).astype(BF16)
    qe = (qa3 * jnp.exp(ref3)).reshape(cum.shape).astype(BF16)
    kd = (ka3 * jnp.exp(end3 - ref3)).reshape(cum.shape).astype(BF16)
    dec = jnp.exp(end3)
    vb = v.astype(BF16)
    chunk_cols = cmask_ref[...]

    heads = []
    for h in range(n_heads):
        cols = slice(h * HEAD_DIM, (h + 1) * HEAD_DIM)
        attn = lax.dot_general(qa[:, cols], ka[:, cols], (((1,), (1,)), ((), ())), preferred_element_type=F32)
        attn = jnp.where(in_chunk, attn, 0.0)
        o_intra = jnp.dot(attn.astype(BF16), vb[:, cols], preferred_element_type=F32)
        kd_wide = jnp.concatenate([kd[:, cols]] * n_chunks, axis=1) * chunk_cols
        upd = jnp.dot(v[:, cols].T.astype(BF16), kd_wide, preferred_element_type=F32)
        heads.append(dict(o_intra=o_intra, upd=upd, state=st_scr[h]))

    order = range(n_chunks - 1, -1, -1) if rev else range(n_chunks)
    for c in order:
        rows = slice(c * HGRN_CHUNK, (c + 1) * HGRN_CHUNK)
        for h, hd in enumerate(heads):
            cols = slice(h * HEAD_DIM, (h + 1) * HEAD_DIM)
            inter = lax.dot_general(qe[rows, cols], hd["state"].astype(BF16), (((1,), (1,)), ((), ())),
                                    preferred_element_type=F32)
            o = hd["o_intra"][rows] + inter
            hd["state"] = hd["state"] * dec[c][:, cols] + hd["upd"][:, c * HEAD_DIM:(c + 1) * HEAD_DIM]
            if final:
                o = o + oprev_ref[rows, cols]
                y = o * lax.rsqrt(jnp.mean(o * o, axis=-1, keepdims=True) + EPS) * gain_ref[...]
                o = y * jax.nn.silu(g_ref[rows, cols])
            o_ref[rows, cols] = o.astype(o_ref.dtype)
    for h, hd in enumerate(heads):
        st_scr[h] = hd["state"]
        sfin_ref[0, h] = hd["state"]


def _hgrn_scan(p, lb_row, s0, batch, cols, rev, o_prev=None, out_gain=None):
    t = p.shape[0] // batch
    nsb = t // HGRN_BLOCK
    n_heads = s0.shape[1]
    hps = next(g for g in (HGRN_HEADS_PER_STEP, 4, 2, 1)
               if n_heads % g == 0 and all(c % (g * HEAD_DIM) == 0 for c in cols))
    gw = hps * HEAD_DIM
    final = o_prev is not None
    q0, z0, v0, g0 = (c // gw for c in cols)

    def row_blk(b, s):
        return b * nsb + (nsb - 1 - s if rev else s)

    blk = (HGRN_BLOCK, gw)
    st_blk = (1, hps, HEAD_DIM, HEAD_DIM)
    in_specs = [
        pl.BlockSpec(blk, lambda b, h, s: (row_blk(b, s), q0 + h)),
        pl.BlockSpec(blk, lambda b, h, s: (row_blk(b, s), z0 + h)),
        pl.BlockSpec(blk, lambda b, h, s: (row_blk(b, s), v0 + h)),
        pl.BlockSpec((1, gw), lambda b, h, s: (0, h)),
        pl.BlockSpec((HGRN_BLOCK, HGRN_BLOCK), lambda b, h, s: (0, 0)),
        pl.BlockSpec((HGRN_BLOCK, (HGRN_BLOCK // HGRN_CHUNK) * HEAD_DIM), lambda b, h, s: (0, 0)),
        pl.BlockSpec(st_blk, lambda b, h, s: (b, h, 0, 0)),
    ]
    args = [p, p, p, lb_row, _chunk_tri(rev), _chunk_cols(), s0]
    if final:
        in_specs += [
            pl.BlockSpec(blk, lambda b, h, s: (row_blk(b, s), h)),
            pl.BlockSpec(blk, lambda b, h, s: (row_blk(b, s), g0 + h)),
            pl.BlockSpec((1, HEAD_DIM), lambda b, h, s: (0, 0)),
        ]
        args += [o_prev, p, out_gain.reshape(1, HEAD_DIM)]
    return pl.pallas_call(
        functools.partial(_hgrn_kernel, rev=rev, final=final, n_heads=hps),
        grid=(batch, n_heads // hps, nsb),
        in_specs=in_specs,
        out_specs=[
            pl.BlockSpec(blk, lambda b, h, s: (row_blk(b, s), h)),
            pl.BlockSpec(st_blk, lambda b, h, s: (b, h, 0, 0)),
        ],
        out_shape=[
            jax.ShapeDtypeStruct((batch * t, n_heads * HEAD_DIM), BF16 if final else F32),
            jax.ShapeDtypeStruct(s0.shape, F32),
        ],
        scratch_shapes=[pltpu.VMEM((hps, HEAD_DIM, HEAD_DIM), F32)],
        compiler_params=_params("parallel", "parallel", "arbitrary"),
        name="hgrn_bwd" if rev else "hgrn_fwd",
    )(*args)


def _conv_kernel(b_ref, c_ref, u_ref, w_ref, o_ref):
    u = c_ref[...] * u_ref[...]
    t = u.shape[0]
    row = lax.broadcasted_iota(jnp.int32, u.shape, 0)
    prev = jnp.where(row == 0, 0.0, pltpu.roll(u, 1, 0))
    nxt = jnp.where(row == t - 1, 0.0, pltpu.roll(u, t - 1, 0))
    y = prev * w_ref[0:1, :] + u * w_ref[1:2, :] + nxt * w_ref[2:3, :]
    o_ref[...] = (b_ref[...] * y).astype(o_ref.dtype)


def _gated_conv(p, conv_w, batch):
    d = conv_w.shape[1]
    t = p.shape[0] // batch
    td = _pick(d, (256, 128))
    nd = d // td
    return pl.pallas_call(
        _conv_kernel,
        grid=(batch, nd),
        in_specs=[
            pl.BlockSpec((t, td), lambda b, j: (b, j)),
            pl.BlockSpec((t, td), lambda b, j: (b, nd + j)),
            pl.BlockSpec((t, td), lambda b, j: (b, 2 * nd + j)),
            pl.BlockSpec((conv_w.shape[0], td), lambda b, j: (0, j)),
        ],
        out_specs=pl.BlockSpec((t, td), lambda b, j: (b, j)),
        out_shape=jax.ShapeDtypeStruct((batch * t, d), BF16),
        compiler_params=_params("parallel", "parallel"),
        name="gated_conv3",
    )(p, p, p, conv_w)


def kernel(x, c, ctx, c_ctx, mod_w, mod_b, norm_g, ffn_w1, ffn_w2, ab_w_in, ab_w_out, attn_q_gain, attn_k_gain, hgrn_lb_logits, hgrn_out_gain, conv_w_in, conv_w, conv_w_out):
    batch, seq, d = x.shape
    ctx_len = ctx.shape[1]
    depth = mod_w.shape[0]
    attn_w = d // 2
    kv_w = attn_w // KV_GROUP
    hg_w = d - attn_w
    n_hg_heads = hg_w // HEAD_DIM
    col_q, col_k, col_v = 0, attn_w, attn_w + kv_w
    col_qb = attn_w + 2 * kv_w
    col_zf, col_zb, col_ib, col_gb = (col_qb + hg_w * i for i in range(1, 5))

    lb_table = jnp.cumsum(jax.nn.softmax(hgrn_lb_logits.astype(F32), axis=1), axis=1)
    tables = _rope_tables(seq)

    pad = (-(batch + 1)) % 8
    cc = jnp.concatenate([c, c_ctx[None], jnp.zeros((pad, d), F32)], axis=0)
    mod_all = _mod_all(cc, mod_w, mod_b).reshape(depth, batch + 1 + pad, N_MOD, d)

    xl = x.reshape(batch * seq, d)
    xc = ctx.reshape(batch * ctx_len, d)
    n_ctx = batch * ctx_len

    n_lat = batch * seq
    f_ff = ffn_w2.shape[2]
    tn_up, tn_down = _up_tn(n_lat, d, f_ff), _down_tn(seq, f_ff, d)
    needs = {}
    for layer in range(depth):
        mix_in, mix_out = (ab_w_in, ab_w_out) if layer % 2 == 0 else (conv_w_in, conv_w_out)
        needs[("w1", layer, 0)] = (ffn_w1, (layer, 0), tn_up)
        needs[("w2", layer, 0)] = (ffn_w2, (layer, 0), tn_down)
        needs[("in", layer)] = (mix_in, (layer // 2,), _proj_tn(n_lat, d, mix_in.shape[2]))
        needs[("out", layer)] = (mix_out, (layer // 2,), _down_tn(seq, mix_out.shape[1], d))
        needs[("w1", layer, 1)] = (ffn_w1, (layer, 1), tn_up)
        needs[("w2", layer, 1)] = (ffn_w2, (layer, 1), tn_down)
    ready = {}

    def weight(key):
        if key not in ready:
            ready[key] = _weight_bf16(*needs[key])
        return ready[key]

    def hosted(call, w_key, tiles=1):
        w = weight(w_key)
        steps = (n_lat // _row_tile(seq)) * (w.shape[0] // tiles)
        budget = HOST_CAST_FRACTION * steps * tiles * w.shape[1] * w.shape[2] * BF16_BYTES
        keys = []
        for key, (src, _, _) in needs.items():
            size = src.shape[-2] * src.shape[-1] * F32_BYTES
            if key not in ready and size <= budget:
                keys.append(key)
                budget -= size
        out, casts = call([needs[k] for k in keys])
        ready.update(zip(keys, casts))
        return out

    def ffn(xs, mod, gain, layer, idx, rows3, gate_row, rpm, latent):
        h = _adaln(xs, mod, gain, rows3, rpm)
        k1, k2 = ("w1", layer, idx), ("w2", layer, idx)
        if latent:
            a = hosted(lambda jobs: _ffn_up(h, weight(k1), jobs), k1, tiles=2)
            return hosted(lambda jobs: _down([a], weight(k2), xs, mod, gate_row, 0.5, rpm, "ffn_down", jobs), k2)
        a, _ = _ffn_up(h, weight(k1))
        return _down([a], weight(k2), xs, mod, gate_row, 0.5, rpm, "ffn_down")[0]

    for layer in range(depth):
        last = layer == depth - 1
        even = layer % 2 == 0
        ctx_needed = even or not last
        gain = norm_g[layer]
        mod_l = mod_all[layer, :batch]
        mod_c = mod_all[layer, batch:batch + 1]

        xl = ffn(xl, mod_l, gain, layer, 0, (0, 0, 1), 2, seq, True)
        if ctx_needed:
            xc = ffn(xc, mod_c, gain, layer, 0, (0, 0, 1), 2, n_ctx, False)

        k_in, k_out = ("in", layer), ("out", layer)
        hl = _adaln(xl, mod_l, gain, (1, 3, 4), seq)
        if even:
            e = layer // 2
            pl_ = hosted(lambda jobs: _proj(hl, weight(k_in), "ab_proj", jobs), k_in)
            pc_, _ = _proj(_adaln(xc, mod_c, gain, (1, 3, 4), n_ctx), weight(k_in), "ab_proj")
            o_attn = _attention(pl_, [pl_, pc_], attn_q_gain[e], attn_k_gain[e], tables, batch, col_q, col_k, col_v)
            s_zero = jnp.zeros((batch, n_hg_heads, HEAD_DIM, HEAD_DIM), F32)
            lbs = [lb_table[dd, layer].reshape(1, hg_w) for dd in range(2)]
            zcols = (col_zf, col_zb)
            sc_dir = []
            o_prev = None
            for dd in range(2):
                o_prev, s_c = _hgrn_scan(pc_, lbs[dd], s_zero, batch, (col_qb, zcols[dd], col_ib, col_gb), dd == 1,
                                         o_prev, hgrn_out_gain[e] if dd == 1 else None)
                sc_dir.append(s_c)
            o_rec_c = o_prev
            o_prev = None
            for dd in range(2):
                o_prev, _ = _hgrn_scan(pl_, lbs[dd], sc_dir[dd], batch, (col_qb, zcols[dd], col_ib, col_gb), dd == 1,
                                       o_prev, hgrn_out_gain[e] if dd == 1 else None)
            mixed = [o_attn, o_prev]
            if not last:
                o_attn_c = _attention(pc_, [pc_], attn_q_gain[e], attn_k_gain[e], None, batch, col_q, col_k, col_v)
                mixed_c = [o_attn_c, o_rec_c]
        else:
            o = layer // 2
            pl_ = hosted(lambda jobs: _proj(hl, weight(k_in), "conv_proj", jobs), k_in)
            mixed = [_gated_conv(pl_, conv_w[o], batch)]
            if not last:
                pc_, _ = _proj(_adaln(xc, mod_c, gain, (1, 3, 4), n_ctx), weight(k_in), "conv_proj")
                mixed_c = [_gated_conv(pc_, conv_w[o], batch)]
        x_res = xl
        xl = hosted(lambda jobs: _down(mixed, weight(k_out), x_res, mod_l, 5, 1.0, seq, "mixer_out", jobs), k_out)
        if not last:
            xc = _down(mixed_c, weight(k_out), xc, mod_c, 5, 1.0, n_ctx, "mixer_out")[0]

        xl = ffn(xl, mod_l, gain, layer, 1, (2, 6, 7), 8, seq, True)
        if not last:
            xc = ffn(xc, mod_c, gain, layer, 1, (2, 6, 7), 8, n_ctx, False)

    return xl.reshape(batch, seq, d)
```

```python
import functools

import numpy as np
import jax
import jax.numpy as jnp
from jax import lax
from jax.experimental import pallas as pl
from jax.experimental.pallas import tpu as pltpu

GRID_W = 64
HEAD_DIM = 128
KV_GROUP = 4
ROPE_THETA = 10000.0
ROPE_AXIS_DIM = HEAD_DIM // 2
HGRN_CHUNK = 32
N_MOD = 9
EPS = 1e-6
LOG2_E = 1.4426950408889634

V7X_VMEM_BYTES = 64 * 1024 * 1024
VMEM_LIMIT_BYTES = V7X_VMEM_BYTES * 7 // 8
MATMUL_VMEM_BUDGET = 46 * 1024 * 1024
F32_SUBLANES = 8
BF16_BYTES, F32_BYTES = 2, 4
ADALN_UNROLL = 16
MATMUL_ROWS = 1024
A_LEAD = 3
ROW_BLOCK_DMA_PRIORITY = 1
W_RING = 3
ATTN_Q_ROWS = 256
ATTN_ONES_ROWS = 16
HGRN_BLOCK = 256
HGRN_HEADS_PER_STEP = 8
CAST_BLOCK_BYTES = 8 * 1024 * 1024
CAST_SLAB_ROWS = 16
HOST_CAST_FRACTION = 0.2

BF16 = jnp.bfloat16
F32 = jnp.float32


def _params(*sem):
    return pltpu.CompilerParams(dimension_semantics=sem, vmem_limit_bytes=VMEM_LIMIT_BYTES)


def _pick(n, candidates):
    for c in candidates:
        if n % c == 0:
            return c
    return n


def _mod_kernel(c_ref, w_ref, b_ref, o_ref):
    a = jax.nn.silu(c_ref[...]).astype(BF16)
    o_ref[0] = jnp.dot(a, w_ref[0].astype(BF16), preferred_element_type=F32) + b_ref[0]


def _mod_all(cc, mod_w, mod_b):
    depth, d, n = mod_w.shape
    rows = cc.shape[0]
    tn = _pick(n, (512, 256, 128))
    return pl.pallas_call(
        _mod_kernel,
        grid=(depth, n // tn),
        in_specs=[
            pl.BlockSpec((rows, d), lambda l, j: (0, 0)),
            pl.BlockSpec((1, d, tn), lambda l, j: (l, 0, j)),
            pl.BlockSpec((1, 1, tn), lambda l, j: (l, 0, j)),
        ],
        out_specs=pl.BlockSpec((1, rows, tn), lambda l, j: (l, 0, j)),
        out_shape=jax.ShapeDtypeStruct((depth, rows, n), F32),
        compiler_params=_params("parallel", "parallel"),
        name="mod_proj",
    )(cc, mod_w, mod_b.reshape(depth, 1, n))


def _cast_rows(src_ref, dst_ref):
    tn = dst_ref.shape[2]
    for t in range(dst_ref.shape[0]):
        dst_ref[t] = src_ref[:, t * tn:(t + 1) * tn].astype(dst_ref.dtype)


def _weight_bf16(w, lead, tn):
    r, c = w.shape[-2:]
    tr = _pick(r, tuple(t for t in (2048, 1024, 512, 256, 128, 64, 32, 16) if t * c * F32_BYTES <= CAST_BLOCK_BYTES))
    return pl.pallas_call(
        _cast_rows,
        grid=(r // tr,),
        in_specs=[pl.BlockSpec((None,) * len(lead) + (tr, c), lambda i: tuple(lead) + (i, 0))],
        out_specs=pl.BlockSpec((c // tn, tr, tn), lambda i: (0, i, 0)),
        out_shape=jax.ShapeDtypeStruct((c // tn, r, tn), BF16),
        compiler_params=_params("parallel"),
        name="weight_cast",
    )(w)


def _call_with_casts(kernel_fn, grid, in_specs, out_specs, out_shape, args, jobs, name, scratch_shapes=()):
    n_in, n_out, n_jobs = len(in_specs), len(out_specs), len(jobs)
    steps, nj = grid[0] * grid[1], grid[1]
    in_specs, out_specs, out_shape, args = list(in_specs), list(out_specs), list(out_shape), list(args)
    for w, lead, tn in jobs:
        r, c = w.shape[-2:]
        slab = next(t for t in range(CAST_SLAB_ROWS, r + 1, CAST_SLAB_ROWS) if r % t == 0 and r // t <= steps)

        def slab_idx(i, j, last=r // slab - 1):
            return jnp.minimum(i * nj + j, last)

        in_specs.append(pl.BlockSpec((None,) * len(lead) + (slab, c),
                                     lambda i, j, lead=tuple(lead), f=slab_idx: lead + (f(i, j), 0)))
        out_specs.append(pl.BlockSpec((c // tn, slab, tn), lambda i, j, f=slab_idx: (0, f(i, j), 0)))
        out_shape.append(jax.ShapeDtypeStruct((c // tn, r, tn), BF16))
        args.append(w)

    def body(*refs):
        ins, srcs = refs[:n_in], refs[n_in:n_in + n_jobs]
        outs = refs[n_in + n_jobs:n_in + n_jobs + n_out]
        dsts = refs[n_in + n_jobs + n_out:n_in + 2 * n_jobs + n_out]
        for src, dst in zip(srcs, dsts):
            _cast_rows(src, dst)
        kernel_fn(*ins, *outs, *refs[n_in + 2 * n_jobs + n_out:])

    res = pl.pallas_call(
        body,
        grid=grid,
        in_specs=in_specs,
        out_specs=out_specs,
        out_shape=out_shape,
        scratch_shapes=list(scratch_shapes),
        compiler_params=_params("arbitrary" if jobs or scratch_shapes else "parallel", "arbitrary"),
        name=name,
    )(*args)
    return res[:n_out], res[n_out:]


def _adaln_kernel(x_ref, mod_ref, gain_ref, o_ref, *, rows3):
    gain_row, shift_row, scale_row = rows3
    gs = gain_ref[gain_row:gain_row + 1, :] * (1.0 + mod_ref[0, scale_row:scale_row + 1, :])
    shift = mod_ref[0, shift_row:shift_row + 1, :]

    def body(r, carry):
        sl = pl.ds(pl.multiple_of(r * F32_SUBLANES, F32_SUBLANES), F32_SUBLANES)
        x = x_ref[sl, :]
        y = x * lax.rsqrt(jnp.mean(x * x, axis=-1, keepdims=True) + EPS)
        o_ref[sl, :] = (y * gs + shift).astype(o_ref.dtype)
        return carry

    lax.fori_loop(0, x_ref.shape[0] // F32_SUBLANES, body, 0, unroll=ADALN_UNROLL)


def _adaln(x2d, mod, gain, rows3, rows_per_mod):
    m, d = x2d.shape
    tr = _pick(rows_per_mod, (512, 256, 128, 64, 32, 16, 8))
    bpm = rows_per_mod // tr
    return pl.pallas_call(
        functools.partial(_adaln_kernel, rows3=rows3),
        grid=(m // tr,),
        in_specs=[
            pl.BlockSpec((tr, d), lambda i: (i, 0)),
            pl.BlockSpec((1, N_MOD, d), lambda i: (i // bpm, 0, 0)),
            pl.BlockSpec(gain.shape, lambda i: (0, 0)),
        ],
        out_specs=pl.BlockSpec((tr, d), lambda i: (i, 0)),
        out_shape=jax.ShapeDtypeStruct((m, d), BF16),
        compiler_params=_params("parallel"),
        name="adaln",
    )(x2d, mod, gain)


def _row_tile(rows):
    return _pick(rows, (MATMUL_ROWS, 512, 256, 128, 64, 32, 16, 8))


def _col_tile(tm, n, a_bytes_per_row, w_bytes_per_col, io_bytes_per_elem):
    for tn in (1024, 512, 256, 128):
        vmem = 2 * (tm * a_bytes_per_row + tn * w_bytes_per_col + tm * tn * io_bytes_per_elem)
        if n % tn == 0 and vmem <= MATMUL_VMEM_BUDGET:
            return tn
    return 128


def _up_tn(rows, d, f):
    return _col_tile(_row_tile(rows), f, BF16_BYTES * d, 2 * BF16_BYTES * d, BF16_BYTES)


def _proj_tn(rows, d, n):
    return _col_tile(_row_tile(rows), n, BF16_BYTES * d, BF16_BYTES * d, F32_BYTES)


def _down_tn(rows, k, d):
    return _col_tile(_row_tile(rows), d, BF16_BYTES * k, BF16_BYTES * k, 2 * F32_BYTES)


def _up_kernel(h_ref, wg_ref, wu_ref, o_ref):
    h = h_ref[...]
    g = jnp.dot(h, wg_ref[...], preferred_element_type=F32)
    u = jnp.dot(h, wu_ref[...], preferred_element_type=F32)
    o_ref[...] = (jax.nn.silu(g) * u).astype(o_ref.dtype)


def _proj_kernel(h_ref, w_ref, o_ref):
    o_ref[...] = jnp.dot(h_ref[...], w_ref[...], preferred_element_type=F32).astype(o_ref.dtype)


def _ffn_up(h, w1, jobs=()):
    m, d = h.shape
    tn = w1.shape[2]
    nj = w1.shape[0] // 2
    tm = _row_tile(m)
    (a,), casts = _call_with_casts(
        _up_kernel,
        (m // tm, nj),
        [
            pl.BlockSpec((tm, d), lambda i, j: (i, 0)),
            pl.BlockSpec((None, d, tn), lambda i, j: (j, 0, 0)),
            pl.BlockSpec((None, d, tn), lambda i, j: (j + nj, 0, 0)),
        ],
        [pl.BlockSpec((tm, tn), lambda i, j: (i, j))],
        [jax.ShapeDtypeStruct((m, nj * tn), BF16)],
        [h, w1, w1], jobs, "ffn_up")
    return a, casts


def _proj(h, w, name, jobs=()):
    m, d = h.shape
    nj, _, tn = w.shape
    tm = _row_tile(m)
    (p,), casts = _call_with_casts(
        _proj_kernel,
        (m // tm, nj),
        [
            pl.BlockSpec((tm, d), lambda i, j: (i, 0)),
            pl.BlockSpec((None, d, tn), lambda i, j: (j, 0, 0)),
        ],
        [pl.BlockSpec((tm, tn), lambda i, j: (i, j))],
        [jax.ShapeDtypeStruct((m, nj * tn), F32)],
        [h, w], jobs, name)
    return p, casts


def _down_kernel(*refs, n_a, coef, steps):
    a_hbms = refs[:n_a]
    w_hbm, x_hbm, gate_ref, o_ref = refs[n_a:n_a + 4]
    a_rings = refs[n_a + 4:2 * n_a + 4]
    w_ring, x_ring, a_sem, w_sem, x_sem = refs[2 * n_a + 4:]
    blk, nblk, nj = pl.program_id(0), pl.num_programs(0), pl.num_programs(1)
    step = blk * nj + pl.program_id(1)
    kk = a_hbms[0].shape[1]
    tm, tn = o_ref.shape

    def a_copies(b):
        rows = pl.ds(pl.multiple_of(b * tm, tm), tm)
        return [pltpu.make_async_copy(a_hbms[r].at[rows, :], a_rings[r].at[b % 2], a_sem.at[b % 2, r]) for r in range(n_a)]

    @pl.when(step == 0)
    def _():
        for c in a_copies(0):
            c.start()

    @pl.when((pl.program_id(1) == jnp.maximum(nj - A_LEAD, 0)) & (blk + 1 < nblk))
    def _():
        for c in a_copies(blk + 1):
            c.start(priority=ROW_BLOCK_DMA_PRIORITY)

    @pl.when(pl.program_id(1) == 0)
    def _():
        for c in a_copies(blk):
            c.wait()

    def copies(s):
        slot = s % W_RING
        rows = pl.ds(pl.multiple_of((s // nj) * tm, tm), tm)
        cols = pl.ds(pl.multiple_of((s % nj) * tn, tn), tn)
        return (pltpu.make_async_copy(w_hbm.at[s % nj], w_ring.at[slot], w_sem.at[slot]),
                pltpu.make_async_copy(x_hbm.at[rows, cols], x_ring.at[slot], x_sem.at[slot]))

    @pl.when(step == 0)
    def _():
        for s in range(min(W_RING - 1, steps)):
            for c in copies(s):
                c.start()

    @pl.when(step + (W_RING - 1) < steps)
    def _():
        for c in copies(step + (W_RING - 1)):
            c.start()

    for c in copies(step):
        c.wait()
    w_tile = w_ring.at[step % W_RING]
    acc = jnp.dot(a_rings[0][blk % 2], w_tile[0:kk, :], preferred_element_type=F32)
    for r in range(1, n_a):
        acc = acc + jnp.dot(a_rings[r][blk % 2], w_tile[r * kk:(r + 1) * kk, :], preferred_element_type=F32)
    gate = gate_ref[0, pl.program_id(1)]
    if coef != 1.0:
        gate = coef * gate
    o_ref[...] = x_ring[step % W_RING] + gate * acc


def _down(a_list, w, x2d, mod, gate_row, coef, rows_per_mod, name, jobs=()):
    m, d = x2d.shape
    n_a = len(a_list)
    kk = a_list[0].shape[1]
    nj, _, tn = w.shape
    tm = _row_tile(rows_per_mod)
    bpm = rows_per_mod // tm
    gate = mod[:, gate_row].reshape(mod.shape[0], nj, 1, tn)
    in_specs = [pl.BlockSpec(memory_space=pl.ANY)] * (n_a + 2)
    in_specs.append(pl.BlockSpec((1, nj, 1, tn), lambda i, j: (i // bpm, 0, 0, 0)))
    grid = (m // tm, nj)
    (y,), casts = _call_with_casts(
        functools.partial(_down_kernel, n_a=n_a, coef=coef, steps=grid[0] * grid[1]),
        grid,
        in_specs,
        [pl.BlockSpec((tm, tn), lambda i, j: (i, j))],
        [jax.ShapeDtypeStruct((m, d), F32)],
        [*a_list, w, x2d, gate], jobs, name,
        scratch_shapes=[pltpu.VMEM((2, tm, kk), BF16)] * n_a
        + [pltpu.VMEM((W_RING,) + w.shape[1:], BF16), pltpu.VMEM((W_RING, tm, tn), F32),
           pltpu.SemaphoreType.DMA((2, n_a)), pltpu.SemaphoreType.DMA((W_RING,)), pltpu.SemaphoreType.DMA((W_RING,))])
    return y, casts


def _head_norm(x, gain):
    return x * lax.rsqrt(jnp.mean(x * x, axis=-1, keepdims=True) + EPS) * gain


def _rope(x, c, s_hi, s_lo):
    return x * c + pltpu.roll(x, HEAD_DIM - ROPE_AXIS_DIM // 2, 1) * s_hi + pltpu.roll(x, ROPE_AXIS_DIM // 2, 1) * s_lo


def _attn_kernel(*refs, kv_rows, rope):
    n_kv = len(kv_rows)
    q_ref = refs[0]
    kv_refs = refs[1:1 + 2 * n_kv]
    qg_ref, kg_ref = refs[1 + 2 * n_kv:3 + 2 * n_kv]
    pos = 3 + 2 * n_kv
    if rope:
        cq_ref, hq_ref, lq_ref, ck_ref, hk_ref, lk_ref = refs[pos:pos + 6]
        pos += 6
    o_ref, k_scr, vt_scr = refs[pos:pos + 3]
    tq = q_ref.shape[0]

    @pl.when(pl.program_id(2) == 0)
    def _():
        off = 0
        for part, rows in enumerate(kv_rows):
            kn = _head_norm(kv_refs[2 * part][...], kg_ref[...])
            if rope and part == 0:
                kn = _rope(kn, ck_ref[...], hk_ref[...], lk_ref[...])
            k_scr[off:off + rows, :] = kn.astype(BF16)
            vt_scr[:HEAD_DIM, off:off + rows] = kv_refs[2 * part + 1][...].T.astype(BF16)
            off += rows
        vt_scr[HEAD_DIM:, :] = jnp.ones((vt_scr.shape[0] - HEAD_DIM, off), BF16)

    scale = HEAD_DIM ** -0.5 * LOG2_E
    qs = []
    for g in range(KV_GROUP):
        q = _head_norm(q_ref[:, g * HEAD_DIM:(g + 1) * HEAD_DIM], qg_ref[...])
        if rope:
            q = _rope(q, cq_ref[...], hq_ref[...], lq_ref[...])
        qs.append((q * scale).astype(BF16))
    q_all = jnp.concatenate(qs, axis=0)
    s_t = lax.dot_general(k_scr[...], q_all, (((1,), (1,)), ((), ())), preferred_element_type=F32)
    p_t = jnp.exp2(s_t - jnp.max(s_t, axis=0, keepdims=True)).astype(BF16)
    o_t = jnp.dot(vt_scr[...], p_t, preferred_element_type=F32)
    o = (o_t[:HEAD_DIM] / o_t[HEAD_DIM:HEAD_DIM + 1]).T
    for g in range(KV_GROUP):
        o_ref[:, g * HEAD_DIM:(g + 1) * HEAD_DIM] = o[g * tq:(g + 1) * tq].astype(o_ref.dtype)


def _attention(pq, kv_sources, q_gain, k_gain, tables, batch, q_col0, k_col0, v_col0):
    t = pq.shape[0] // batch
    n_kv_heads = (k_col0 - q_col0) // (KV_GROUP * HEAD_DIM)
    tq = _pick(t, (ATTN_Q_ROWS, 256, 128, 64, 32, 16, 8))
    nq = t // tq
    gw = KV_GROUP * HEAD_DIM
    kv_rows = tuple(src.shape[0] // batch for src in kv_sources)
    rope = tables is not None
    in_specs = [pl.BlockSpec((tq, gw), lambda b, h, i: (b * nq + i, q_col0 // gw + h))]
    args = [pq]
    for src, rows in zip(kv_sources, kv_rows):
        in_specs.append(pl.BlockSpec((rows, HEAD_DIM), lambda b, h, i: (b, k_col0 // HEAD_DIM + h)))
        in_specs.append(pl.BlockSpec((rows, HEAD_DIM), lambda b, h, i: (b, v_col0 // HEAD_DIM + h)))
        args += [src, src]
    in_specs += [pl.BlockSpec((1, HEAD_DIM), lambda b, h, i: (0, 0))] * 2
    args += [q_gain.reshape(1, HEAD_DIM), k_gain.reshape(1, HEAD_DIM)]
    if rope:
        in_specs += [pl.BlockSpec((tq, HEAD_DIM), lambda b, h, i: (i, 0))] * 3
        in_specs += [pl.BlockSpec((t, HEAD_DIM), lambda b, h, i: (0, 0))] * 3
        args += list(tables) * 2
    return pl.pallas_call(
        functools.partial(_attn_kernel, kv_rows=kv_rows, rope=rope),
        grid=(batch, n_kv_heads, nq),
        in_specs=in_specs,
        out_specs=pl.BlockSpec((tq, gw), lambda b, h, i: (b * nq + i, h)),
        out_shape=jax.ShapeDtypeStruct((batch * t, n_kv_heads * gw), BF16),
        scratch_shapes=[pltpu.VMEM((sum(kv_rows), HEAD_DIM), BF16),
                        pltpu.VMEM((HEAD_DIM + ATTN_ONES_ROWS, sum(kv_rows)), BF16)],
        compiler_params=_params("parallel", "parallel", "arbitrary"),
        name="gqa_rope" if rope else "gqa_ctx",
    )(*args)


def _rope_tables(t):
    rows = t // GRID_W
    row_ids = np.repeat(np.arange(rows, dtype=np.float32), GRID_W)
    col_ids = np.tile(np.arange(GRID_W, dtype=np.float32), rows)
    inv_freq = jnp.asarray(ROPE_THETA, F32) ** (-jnp.arange(0, ROPE_AXIS_DIM, 2, dtype=F32) / ROPE_AXIS_DIM)
    ang_r = jnp.asarray(row_ids)[:, None] * inv_freq
    ang_c = jnp.asarray(col_ids)[:, None] * inv_freq
    zero = jnp.zeros_like(ang_r)
    cos = jnp.concatenate([jnp.cos(ang_r), jnp.cos(ang_r), jnp.cos(ang_c), jnp.cos(ang_c)], axis=1)
    s_hi = jnp.concatenate([-jnp.sin(ang_r), zero, -jnp.sin(ang_c), zero], axis=1)
    s_lo = jnp.concatenate([zero, jnp.sin(ang_r), zero, jnp.sin(ang_c)], axis=1)
    return cos, s_hi, s_lo


def _chunk_tri(rev):
    idx = np.arange(HGRN_BLOCK)
    same = (idx[:, None] // HGRN_CHUNK) == (idx[None, :] // HGRN_CHUNK)
    tri = (idx[None, :] >= idx[:, None]) if rev else (idx[None, :] <= idx[:, None])
    return jnp.asarray((same & tri).astype(np.float32), BF16)


def _chunk_cols():
    idx = np.arange(HGRN_BLOCK)
    col = np.arange((HGRN_BLOCK // HGRN_CHUNK) * HEAD_DIM)
    return jnp.asarray(((idx[:, None] // HGRN_CHUNK) == (col[None, :] // HEAD_DIM)).astype(np.float32), BF16)


def _hgrn_kernel(*refs, rev, final, n_heads):
    q_ref, z_ref, v_ref, lb_ref, tri_ref, cmask_ref, s0_ref = refs[:7]
    pos = 7
    if final:
        oprev_ref, g_ref, gain_ref = refs[7:10]
        pos = 10
    o_ref, sfin_ref, st_scr = refs[pos:pos + 3]
    n_chunks = HGRN_BLOCK // HGRN_CHUNK
    ref_row = HGRN_CHUNK - 1 - HGRN_CHUNK // 2 if rev else HGRN_CHUNK // 2
    end_row = 0 if rev else HGRN_CHUNK - 1

    @pl.when(pl.program_id(2) == 0)
    def _():
        st_scr[...] = s0_ref[0]

    gw = n_heads * HEAD_DIM
    tri = tri_ref[...]
    in_chunk = tri > 0
    z = z_ref[...]
    lb = lb_ref[...]
    sig = jax.nn.sigmoid(z)
    log_f = jnp.log(lb + (1.0 - lb) * sig)
    k = (1.0 - lb) * (1.0 - sig)
    q = jax.nn.silu(q_ref[...])
    v = v_ref[...]
    g1 = log_f.astype(BF16)
    r1 = log_f - g1.astype(F32)
    g2 = r1.astype(BF16)
    g3 = (r1 - g2.astype(F32)).astype(BF16)
    cum = (jnp.dot(tri, g1, preferred_element_type=F32) + jnp.dot(tri, g2, preferred_element_type=F32)
           + jnp.dot(tri, g3, preferred_element_type=F32))
    shape3 = (n_chunks, HGRN_CHUNK, gw)
    cum3 = cum.reshape(shape3)
    ref3 = cum3[:, ref_row:ref_row + 1, :]
    end3 = cum3[:, end_row:end_row + 1, :]
    rel = cum3 - ref3
    qa3 = q.reshape(shape3) * jnp.exp(rel)
    ka3 = k.reshape(shape3) * jnp.exp(-rel)
    qa = qa3.reshape(cum.shape).astype(BF16)
    ka = ka3.reshape(cum.shape).astype(BF16)
    qe = (qa3 * jnp.exp(ref3)).reshape(cum.shape).astype(BF16)
    kd = (ka3 * jnp.exp(end3 - ref3)).reshape(cum.shape).astype(BF16)
    dec = jnp.exp(end3)
    vb = v.astype(BF16)
    chunk_cols = cmask_ref[...]

    heads = []
    for h in range(n_heads):
        cols = slice(h * HEAD_DIM, (h + 1) * HEAD_DIM)
        attn = lax.dot_general(qa[:, cols], ka[:, cols], (((1,), (1,)), ((), ())), preferred_element_type=F32)
        attn = jnp.where(in_chunk, attn, 0.0)
        o_intra = jnp.dot(attn.astype(BF16), vb[:, cols], preferred_element_type=F32)
        kd_wide = jnp.concatenate([kd[:, cols]] * n_chunks, axis=1) * chunk_cols
        upd = jnp.dot(v[:, cols].T.astype(BF16), kd_wide, preferred_element_type=F32)
        heads.append(dict(o_intra=o_intra, upd=upd, state=st_scr[h]))

    order = range(n_chunks - 1, -1, -1) if rev else range(n_chunks)
    for c in order:
        rows = slice(c * HGRN_CHUNK, (c + 1) * HGRN_CHUNK)
        for h, hd in enumerate(heads):
            cols = slice(h * HEAD_DIM, (h + 1) * HEAD_DIM)
            inter = lax.dot_general(qe[rows, cols], hd["state"].astype(BF16), (((1,), (1,)), ((), ())),
                                    preferred_element_type=F32)
            o = hd["o_intra"][rows] + inter
            hd["state"] = hd["state"] * dec[c][:, cols] + hd["upd"][:, c * HEAD_DIM:(c + 1) * HEAD_DIM]
            if final:
                o = o + oprev_ref[rows, cols]
                y = o * lax.rsqrt(jnp.mean(o * o, axis=-1, keepdims=True) + EPS) * gain_ref[...]
                o = y * jax.nn.silu(g_ref[rows, cols])
            o_ref[rows, cols] = o.astype(o_ref.dtype)
    for h, hd in enumerate(heads):
        st_scr[h] = hd["state"]
        sfin_ref[0, h] = hd["state"]


def _hgrn_scan(p, lb_row, s0, batch, cols, rev, o_prev=None, out_gain=None):
    t = p.shape[0] // batch
    nsb = t // HGRN_BLOCK
    n_heads = s0.shape[1]
    hps = next(g for g in (HGRN_HEADS_PER_STEP, 4, 2, 1)
               if n_heads % g == 0 and all(c % (g * HEAD_DIM) == 0 for c in cols))
    gw = hps * HEAD_DIM
    final = o_prev is not None
    q0, z0, v0, g0 = (c // gw for c in cols)

    def row_blk(b, s):
        return b * nsb + (nsb - 1 - s if rev else s)

    blk = (HGRN_BLOCK, gw)
    st_blk = (1, hps, HEAD_DIM, HEAD_DIM)
    in_specs = [
        pl.BlockSpec(blk, lambda b, h, s: (row_blk(b, s), q0 + h)),
        pl.BlockSpec(blk, lambda b, h, s: (row_blk(b, s), z0 + h)),
        pl.BlockSpec(blk, lambda b, h, s: (row_blk(b, s), v0 + h)),
        pl.BlockSpec((1, gw), lambda b, h, s: (0, h)),
        pl.BlockSpec((HGRN_BLOCK, HGRN_BLOCK), lambda b, h, s: (0, 0)),
        pl.BlockSpec((HGRN_BLOCK, (HGRN_BLOCK // HGRN_CHUNK) * HEAD_DIM), lambda b, h, s: (0, 0)),
        pl.BlockSpec(st_blk, lambda b, h, s: (b, h, 0, 0)),
    ]
    args = [p, p, p, lb_row, _chunk_tri(rev), _chunk_cols(), s0]
    if final:
        in_specs += [
            pl.BlockSpec(blk, lambda b, h, s: (row_blk(b, s), h)),
            pl.BlockSpec(blk, lambda b, h, s: (row_blk(b, s), g0 + h)),
            pl.BlockSpec((1, HEAD_DIM), lambda b, h, s: (0, 0)),
        ]
        args += [o_prev, p, out_gain.reshape(1, HEAD_DIM)]
    return pl.pallas_call(
        functools.partial(_hgrn_kernel, rev=rev, final=final, n_heads=hps),
        grid=(batch, n_heads // hps, nsb),
        in_specs=in_specs,
        out_specs=[
            pl.BlockSpec(blk, lambda b, h, s: (row_blk(b, s), h)),
            pl.BlockSpec(st_blk, lambda b, h, s: (b, h, 0, 0)),
        ],
        out_shape=[
            jax.ShapeDtypeStruct((batch * t, n_heads * HEAD_DIM), BF16 if final else F32),
            jax.ShapeDtypeStruct(s0.shape, F32),
        ],
        scratch_shapes=[pltpu.VMEM((hps, HEAD_DIM, HEAD_DIM), F32)],
        compiler_params=_params("parallel", "parallel", "arbitrary"),
        name="hgrn_bwd" if rev else "hgrn_fwd",
    )(*args)


def _conv_kernel(b_ref, c_ref, u_ref, w_ref, o_ref):
    u = c_ref[...] * u_ref[...]
    t = u.shape[0]
    row = lax.broadcasted_iota(jnp.int32, u.shape, 0)
    prev = jnp.where(row == 0, 0.0, pltpu.roll(u, 1, 0))
    nxt = jnp.where(row == t - 1, 0.0, pltpu.roll(u, t - 1, 0))
    y = prev * w_ref[0:1, :] + u * w_ref[1:2, :] + nxt * w_ref[2:3, :]
    o_ref[...] = (b_ref[...] * y).astype(o_ref.dtype)


def _gated_conv(p, conv_w, batch):
    d = conv_w.shape[1]
    t = p.shape[0] // batch
    td = _pick(d, (256, 128))
    nd = d // td
    return pl.pallas_call(
        _conv_kernel,
        grid=(batch, nd),
        in_specs=[
            pl.BlockSpec((t, td), lambda b, j: (b, j)),
            pl.BlockSpec((t, td), lambda b, j: (b, nd + j)),
            pl.BlockSpec((t, td), lambda b, j: (b, 2 * nd + j)),
            pl.BlockSpec((conv_w.shape[0], td), lambda b, j: (0, j)),
        ],
        out_specs=pl.BlockSpec((t, td), lambda b, j: (b, j)),
        out_shape=jax.ShapeDtypeStruct((batch * t, d), BF16),
        compiler_params=_params("parallel", "parallel"),
        name="gated_conv3",
    )(p, p, p, conv_w)


def kernel(x, c, ctx, c_ctx, mod_w, mod_b, norm_g, ffn_w1, ffn_w2, ab_w_in, ab_w_out, attn_q_gain, attn_k_gain, hgrn_lb_logits, hgrn_out_gain, conv_w_in, conv_w, conv_w_out):
    batch, seq, d = x.shape
    ctx_len = ctx.shape[1]
    depth = mod_w.shape[0]
    attn_w = d // 2
    kv_w = attn_w // KV_GROUP
    hg_w = d - attn_w
    n_hg_heads = hg_w // HEAD_DIM
    col_q, col_k, col_v = 0, attn_w, attn_w + kv_w
    col_qb = attn_w + 2 * kv_w
    col_zf, col_zb, col_ib, col_gb = (col_qb + hg_w * i for i in range(1, 5))

    lb_table = jnp.cumsum(jax.nn.softmax(hgrn_lb_logits.astype(F32), axis=1), axis=1)
    tables = _rope_tables(seq)

    pad = (-(batch + 1)) % 8
    cc = jnp.concatenate([c, c_ctx[None], jnp.zeros((pad, d), F32)], axis=0)
    mod_all = _mod_all(cc, mod_w, mod_b).reshape(depth, batch + 1 + pad, N_MOD, d)

    xl = x.reshape(batch * seq, d)
    xc = ctx.reshape(batch * ctx_len, d)
    n_ctx = batch * ctx_len

    n_lat = batch * seq
    f_ff = ffn_w2.shape[2]
    tn_up, tn_down = _up_tn(n_lat, d, f_ff), _down_tn(seq, f_ff, d)
    needs = {}
    for layer in range(depth):
        mix_in, mix_out = (ab_w_in, ab_w_out) if layer % 2 == 0 else (conv_w_in, conv_w_out)
        needs[("w1", layer, 0)] = (ffn_w1, (layer, 0), tn_up)
        needs[("w2", layer, 0)] = (ffn_w2, (layer, 0), tn_down)
        needs[("in", layer)] = (mix_in, (layer // 2,), _proj_tn(n_lat, d, mix_in.shape[2]))
        needs[("out", layer)] = (mix_out, (layer // 2,), _down_tn(seq, mix_out.shape[1], d))
        needs[("w1", layer, 1)] = (ffn_w1, (layer, 1), tn_up)
        needs[("w2", layer, 1)] = (ffn_w2, (layer, 1), tn_down)
    ready = {}

    def weight(key):
        if key not in ready:
            ready[key] = _weight_bf16(*needs[key])
        return ready[key]

    def hosted(call, w_key, tiles=1):
        w = weight(w_key)
        steps = (n_lat // _row_tile(seq)) * (w.shape[0] // tiles)
        budget = HOST_CAST_FRACTION * steps * tiles * w.shape[1] * w.shape[2] * BF16_BYTES
        keys = []
        for key, (src, _, _) in needs.items():
            size = src.shape[-2] * src.shape[-1] * F32_BYTES
            if key not in ready and size <= budget:
                keys.append(key)
                budget -= size
        out, casts = call([needs[k] for k in keys])
        ready.update(zip(keys, casts))
        return out

    def ffn(xs, mod, gain, layer, idx, rows3, gate_row, rpm, latent):
        h = _adaln(xs, mod, gain, rows3, rpm)
        k1, k2 = ("w1", layer, idx), ("w2", layer, idx)
        if latent:
            a = hosted(lambda jobs: _ffn_up(h, weight(k1), jobs), k1, tiles=2)
            return hosted(lambda jobs: _down([a], weight(k2), xs, mod, gate_row, 0.5, rpm, "ffn_down", jobs), k2)
        a, _ = _ffn_up(h, weight(k1))
        return _down([a], weight(k2), xs, mod, gate_row, 0.5, rpm, "ffn_down")[0]

    for layer in range(depth):
        last = layer == depth - 1
        even = layer % 2 == 0
        ctx_needed = even or not last
        gain = norm_g[layer]
        mod_l = mod_all[layer, :batch]
        mod_c = mod_all[layer, batch:batch + 1]

        xl = ffn(xl, mod_l, gain, layer, 0, (0, 0, 1), 2, seq, True)
        if ctx_needed:
            xc = ffn(xc, mod_c, gain, layer, 0, (0, 0, 1), 2, n_ctx, False)

        k_in, k_out = ("in", layer), ("out", layer)
        hl = _adaln(xl, mod_l, gain, (1, 3, 4), seq)
        if even:
            e = layer // 2
            pl_ = hosted(lambda jobs: _proj(hl, weight(k_in), "ab_proj", jobs), k_in)
            pc_, _ = _proj(_adaln(xc, mod_c, gain, (1, 3, 4), n_ctx), weight(k_in), "ab_proj")
            o_attn = _attention(pl_, [pl_, pc_], attn_q_gain[e], attn_k_gain[e], tables, batch, col_q, col_k, col_v)
            s_zero = jnp.zeros((batch, n_hg_heads, HEAD_DIM, HEAD_DIM), F32)
            lbs = [lb_table[dd, layer].reshape(1, hg_w) for dd in range(2)]
            zcols = (col_zf, col_zb)
            sc_dir = []
            o_prev = None
            for dd in range(2):
                o_prev, s_c = _hgrn_scan(pc_, lbs[dd], s_zero, batch, (col_qb, zcols[dd], col_ib, col_gb), dd == 1,
                                         o_prev, hgrn_out_gain[e] if dd == 1 else None)
                sc_dir.append(s_c)
            o_rec_c = o_prev
            o_prev = None
            for dd in range(2):
                o_prev, _ = _hgrn_scan(pl_, lbs[dd], sc_dir[dd], batch, (col_qb, zcols[dd], col_ib, col_gb), dd == 1,
                                       o_prev, hgrn_out_gain[e] if dd == 1 else None)
            mixed = [o_attn, o_prev]
            if not last:
                o_attn_c = _attention(pc_, [pc_], attn_q_gain[e], attn_k_gain[e], None, batch, col_q, col_k, col_v)
                mixed_c = [o_attn_c, o_rec_c]
        else:
            o = layer // 2
            pl_ = hosted(lambda jobs: _proj(hl, weight(k_in), "conv_proj", jobs), k_in)
            mixed = [_gated_conv(pl_, conv_w[o], batch)]
            if not last:
                pc_, _ = _proj(_adaln(xc, mod_c, gain, (1, 3, 4), n_ctx), weight(k_in), "conv_proj")
                mixed_c = [_gated_conv(pc_, conv_w[o], batch)]
        x_res = xl
        xl = hosted(lambda jobs: _down(mixed, weight(k_out), x_res, mod_l, 5, 1.0, seq, "mixer_out", jobs), k_out)
        if not last:
            xc = _down(mixed_c, weight(k_out), xc, mod_c, 5, 1.0, n_ctx, "mixer_out")[0]

        xl = ffn(xl, mod_l, gain, layer, 1, (2, 6, 7), 8, seq, True)
        if not last:
            xc = ffn(xc, mod_c, gain, layer, 1, (2, 6, 7), 8, n_ctx, False)

    return xl.reshape(batch, seq, d)
```
